```python
import math
import jax, jax.numpy as jnp
from jax import lax
import numpy as np

D_MODEL = 1024
BATCH = 1
SEQ = 16384
DEPTH = 4

D_MIX = D_MODEL
HEAD_DIM = 64
D_M = D_MIX // 4
M_HEADS = D_M // HEAD_DIM
M_CHUNK = 64
D_A = D_MIX // 4
A_HEADS = D_A // HEAD_DIM
IDX_HEADS = 4
IDX_DIM = 64
TOPK_MAX = 256
Q_BLOCK = 128
D_S = D_MIX // 2
S_HEADS = D_S // HEAD_DIM
S_GROUPS = 2
S_STATE = 128
S_CONV = 4
S_CHUNK = 64
CONV_DIM = D_S + 2 * S_GROUPS * S_STATE
ROPE_THETA = 500000.0
ROPE_DIM = HEAD_DIM // 4
D_FF = -(-8 * D_MODEL // (3 * 256)) * 256
EPS = 1e-6
N_IN = (4 * D_M + 2 * M_HEADS) + (3 * D_A + IDX_HEADS * IDX_DIM + IDX_DIM + IDX_HEADS) + (D_S + CONV_DIM + S_HEADS)

kernel_name = "hymba_style_mlstm_dsa_ssd_hybrid"


def _split_points():
    sizes = (D_M, D_M, D_M, D_M, M_HEADS, M_HEADS,
             D_A, D_A, D_A, IDX_HEADS * IDX_DIM, IDX_DIM, IDX_HEADS,
             D_S, CONV_DIM, S_HEADS)
    return [int(v) for v in np.cumsum(sizes)[:-1]]


def rms_norm(x, w):
    xf = x.astype(jnp.float32)
    y = xf * lax.rsqrt(jnp.mean(xf * xf, axis=-1, keepdims=True) + EPS)
    return (y * w.astype(jnp.float32)).astype(x.dtype)


def rope_partial(x, positions):
    half = ROPE_DIM // 2
    inv_freq = jnp.power(jnp.float32(ROPE_THETA), -jnp.arange(half, dtype=jnp.float32) / half)
    ang = positions.astype(jnp.float32)[..., None] * inv_freq
    cos = jnp.cos(ang)[:, :, None, :]
    sin = jnp.sin(ang)[:, :, None, :]
    x1 = x[..., :half].astype(jnp.float32)
    x2 = x[..., half:ROPE_DIM].astype(jnp.float32)
    rot = jnp.concatenate([x1 * cos - x2 * sin, x2 * cos + x1 * sin], axis=-1).astype(x.dtype)
    return jnp.concatenate([rot, x[..., ROPE_DIM:]], axis=-1)


def mlstm_chunkwise(q, k, v, i_pre, f_pre):
    B, S, H, d = q.shape
    L = M_CHUNK
    nc = S // L
    f32 = jnp.float32

    def to_chunks(t):
        t = t.reshape(B, nc, L, H, *t.shape[3:])
        return jnp.moveaxis(t, (1, 3), (0, 2))

    qc = to_chunks(q.astype(f32) * (d ** -0.5))
    kc = to_chunks(k.astype(f32))
    vc = to_chunks(v.astype(f32))
    ic = to_chunks(i_pre.astype(f32))
    lfc = to_chunks(jax.nn.log_sigmoid(f_pre.astype(f32)))
    causal = jnp.tril(jnp.ones((L, L), dtype=bool))

    def step(carry, inp):
        C, n, m = carry
        qb, kb, vb, ib, lfb = inp
        b = jnp.cumsum(lfb, axis=-1)
        log_d = jnp.where(causal, b[..., :, None] - b[..., None, :] + ib[..., None, :], -jnp.inf)
        m_inter = b + m[..., None]
        m_t = jnp.maximum(m_inter, jnp.max(log_d, axis=-1))
        s = jnp.einsum('bhtd,bhsd->bhts', qb, kb) * jnp.exp(log_d - m_t[..., None])
        scale = jnp.exp(m_inter - m_t)
        num = (jnp.einsum('bhts,bhsd->bhtd', s, vb)
               + scale[..., None] * jnp.einsum('bhtd,bhde->bhte', qb, C))
        den = jnp.sum(s, axis=-1) + scale * jnp.einsum('bhtd,bhd->bht', qb, n)
        h = num / jnp.maximum(jnp.abs(den), jnp.exp(-m_t))[..., None]
        b_last = b[..., -1]
        log_w = b_last[..., None] - b + ib
        m_new = jnp.maximum(b_last + m, jnp.max(log_w, axis=-1))
        w = jnp.exp(log_w - m_new[..., None])
        decay = jnp.exp(b_last + m - m_new)
        C_new = decay[..., None, None] * C + jnp.einsum('bhs,bhsd,bhse->bhde', w, kb, vb)
        n_new = decay[..., None] * n + jnp.einsum('bhs,bhsd->bhd', w, kb)
        return (C_new, n_new, m_new), h

    init = (jnp.zeros((B, H, d, d), f32), jnp.zeros((B, H, d), f32), jnp.zeros((B, H), f32))
    _, hc = lax.scan(step, init, (qc, kc, vc, ic, lfc))
    return jnp.moveaxis(hc, (0, 2), (1, 3)).reshape(B, S, H, d)


def dsa_attention(q, k, v, q_idx, k_idx, w_idx):
    B, S, H, d = q.shape
    top_k = min(TOPK_MAX, S // 4)
    nb = S // Q_BLOCK
    f32 = jnp.float32
    key_pos = jnp.arange(S)
    k_idx_f = k_idx.astype(f32)

    def blocks(t):
        return jnp.moveaxis(t.reshape(B, nb, Q_BLOCK, *t.shape[2:]), 1, 0)

    def one_block(inp):
        qb, qib, wb, qpos = inp
        dots = jnp.einsum('bthd,bsd->bths', qib.astype(f32), k_idx_f) * (IDX_DIM ** -0.5)
        score = jnp.einsum('bth,bths->bts', wb.astype(f32), jax.nn.relu(dots))
        causal = key_pos[None, :] <= qpos[:, None]
        score = jnp.where(causal[None], score, -jnp.inf)
        _, sel = lax.top_k(score, top_k)
        valid = sel <= qpos[None, :, None]
        k_sel = jax.vmap(lambda kk, ii: kk[ii])(k, sel)
        v_sel = jax.vmap(lambda vv, ii: vv[ii])(v, sel)
        logits = jnp.einsum('bthd,btkhd->bthk', qb.astype(f32), k_sel.astype(f32)) * (d ** -0.5)
        logits = jnp.where(valid[:, :, None, :], logits, -jnp.inf)
        p = jax.nn.softmax(logits, axis=-1)
        return jnp.einsum('bthk,btkhd->bthd', p, v_sel.astype(f32))

    out = lax.map(one_block, (blocks(q), blocks(q_idx), blocks(w_idx),
                              jnp.arange(S).reshape(nb, Q_BLOCK)))
    return jnp.moveaxis(out, 0, 1).reshape(B, S, H, d)


def ssd_chunked(x, dt, A, Bh, Ch):
    B, S, H, P = x.shape
    N = Bh.shape[-1]
    L = S_CHUNK
    nc = S // L

    def to_chunks(t):
        return jnp.moveaxis(t.reshape(B, nc, L, *t.shape[2:]), 1, 0)

    causal = jnp.tril(jnp.ones((L, L), dtype=bool))

    def step(state, inp):
        xb, dtb, Bb, Cb = inp
        acum = jnp.cumsum(dtb * A, axis=1)
        seg = jnp.where(causal[None, :, :, None], acum[:, :, None, :] - acum[:, None, :, :], -jnp.inf)
        scores = jnp.einsum('bthn,bshn->btsh', Cb, Bb) * jnp.exp(seg)
        y_intra = jnp.einsum('btsh,bshp->bthp', scores, xb * dtb[..., None])
        y_inter = jnp.einsum('bthn,bhpn->bthp', Cb, state) * jnp.exp(acum)[..., None]
        a_last = acum[:, -1]
        w = jnp.exp(a_last[:, None, :] - acum) * dtb
        state_new = (state * jnp.exp(a_last)[..., None, None]
                     + jnp.einsum('bshn,bsh,bshp->bhpn', Bb, w, xb))
        return state_new, y_intra + y_inter

    init = jnp.zeros((B, H, P, N), jnp.float32)
    _, yc = lax.scan(step, init, (to_chunks(x), to_chunks(dt), to_chunks(Bh), to_chunks(Ch)))
    return jnp.moveaxis(yc, 0, 1).reshape(B, S, H, P)


def ssd_mixer(xbc_raw, z, dt_raw, conv_w, conv_b, dt_bias, a_log, d_skip, norm_w):
    B, S, _ = xbc_raw.shape
    f32 = jnp.float32
    xbc = lax.conv_general_dilated(
        xbc_raw, conv_w[:, None, :], window_strides=(1,), padding=[(S_CONV - 1, 0)],
        dimension_numbers=('NWC', 'WIO', 'NWC'), feature_group_count=CONV_DIM) + conv_b
    xbc = jax.nn.silu(xbc.astype(f32))
    xs, Bm, Cm = jnp.split(xbc, [D_S, D_S + S_GROUPS * S_STATE], axis=-1)
    xs = xs.reshape(B, S, S_HEADS, HEAD_DIM)
    rep = S_HEADS // S_GROUPS
    Bh = jnp.repeat(Bm.reshape(B, S, S_GROUPS, S_STATE), rep, axis=2)
    Ch = jnp.repeat(Cm.reshape(B, S, S_GROUPS, S_STATE), rep, axis=2)
    dt = jax.nn.softplus(dt_raw.astype(f32) + dt_bias.astype(f32))
    A = -jnp.exp(a_log.astype(f32))
    y = ssd_chunked(xs, dt, A, Bh, Ch) + d_skip.astype(f32)[:, None] * xs
    gated = y.reshape(B, S, D_S) * jax.nn.silu(z.astype(f32))
    out = rms_norm(gated.reshape(B, S, S_GROUPS, D_S // S_GROUPS), norm_w.reshape(S_GROUPS, -1))
    return out.reshape(B, S, D_S)


def hybrid_layer(x, positions, norm_w, w_in, gate_b, m_norm_w, conv_w, conv_b,
                 dt_bias, a_log, d_skip, s_norm_w, w_out, w_gate, w_up, w_down):
    B, S, _ = x.shape
    h = rms_norm(x, norm_w[0])
    proj = h @ w_in
    (mq, mk, mv, mo, mi, mf, aq, ak, av, iq, ik, iw, sz, sxbc, sdt) = jnp.split(proj, _split_points(), axis=-1)

    def heads(t, n):
        return t.reshape(B, S, n, -1)

    hm = mlstm_chunkwise(heads(mq, M_HEADS), heads(mk, M_HEADS), heads(mv, M_HEADS),
                         mi + gate_b[:M_HEADS], mf + gate_b[M_HEADS:])
    hm = rms_norm(hm, m_norm_w.reshape(M_HEADS, HEAD_DIM)).reshape(B, S, D_M)
    hm = (jax.nn.sigmoid(mo.astype(jnp.float32)) * hm).astype(x.dtype)

    aq_r = rope_partial(heads(aq, A_HEADS), positions)
    ak_r = rope_partial(heads(ak, A_HEADS), positions)
    iq_r = rope_partial(heads(iq, IDX_HEADS), positions)
    ik_r = rope_partial(ik[:, :, None, :], positions)[:, :, 0, :]
    ha = dsa_attention(aq_r, ak_r, heads(av, A_HEADS), iq_r, ik_r, iw * (IDX_HEADS ** -0.5))
    ha = ha.reshape(B, S, D_A).astype(x.dtype)

    hs = ssd_mixer(sxbc, sz, sdt, conv_w, conv_b, dt_bias, a_log, d_skip, s_norm_w).astype(x.dtype)

    mix = jnp.concatenate([hm, ha, hs], axis=-1) @ w_out
    x = x + rms_norm(mix, norm_w[1])
    h = rms_norm(x, norm_w[2])
    ff = (jax.nn.silu(h @ w_gate) * (h @ w_up)) @ w_down
    return x + rms_norm(ff, norm_w[3])


def setup_inputs(seed: int = 0) -> dict:
    key = jax.random.key(seed)
    ks = jax.random.split(key, 18)
    f32 = jnp.float32
    nrm = lambda k, shape, s: jax.random.normal(k, shape, f32) * s
    x = jax.random.normal(ks[0], (BATCH, SEQ, D_MODEL), f32)
    positions = jnp.broadcast_to(jnp.arange(SEQ, dtype=jnp.int32), (BATCH, SEQ))
    norm_w = 1.0 + nrm(ks[1], (DEPTH, 4, D_MODEL), 0.02)
    w_in = nrm(ks[2], (DEPTH, D_MODEL, N_IN), D_MODEL ** -0.5)
    i_bias = nrm(ks[3], (DEPTH, M_HEADS), 0.1)
    f_bias = jax.random.uniform(ks[4], (DEPTH, M_HEADS), f32, 3.0, 6.0)
    mlstm_gate_bias = jnp.concatenate([i_bias, f_bias], axis=-1)
    mlstm_norm_w = 1.0 + nrm(ks[5], (DEPTH, D_M), 0.02)
    conv_w = nrm(ks[6], (DEPTH, S_CONV, CONV_DIM), S_CONV ** -0.5)
    conv_b = nrm(ks[7], (DEPTH, CONV_DIM), 0.02)
    dt0 = jnp.exp(jax.random.uniform(ks[8], (DEPTH, S_HEADS), f32, math.log(1e-3), math.log(1e-1)))
    dt_bias = dt0 + jnp.log(-jnp.expm1(-dt0))
    a_log = jnp.log(jax.random.uniform(ks[9], (DEPTH, S_HEADS), f32, 1.0, 16.0))
    d_skip = 1.0 + nrm(ks[10], (DEPTH, S_HEADS), 0.1)
    ssd_norm_w = 1.0 + nrm(ks[11], (DEPTH, D_S), 0.02)
    w_out = nrm(ks[12], (DEPTH, D_MIX, D_MODEL), D_MIX ** -0.5)
    w_gate = nrm(ks[13], (DEPTH, D_MODEL, D_FF), D_MODEL ** -0.5)
    w_up = nrm(ks[14], (DEPTH, D_MODEL, D_FF), D_MODEL ** -0.5)
    w_down = nrm(ks[15], (DEPTH, D_FF, D_MODEL), D_FF ** -0.5)
    return {"x": x, "positions": positions, "norm_w": norm_w, "w_in": w_in,
            "mlstm_gate_bias": mlstm_gate_bias, "mlstm_norm_w": mlstm_norm_w,
            "conv_w": conv_w, "conv_b": conv_b, "dt_bias": dt_bias, "a_log": a_log,
            "d_skip": d_skip, "ssd_norm_w": ssd_norm_w, "w_out": w_out,
            "w_gate": w_gate, "w_up": w_up, "w_down": w_down}


def reference(x, positions, norm_w, w_in, mlstm_gate_bias, mlstm_norm_w, conv_w, conv_b,
              dt_bias, a_log, d_skip, ssd_norm_w, w_out, w_gate, w_up, w_down):
    for l in range(DEPTH):
        x = hybrid_layer(x, positions, norm_w[l], w_in[l], mlstm_gate_bias[l], mlstm_norm_w[l],
                         conv_w[l], conv_b[l], dt_bias[l], a_log[l], d_skip[l], ssd_norm_w[l],
                         w_out[l], w_gate[l], w_up[l], w_down[l])
    return x
```

```python
import functools
import math

import numpy as np
import jax
import jax.numpy as jnp
from jax import lax
from jax.experimental import pallas as pl
from jax.experimental.pallas import tpu as pltpu

F32 = jnp.float32
BF16 = jnp.bfloat16
I32 = jnp.int32

D_MODEL = 1024
HEAD_DIM = 64
D_M = 256
M_HEADS = 4
D_A = 256
A_HEADS = 4
IDX_HEADS = 4
IDX_DIM = 64
TOPK_MAX = 256
D_S = 512
S_HEADS = 8
S_GROUPS = 2
S_STATE = 128
S_CONV = 4
CONV_DIM = D_S + 2 * S_GROUPS * S_STATE
ROPE_THETA = 500000.0
ROPE_DIM = HEAD_DIM // 4
ROPE_HALF = ROPE_DIM // 2
D_FF = 2816
EPS = 1e-6

LANES = 128
SUBLANES = 8
VMEM_LIMIT = 56 * 1024 * 1024

C_M = 0
C_A = C_M + 4 * D_M
C_Z = C_A + 4 * D_A
C_XBC = C_Z + D_S
C_MG = C_XBC + CONV_DIM
C_IK = C_MG + LANES
C_DT = C_IK + LANES
N_P = C_DT + LANES

INT_MIN = -2 ** 31
NEG_BIG = -1e30


def _params(sem):
    return pltpu.CompilerParams(dimension_semantics=sem, vmem_limit_bytes=VMEM_LIMIT)


def _resident(shape, index_map):
    return pl.BlockSpec(shape, index_map, pipeline_mode=pl.Buffered(1))


def _split3(x):
    h = x.astype(BF16)
    r = x - h.astype(F32)
    m = r.astype(BF16)
    lo = (r - m.astype(F32)).astype(BF16)
    return h, m, lo


def _cumsum_cols(tril, x):
    return sum(jnp.dot(tril, t, preferred_element_type=F32) for t in _split3(x))


def _cumsum_rows(x, triu):
    return sum(jnp.dot(t, triu, preferred_element_type=F32) for t in _split3(x))


def _tri(L):
    row = lax.broadcasted_iota(I32, (L, L), 0)
    col = lax.broadcasted_iota(I32, (L, L), 1)
    causal = col <= row
    tril = jnp.where(causal, 1.0, 0.0).astype(BF16)
    triu = jnp.where(row <= col, 1.0, 0.0).astype(BF16)
    return causal, tril, triu


def _rope_tables_kernel(pos_ref, inv_ref, cf_ref, sa_ref, sb_ref):
    ang = pos_ref[...].astype(F32) * inv_ref[...]
    c = jnp.cos(ang)
    s = jnp.sin(ang)
    j = lax.broadcasted_iota(I32, ang.shape, 1) & (HEAD_DIM - 1)
    cf_ref[...] = jnp.where(j < ROPE_DIM, c, 1.0)
    sa_ref[...] = jnp.where(j < ROPE_HALF, -s, 0.0)
    sb_ref[...] = jnp.where(j < ROPE_HALF, 0.0, jnp.where(j < ROPE_DIM, s, 0.0))


def _rope_tables(positions, S):
    tb = min(S, 1024)
    inv = np.power(np.float32(ROPE_THETA), -np.arange(ROPE_HALF, dtype=np.float32) / np.float32(ROPE_HALF))
    lane = np.arange(LANES) % HEAD_DIM
    inv_lanes = np.where(lane < ROPE_DIM, inv[lane % ROPE_HALF], np.float32(0)).astype(np.float32)[None, :]
    tab = jax.ShapeDtypeStruct((S, LANES), F32)
    row = pl.BlockSpec((tb, LANES), lambda i: (i, 0))
    return pl.pallas_call(
        _rope_tables_kernel,
        grid=(S // tb,),
        in_specs=[pl.BlockSpec((tb, 1), lambda i: (i, 0)), pl.BlockSpec((1, LANES), lambda i: (0, 0))],
        out_specs=[row, row, row],
        out_shape=[tab, tab, tab],
        compiler_params=_params(("arbitrary",)),
        name="rope_tables",
    )(positions.reshape(S, 1), jnp.asarray(inv_lanes))


def _inproj_kernel(x_ref, nw_ref, w_ref, cf_ref, sa_ref, sb_ref,
                   om_ref, omg_ref, oq_ref, oiq_ref, oiw_ref, os_ref, odt_ref,
                   akT_ref, avT_ref, ikT_ref):
    x = x_ref[...]
    ms = jnp.mean(x * x, axis=-1, keepdims=True)
    h = (x * lax.rsqrt(ms + EPS) * nw_ref[...]).astype(BF16)

    def proj(c0, n):
        return jnp.dot(h, w_ref[:, c0:c0 + n], preferred_element_type=F32)

    om_ref[...] = proj(C_M, 4 * D_M)
    os_ref[...] = proj(C_Z, D_S + CONV_DIM)
    omg_ref[...] = proj(C_MG, LANES)
    odt_ref[...] = proj(C_DT, LANES)

    cf = cf_ref[...]
    sa = sa_ref[...]
    sb = sb_ref[...]

    def rope(c, cf=cf, sa=sa, sb=sb):
        return c * cf + pltpu.roll(c, LANES - ROPE_HALF, 1) * sa + pltpu.roll(c, ROPE_HALF, 1) * sb

    def rope2(a2):
        return jnp.concatenate([rope(a2[:, :LANES]), rope(a2[:, LANES:])], axis=1)

    a = proj(C_A, 4 * D_A)
    oq_ref[...] = (rope2(a[:, 0:D_A]) * (HEAD_DIM ** -0.5)).astype(BF16)
    akT_ref[...] = rope2(a[:, D_A:2 * D_A]).T.astype(BF16)
    avT_ref[...] = a[:, 2 * D_A:3 * D_A].T.astype(BF16)
    oiq_ref[...] = rope2(a[:, 3 * D_A:4 * D_A]).astype(BF16)

    ikw = proj(C_IK, LANES)
    is_ik = lax.broadcasted_iota(I32, ikw.shape, 1) < IDX_DIM
    ikr = rope(ikw, jnp.where(is_ik, cf, 1.0), jnp.where(is_ik, sa, 0.0), jnp.where(is_ik, sb, 0.0))
    oiw_ref[...] = ikr
    ikT_ref[...] = ikr.T[0:IDX_DIM, :].astype(BF16)


def _inproj(x2, nw, wp, cf, sa, sb, S, tm):
    row = lambda n: pl.BlockSpec((tm, n), lambda i: (i, 0))
    colT = lambda n: pl.BlockSpec((n, tm), lambda i: (0, i))
    f = lambda n, dt=F32: jax.ShapeDtypeStruct((S, n), dt)
    fT = lambda n: jax.ShapeDtypeStruct((n, S), BF16)
    return pl.pallas_call(
        _inproj_kernel,
        grid=(S // tm,),
        in_specs=[row(D_MODEL), pl.BlockSpec((1, D_MODEL), lambda i: (0, 0)),
                  _resident((D_MODEL, N_P), lambda i: (0, 0)),
                  row(LANES), row(LANES), row(LANES)],
        out_specs=[row(4 * D_M), row(LANES), row(D_A), row(D_A), row(LANES),
                   row(D_S + CONV_DIM), row(LANES), colT(D_A), colT(D_A), colT(IDX_DIM)],
        out_shape=[f(4 * D_M), f(LANES), f(D_A, BF16), f(D_A, BF16), f(LANES),
                   f(D_S + CONV_DIM), f(LANES), fT(D_A), fT(D_A), fT(IDX_DIM)],
        compiler_params=_params(("arbitrary",)),
        name="inproj",
    )(x2, nw, wp, cf, sa, sb)


def _mlstm_kernel(om_ref, omg_ref, gb_ref, nw_ref, o_ref, C_ref, n_ref, m_ref):
    L = om_ref.shape[0]

    @pl.when(pl.program_id(0) == 0)
    def _():
        C_ref[...] = jnp.zeros_like(C_ref)
        n_ref[...] = jnp.zeros_like(n_ref)
        m_ref[...] = jnp.zeros_like(m_ref)

    causal, tril, triu = _tri(L)
    G = omg_ref[...] + gb_ref[...]
    LF = jnp.minimum(G, 0.0) - jnp.log1p(jnp.exp(-jnp.abs(G)))
    Bc = _cumsum_cols(tril, LF)
    GT = G.T
    BrT = _cumsum_rows(LF.T, triu)
    nw = nw_ref[...]
    outs = []
    for h in range(M_HEADS):
        lo = HEAD_DIM * (h % 2)

        def head(c0, h=h, lo=lo):
            blk = om_ref[:, pl.ds(c0 + LANES * (h // 2), LANES)]
            return blk[:, lo:lo + HEAD_DIM]

        qf = head(0) * (HEAD_DIM ** -0.5)
        kf = head(D_M)
        vf = head(2 * D_M)
        og = head(3 * D_M)
        q = qf.astype(BF16)
        k = kf.astype(BF16)
        v = vf.astype(BF16)
        f_l = M_HEADS + h
        b_col = Bc[:, f_l:f_l + 1]
        b_row = BrT[f_l:f_l + 1, :]
        i_col = G[:, h:h + 1]
        i_row = GT[h:h + 1, :]
        m_prev = m_ref[h:h + 1, 0:1]
        logd = jnp.where(causal, b_col + (i_row - b_row), -jnp.inf)
        m_inter = b_col + m_prev
        m_t = jnp.maximum(m_inter, jnp.max(logd, axis=-1, keepdims=True))
        s = lax.dot_general(q, k, (((1,), (1,)), ((), ())), preferred_element_type=F32) * jnp.exp(logd - m_t)
        scale = jnp.exp(m_inter - m_t)
        C_prev = C_ref[h]
        n_prev = n_ref[h:h + 1, :]
        num = (jnp.dot(s.astype(BF16), v, preferred_element_type=F32)
               + scale * jnp.dot(q, C_prev.astype(BF16), preferred_element_type=F32))
        den = jnp.sum(s, axis=-1, keepdims=True) + scale * jnp.sum(qf * n_prev, axis=-1, keepdims=True)
        hh = num / jnp.maximum(jnp.abs(den), jnp.exp(-m_t))
        b_last = Bc[L - 1:L, f_l:f_l + 1]
        logw_row = b_last - b_row + i_row
        m_new = jnp.maximum(b_last + m_prev, jnp.max(logw_row, axis=-1, keepdims=True))
        w_col = jnp.exp(b_last - b_col + i_col - m_new)
        decay = jnp.exp(b_last + m_prev - m_new)
        kw = kf * w_col
        C_ref[h] = decay * C_prev + lax.dot_general(kw.astype(BF16), v, (((0,), (0,)), ((), ())),
                                                    preferred_element_type=F32)
        n_ref[h:h + 1, :] = decay * n_prev + jnp.sum(kw, axis=0, keepdims=True)
        m_ref[h:h + 1, :] = jnp.broadcast_to(m_new, (1, LANES))
        y = hh * lax.rsqrt(jnp.mean(hh * hh, axis=-1, keepdims=True) + EPS)
        outs.append(jax.nn.sigmoid(og) * y)
    o_ref[...] = (jnp.concatenate(outs, axis=1) * nw).astype(BF16)


def _mlstm(om, omg, gb, nw, S, L):
    return pl.pallas_call(
        _mlstm_kernel,
        grid=(S // L,),
        in_specs=[pl.BlockSpec((L, 4 * D_M), lambda i: (i, 0)), pl.BlockSpec((L, LANES), lambda i: (i, 0)),
                  pl.BlockSpec((1, LANES), lambda i: (0, 0)), pl.BlockSpec((1, D_M), lambda i: (0, 0))],
        out_specs=pl.BlockSpec((L, D_M), lambda i: (i, 0)),
        out_shape=jax.ShapeDtypeStruct((S, D_M), BF16),
        scratch_shapes=[pltpu.VMEM((M_HEADS, HEAD_DIM, HEAD_DIM), F32),
                        pltpu.VMEM((SUBLANES, HEAD_DIM), F32),
                        pltpu.VMEM((SUBLANES, LANES), F32)],
        compiler_params=_params(("arbitrary",)),
        name="mlstm",
    )(om, omg, gb, nw)


def _ssd_kernel(os_ref, odt_ref, cw_ref, cb_ref, dtb_ref, alog_ref, dsk_ref, nw_ref, o_ref,
                carry_ref, st_ref):
    L = os_ref.shape[0]

    @pl.when(pl.program_id(0) == 0)
    def _():
        carry_ref[...] = jnp.zeros_like(carry_ref)
        st_ref[...] = jnp.zeros_like(st_ref)

    causal, tril, triu = _tri(L)
    z = os_ref[:, 0:D_S]
    raw = os_ref[:, D_S:D_S + CONV_DIM]
    ext = jnp.concatenate([carry_ref[...], raw], axis=0)
    cw = cw_ref[...]
    xbc = cb_ref[...] + cw[S_CONV - 1:S_CONV, :] * raw
    for j in range(S_CONV - 1):
        off = SUBLANES - (S_CONV - 1) + j
        xbc = xbc + cw[j:j + 1, :] * ext[off:off + L, :]
    carry_ref[...] = raw[L - SUBLANES:L, :]
    xbc = xbc * jax.nn.sigmoid(xbc)

    dtr = odt_ref[...] + dtb_ref[...]
    DT = jnp.maximum(dtr, 0.0) + jnp.log1p(jnp.exp(-jnp.abs(dtr)))
    dA = DT * (-jnp.exp(alog_ref[...]))
    Ac = _cumsum_cols(tril, dA)
    ArT = _cumsum_rows(dA.T, triu)
    DTT = DT.T
    hpg = S_HEADS // S_GROUPS
    ys = []
    for g in range(S_GROUPS):
        Bg = xbc[:, D_S + S_STATE * g:D_S + S_STATE * (g + 1)]
        Cg = xbc[:, D_S + S_GROUPS * S_STATE + S_STATE * g:D_S + S_GROUPS * S_STATE + S_STATE * (g + 1)]
        Cb = Cg.astype(BF16)
        CB = lax.dot_general(Cb, Bg.astype(BF16), (((1,), (1,)), ((), ())), preferred_element_type=F32)
        BgT = Bg.T
        for hh in range(hpg):
            h = g * hpg + hh
            xpair = xbc[:, LANES * (h // 2):LANES * (h // 2 + 1)]
            xh = xpair[:, HEAD_DIM * (h % 2):HEAD_DIM * (h % 2 + 1)]
            ac_col = Ac[:, h:h + 1]
            ac_row = ArT[h:h + 1, :]
            dec = jnp.exp(jnp.where(causal, ac_col - ac_row, -jnp.inf))
            sc = (CB * dec).astype(BF16)
            xdt = (xh * DT[:, h:h + 1]).astype(BF16)
            st = st_ref[h]
            y = (jnp.dot(sc, xdt, preferred_element_type=F32)
                 + jnp.dot(Cb, st.astype(BF16), preferred_element_type=F32) * jnp.exp(ac_col))
            a_last = Ac[L - 1:L, h:h + 1]
            w_row = jnp.exp(a_last - ac_row) * DTT[h:h + 1, :]
            st_ref[h] = st * jnp.exp(a_last) + jnp.dot((BgT * w_row).astype(BF16), xh.astype(BF16),
                                                       preferred_element_type=F32)
            ys.append(y)
    Y = jnp.concatenate(ys, axis=1) + dsk_ref[...] * xbc[:, 0:D_S]
    gated = Y * (z * jax.nn.sigmoid(z))
    gw = D_S // S_GROUPS
    outs = []
    for g in range(S_GROUPS):
        gg = gated[:, gw * g:gw * (g + 1)]
        outs.append(gg * lax.rsqrt(jnp.mean(gg * gg, axis=-1, keepdims=True) + EPS))
    o_ref[...] = (jnp.concatenate(outs, axis=1) * nw_ref[...]).astype(BF16)


def _ssd(os_, odt, cw, cb, dtb, alog, dsk, nw, S, L):
    full = lambda r, c: pl.BlockSpec((r, c), lambda i: (0, 0))
    return pl.pallas_call(
        _ssd_kernel,
        grid=(S // L,),
        in_specs=[pl.BlockSpec((L, D_S + CONV_DIM), lambda i: (i, 0)), pl.BlockSpec((L, LANES), lambda i: (i, 0)),
                  full(S_CONV, CONV_DIM), full(1, CONV_DIM), full(1, LANES), full(1, LANES),
                  full(1, D_S), full(1, D_S)],
        out_specs=pl.BlockSpec((L, D_S), lambda i: (i, 0)),
        out_shape=jax.ShapeDtypeStruct((S, D_S), BF16),
        scratch_shapes=[pltpu.VMEM((SUBLANES, CONV_DIM), F32),
                        pltpu.VMEM((S_HEADS, S_STATE, HEAD_DIM), F32)],
        compiler_params=_params(("arbitrary",)),
        name="ssd",
    )(os_, odt, cw, cb, dtb, alog, dsk, nw)


def _dsa_kernel(q_ref, iq_ref, iw_ref, ikT_ref, akT_ref, avT_ref, o_ref, key_ref, *, tk, top_k):
    TQ = q_ref.shape[0]
    q0 = pl.program_id(0) * TQ
    nkb = (q0 + TQ + tk - 1) // tk
    qpos = q0 + lax.broadcasted_iota(I32, (TQ, 1), 0)

    iq = iq_ref[...]
    iwb = iw_ref[...]
    w_scale = (IDX_HEADS ** -0.5) * (IDX_DIM ** -0.5)
    wcol = [iwb[:, IDX_DIM + h:IDX_DIM + h + 1] * w_scale for h in range(IDX_HEADS)]
    iqh = [iq[:, IDX_DIM * h:IDX_DIM * (h + 1)] for h in range(IDX_HEADS)]

    def p1(kb, c):
        k0 = pl.multiple_of(kb * tk, tk)
        ikt = ikT_ref[:, pl.ds(k0, tk)]
        sc = jnp.zeros((TQ, tk), F32)
        for h in range(IDX_HEADS):
            d = jnp.dot(iqh[h], ikt, preferred_element_type=F32)
            sc = sc + wcol[h] * jnp.maximum(d, 0.0)
        bits = lax.bitcast_convert_type(sc, I32)
        key = jnp.where(bits < 0, INT_MIN - bits, bits)
        kpos = k0 + lax.broadcasted_iota(I32, (TQ, tk), 1)
        key_ref[:, pl.ds(k0, tk)] = jnp.where(kpos <= qpos, key, INT_MIN)
        return c

    lax.fori_loop(0, nkb, p1, 0)

    def count_ge(thr):
        thr_b = jnp.broadcast_to(thr, (TQ, LANES))

        def body(kb, acc):
            k0 = pl.multiple_of(kb * tk, tk)
            for c in range(tk // LANES):
                blk = key_ref[:, pl.ds(k0 + c * LANES, LANES)]
                acc = acc + jnp.where(blk >= thr_b, 1.0, 0.0)
            return acc

        acc = lax.fori_loop(0, nkb, body, jnp.zeros((TQ, LANES), F32))
        return jnp.sum(acc, axis=1, keepdims=True)

    kf = float(top_k)

    def bit_body(it, prefix):
        cand = prefix | jnp.left_shift(jnp.int32(1), 31 - it)
        cnt = count_ge(cand ^ INT_MIN)
        return jnp.where(cnt >= kf, cand, prefix)

    prefix = lax.fori_loop(0, 32, bit_body, jnp.zeros((TQ, 1), I32))
    tau = prefix ^ INT_MIN
    r = jnp.where(tau == INT_MIN, 0.0, kf - count_ge(tau + 1))

    rowi = lax.broadcasted_iota(I32, (tk, tk), 0)
    coli = lax.broadcasted_iota(I32, (tk, tk), 1)
    triu = jnp.where(rowi <= coli, 1.0, 0.0).astype(BF16)
    qa = q_ref[...]
    qh = [qa[:, HEAD_DIM * h:HEAD_DIM * (h + 1)] for h in range(A_HEADS)]

    def p3(kb, carry):
        cnt, ms, ls, accs = carry
        k0 = pl.multiple_of(kb * tk, tk)
        key = key_ref[:, pl.ds(k0, tk)]
        gt = key > tau
        eq = key == tau
        pref = jnp.dot(jnp.where(eq, 1.0, 0.0).astype(BF16), triu, preferred_element_type=F32) + cnt
        u = jnp.where(gt, 0.0, jnp.where(eq, pref, 1e9))
        sel = u <= r
        cnt = pref[:, tk - 1:tk]
        ms2, ls2, accs2 = [], [], []
        for h in range(A_HEADS):
            akt = akT_ref[pl.ds(HEAD_DIM * h, HEAD_DIM), pl.ds(k0, tk)]
            avt = avT_ref[pl.ds(HEAD_DIM * h, HEAD_DIM), pl.ds(k0, tk)]
            lg = jnp.where(sel, jnp.dot(qh[h], akt, preferred_element_type=F32), NEG_BIG)
            m_new = jnp.maximum(ms[h], jnp.max(lg, axis=1, keepdims=True))
            alpha = jnp.exp(ms[h] - m_new)
            p = jnp.where(sel, jnp.exp(lg - m_new), 0.0)
            ls2.append(alpha * ls[h] + jnp.sum(p, axis=1, keepdims=True))
            pv = lax.dot_general(p.astype(BF16), avt, (((1,), (1,)), ((), ())), preferred_element_type=F32)
            accs2.append(alpha * accs[h] + pv)
            ms2.append(m_new)
        return cnt, tuple(ms2), tuple(ls2), tuple(accs2)

    init = (jnp.zeros((TQ, 1), F32),
            tuple(jnp.full((TQ, 1), NEG_BIG, F32) for _ in range(A_HEADS)),
            tuple(jnp.zeros((TQ, 1), F32) for _ in range(A_HEADS)),
            tuple(jnp.zeros((TQ, HEAD_DIM), F32) for _ in range(A_HEADS)))
    _, _, ls, accs = lax.fori_loop(0, nkb, p3, init)
    o_ref[...] = jnp.concatenate([accs[h] / ls[h] for h in range(A_HEADS)], axis=1).astype(BF16)


def _dsa(oq, oiq, oiw, ikT, akT, avT, S, tq, tk):
    top_k = min(TOPK_MAX, S // 4)
    return pl.pallas_call(
        functools.partial(_dsa_kernel, tk=tk, top_k=top_k),
        grid=(S // tq,),
        in_specs=[pl.BlockSpec((tq, D_A), lambda i: (i, 0)), pl.BlockSpec((tq, D_A), lambda i: (i, 0)),
                  pl.BlockSpec((tq, LANES), lambda i: (i, 0)),
                  _resident((IDX_DIM, S), lambda i: (0, 0)),
                  _resident((D_A, S), lambda i: (0, 0)),
                  _resident((D_A, S), lambda i: (0, 0))],
        out_specs=pl.BlockSpec((tq, D_A), lambda i: (i, 0)),
        out_shape=jax.ShapeDtypeStruct((S, D_A), BF16),
        scratch_shapes=[pltpu.VMEM((tq, S), I32)],
        compiler_params=_params(("arbitrary",)),
        name="dsa",
    )(oq, oiq, oiw, ikT, akT, avT)


FF_CHUNKS = ((0, 768), (768, 768), (1536, 768), (2304, 512))


def _rms(v, w):
    return v * lax.rsqrt(jnp.mean(v * v, axis=-1, keepdims=True) + EPS) * w


def _outffn_kernel(x_ref, hm_ref, ha_ref, hs_ref, nw_ref, wo_ref, wg_ref, wu_ref, wd_ref, o_ref):
    mix = (jnp.dot(hm_ref[...], wo_ref[0:D_M, :], preferred_element_type=F32)
           + jnp.dot(ha_ref[...], wo_ref[D_M:D_M + D_A, :], preferred_element_type=F32)
           + jnp.dot(hs_ref[...], wo_ref[D_M + D_A:, :], preferred_element_type=F32))
    x1 = x_ref[...] + _rms(mix, nw_ref[1:2, :])
    h2 = _rms(x1, nw_ref[2:3, :]).astype(BF16)
    ff = jnp.zeros_like(x1)
    for c0, n in FF_CHUNKS:
        g = jnp.dot(h2, wg_ref[:, c0:c0 + n], preferred_element_type=F32)
        u = jnp.dot(h2, wu_ref[:, c0:c0 + n], preferred_element_type=F32)
        act = (g * jax.nn.sigmoid(g) * u).astype(BF16)
        ff = ff + jnp.dot(act, wd_ref[c0:c0 + n, :], preferred_element_type=F32)
    o_ref[...] = x1 + _rms(ff, nw_ref[3:4, :])


def _outffn(x2, hm, ha, hs, nw4, wo, wg, wu, wd, S, tm):
    row = lambda n: pl.BlockSpec((tm, n), lambda i: (i, 0))
    return pl.pallas_call(
        _outffn_kernel,
        grid=(S // tm,),
        in_specs=[row(D_MODEL), row(D_M), row(D_A), row(D_S),
                  pl.BlockSpec((4, D_MODEL), lambda i: (0, 0)),
                  _resident((D_MODEL, D_MODEL), lambda i: (0, 0)),
                  _resident((D_MODEL, D_FF), lambda i: (0, 0)),
                  _resident((D_MODEL, D_FF), lambda i: (0, 0)),
                  _resident((D_FF, D_MODEL), lambda i: (0, 0))],
        out_specs=row(D_MODEL),
        out_shape=jax.ShapeDtypeStruct((S, D_MODEL), F32),
        compiler_params=_params(("arbitrary",)),
        name="outffn",
    )(x2, hm, ha, hs, nw4, wo, wg, wu, wd)


def _plan(S):
    return dict(tm=min(S, 512), lm=min(S, 256), ls=min(S, 256), tq=min(S, 128), tk=min(S, 512))


def _pad_lanes(v):
    return jnp.pad(v, [(0, 0)] * (v.ndim - 1) + [(0, LANES - v.shape[-1])])


def kernel(x, positions, norm_w, w_in, mlstm_gate_bias, mlstm_norm_w, conv_w, conv_b, dt_bias, a_log,
           d_skip, ssd_norm_w, w_out, w_gate, w_up, w_down):
    B, S, D = x.shape
    assert B == 1 and D == D_MODEL
    depth = w_in.shape[0]
    plan = _plan(S)

    o = np.cumsum([0, D_M, D_M, D_M, D_M, M_HEADS, M_HEADS, D_A, D_A, D_A, IDX_HEADS * IDX_DIM, IDX_DIM,
                   IDX_HEADS, D_S, CONV_DIM, S_HEADS])
    wp = jnp.concatenate([
        w_in[:, :, o[0]:o[4]], w_in[:, :, o[6]:o[10]], w_in[:, :, o[12]:o[13]], w_in[:, :, o[13]:o[14]],
        _pad_lanes(w_in[:, :, o[4]:o[6]]), _pad_lanes(w_in[:, :, o[10]:o[12]]), _pad_lanes(w_in[:, :, o[14]:o[15]]),
    ], axis=-1).astype(BF16)
    wo = w_out.astype(BF16)
    wg = w_gate.astype(BF16)
    wu = w_up.astype(BF16)
    wd = w_down.astype(BF16)
    gb = _pad_lanes(mlstm_gate_bias[:, None, :])
    dtb = _pad_lanes(dt_bias[:, None, :])
    alog = _pad_lanes(a_log[:, None, :])
    dsk = jnp.repeat(d_skip, HEAD_DIM, axis=-1)[:, None, :]

    cf, sa, sb = _rope_tables(positions.astype(I32), S)
    x2 = x.reshape(S, D)
    for l in range(depth):
        om, omg, oq, oiq, oiw, os_, odt, akT, avT, ikT = _inproj(
            x2, norm_w[l, 0:1], wp[l], cf, sa, sb, S, plan["tm"])
        hm = _mlstm(om, omg, gb[l], mlstm_norm_w[l][None, :], S, plan["lm"])
        hs = _ssd(os_, odt, conv_w[l], conv_b[l][None, :], dtb[l], alog[l], dsk[l], ssd_norm_w[l][None, :],
                  S, plan["ls"])
        ha = _dsa(oq, oiq, oiw, ikT, akT, avT, S, plan["tq"], plan["tk"])
        x2 = _outffn(x2, hm, ha, hs, norm_w[l], wo[l], wg[l], wu[l], wd[l], S, plan["tm"])
    return x2.reshape(B, S, D)
```

```python
import functools

import numpy as np
import jax
import jax.numpy as jnp
from jax import lax
from jax.experimental import pallas as pl
from jax.experimental.pallas import tpu as pltpu

F32 = jnp.float32
BF16 = jnp.bfloat16
I32 = jnp.int32

D_MODEL = 1024
HEAD_DIM = 64
D_M = 256
M_HEADS = 4
D_A = 256
A_HEADS = 4
IDX_HEADS = 4
IDX_DIM = 64
TOPK_MAX = 256
D_S = 512
S_HEADS = 8
S_GROUPS = 2
S_STATE = 128
S_CONV = 4
CONV_DIM = D_S + 2 * S_GROUPS * S_STATE
ROPE_THETA = 500000.0
ROPE_DIM = HEAD_DIM // 4
ROPE_HALF = ROPE_DIM // 2
D_FF = 2816
EPS = 1e-6

LANES = 128
SUBLANES = 8
VMEM_LIMIT = 56 * 1024 * 1024

C_M = 0
C_A = C_M + 4 * D_M
C_Z = C_A + 4 * D_A
C_XBC = C_Z + D_S
C_MG = C_XBC + CONV_DIM
C_IK = C_MG + LANES
C_DT = C_IK + LANES
N_P = C_DT + LANES

INT_MIN = -2 ** 31
NEG_BIG = -1e30
M_FLOOR = -1e29
LOG2E = 1.4426950408889634

DSA_KB = 512
DSA_AB = 256
DSA_GROUP = 16
DSA_GB = DSA_KB // DSA_GROUP
DSA_GCHUNK = 128
DSA_MAX_STEPS = 40


def _params(sem):
    return pltpu.CompilerParams(dimension_semantics=sem, vmem_limit_bytes=VMEM_LIMIT)


def _resident(shape, index_map):
    return pl.BlockSpec(shape, index_map, pipeline_mode=pl.Buffered(1))


def _split3(x):
    h = x.astype(BF16)
    r = x - h.astype(F32)
    m = r.astype(BF16)
    lo = (r - m.astype(F32)).astype(BF16)
    return h, m, lo


def _cumsum_cols(tril, x):
    return sum(jnp.dot(tril, t, preferred_element_type=F32) for t in _split3(x))


def _cumsum_rows(x, triu):
    return sum(jnp.dot(t, triu, preferred_element_type=F32) for t in _split3(x))


def _tri(L):
    row = lax.broadcasted_iota(I32, (L, L), 0)
    col = lax.broadcasted_iota(I32, (L, L), 1)
    causal = col <= row
    tril = jnp.where(causal, 1.0, 0.0).astype(BF16)
    triu = jnp.where(row <= col, 1.0, 0.0).astype(BF16)
    return causal, tril, triu


def _rope_tables_kernel(pos_ref, inv_ref, cf_ref, sa_ref, sb_ref):
    ang = pos_ref[...].astype(F32) * inv_ref[...]
    c = jnp.cos(ang)
    s = jnp.sin(ang)
    j = lax.broadcasted_iota(I32, ang.shape, 1) & (HEAD_DIM - 1)
    cf_ref[...] = jnp.where(j < ROPE_DIM, c, 1.0)
    sa_ref[...] = jnp.where(j < ROPE_HALF, -s, 0.0)
    sb_ref[...] = jnp.where(j < ROPE_HALF, 0.0, jnp.where(j < ROPE_DIM, s, 0.0))


def _rope_tables(positions, S):
    tb = min(S, 1024)
    inv = np.power(np.float32(ROPE_THETA), -np.arange(ROPE_HALF, dtype=np.float32) / np.float32(ROPE_HALF))
    lane = np.arange(LANES) % HEAD_DIM
    inv_lanes = np.where(lane < ROPE_DIM, inv[lane % ROPE_HALF], np.float32(0)).astype(np.float32)[None, :]
    tab = jax.ShapeDtypeStruct((S, LANES), F32)
    row = pl.BlockSpec((tb, LANES), lambda i: (i, 0))
    return pl.pallas_call(
        _rope_tables_kernel,
        grid=(S // tb,),
        in_specs=[pl.BlockSpec((tb, 1), lambda i: (i, 0)), pl.BlockSpec((1, LANES), lambda i: (0, 0))],
        out_specs=[row, row, row],
        out_shape=[tab, tab, tab],
        compiler_params=_params(("arbitrary",)),
        name="rope_tables",
    )(positions.reshape(S, 1), jnp.asarray(inv_lanes))


def _inproj_kernel(x_ref, nw_ref, w_ref, cf_ref, sa_ref, sb_ref,
                   om_ref, omg_ref, os_ref, odt_ref,
                   aqT_ref, iqT_ref, iwT_ref, ik_ref, ak_ref, avT_ref):
    x = x_ref[...]
    ms = jnp.mean(x * x, axis=-1, keepdims=True)
    h = (x * lax.rsqrt(ms + EPS) * nw_ref[...]).astype(BF16)

    def proj(c0, n):
        return jnp.dot(h, w_ref[:, c0:c0 + n], preferred_element_type=F32)

    om_ref[...] = proj(C_M, 4 * D_M)
    os_ref[...] = proj(C_Z, D_S + CONV_DIM)
    omg_ref[...] = proj(C_MG, LANES)
    odt_ref[...] = proj(C_DT, LANES)

    cf = cf_ref[...]
    sa = sa_ref[...]
    sb = sb_ref[...]

    def rope(c, cf=cf, sa=sa, sb=sb):
        return c * cf + pltpu.roll(c, LANES - ROPE_HALF, 1) * sa + pltpu.roll(c, ROPE_HALF, 1) * sb

    def rope2(a2):
        return jnp.concatenate([rope(a2[:, :LANES]), rope(a2[:, LANES:])], axis=1)

    a = proj(C_A, 4 * D_A)
    aqT_ref[...] = (rope2(a[:, 0:D_A]) * (HEAD_DIM ** -0.5 * LOG2E)).T.astype(BF16)
    ak_ref[...] = rope2(a[:, D_A:2 * D_A]).astype(BF16)
    avT_ref[...] = a[:, 2 * D_A:3 * D_A].T.astype(BF16)
    iqT_ref[...] = rope2(a[:, 3 * D_A:4 * D_A]).T.astype(BF16)

    ikw = proj(C_IK, LANES)
    is_ik = lax.broadcasted_iota(I32, ikw.shape, 1) < IDX_DIM
    ikr = rope(ikw, jnp.where(is_ik, cf, 1.0), jnp.where(is_ik, sa, 0.0), jnp.where(is_ik, sb, 0.0))
    ik_ref[...] = jnp.where(is_ik, ikr, 0.0).astype(BF16)
    iwT_ref[...] = ikw.T[IDX_DIM:IDX_DIM + SUBLANES, :]


def _inproj(x2, nw, wp, cf, sa, sb, S, tm):
    row = lambda n: pl.BlockSpec((tm, n), lambda i: (i, 0))
    colT = lambda n: pl.BlockSpec((n, tm), lambda i: (0, i))
    f = lambda n, dt=F32: jax.ShapeDtypeStruct((S, n), dt)
    fT = lambda n, dt=BF16: jax.ShapeDtypeStruct((n, S), dt)
    return pl.pallas_call(
        _inproj_kernel,
        grid=(S // tm,),
        in_specs=[row(D_MODEL), pl.BlockSpec((1, D_MODEL), lambda i: (0, 0)),
                  _resident((D_MODEL, N_P), lambda i: (0, 0)),
                  row(LANES), row(LANES), row(LANES)],
        out_specs=[row(4 * D_M), row(LANES), row(D_S + CONV_DIM), row(LANES),
                   colT(D_A), colT(D_A), colT(SUBLANES), row(LANES), row(D_A), colT(D_A)],
        out_shape=[f(4 * D_M), f(LANES), f(D_S + CONV_DIM), f(LANES),
                   fT(D_A), fT(D_A), fT(SUBLANES, F32), f(LANES, BF16), f(D_A, BF16), fT(D_A)],
        compiler_params=_params(("arbitrary",)),
        name="inproj",
    )(x2, nw, wp, cf, sa, sb)


def _mlstm_kernel(om_ref, omg_ref, gb_ref, nw_ref, o_ref, C_ref, n_ref, m_ref):
    L = om_ref.shape[0]

    @pl.when(pl.program_id(0) == 0)
    def _():
        C_ref[...] = jnp.zeros_like(C_ref)
        n_ref[...] = jnp.zeros_like(n_ref)
        m_ref[...] = jnp.zeros_like(m_ref)

    causal, tril, triu = _tri(L)
    G = omg_ref[...] + gb_ref[...]
    LF = jnp.minimum(G, 0.0) - jnp.log1p(jnp.exp(-jnp.abs(G)))
    Bc = _cumsum_cols(tril, LF)
    GT = G.T
    BrT = _cumsum_rows(LF.T, triu)
    nw = nw_ref[...]
    outs = []
    for h in range(M_HEADS):
        lo = HEAD_DIM * (h % 2)

        def head(c0, h=h, lo=lo):
            blk = om_ref[:, pl.ds(c0 + LANES * (h // 2), LANES)]
            return blk[:, lo:lo + HEAD_DIM]

        qf = head(0) * (HEAD_DIM ** -0.5)
        kf = head(D_M)
        vf = head(2 * D_M)
        og = head(3 * D_M)
        q = qf.astype(BF16)
        k = kf.astype(BF16)
        v = vf.astype(BF16)
        f_l = M_HEADS + h
        b_col = Bc[:, f_l:f_l + 1]
        b_row = BrT[f_l:f_l + 1, :]
        i_col = G[:, h:h + 1]
        i_row = GT[h:h + 1, :]
        m_prev = m_ref[h:h + 1, 0:1]
        logd = jnp.where(causal, b_col + (i_row - b_row), -jnp.inf)
        m_inter = b_col + m_prev
        m_t = jnp.maximum(m_inter, jnp.max(logd, axis=-1, keepdims=True))
        s = lax.dot_general(q, k, (((1,), (1,)), ((), ())), preferred_element_type=F32) * jnp.exp(logd - m_t)
        scale = jnp.exp(m_inter - m_t)
        C_prev = C_ref[h]
        n_prev = n_ref[h:h + 1, :]
        num = (jnp.dot(s.astype(BF16), v, preferred_element_type=F32)
               + scale * jnp.dot(q, C_prev.astype(BF16), preferred_element_type=F32))
        den = jnp.sum(s, axis=-1, keepdims=True) + scale * jnp.sum(qf * n_prev, axis=-1, keepdims=True)
        hh = num / jnp.maximum(jnp.abs(den), jnp.exp(-m_t))
        b_last = Bc[L - 1:L, f_l:f_l + 1]
        logw_row = b_last - b_row + i_row
        m_new = jnp.maximum(b_last + m_prev, jnp.max(logw_row, axis=-1, keepdims=True))
        w_col = jnp.exp(b_last - b_col + i_col - m_new)
        decay = jnp.exp(b_last + m_prev - m_new)
        kw = kf * w_col
        C_ref[h] = decay * C_prev + lax.dot_general(kw.astype(BF16), v, (((0,), (0,)), ((), ())),
                                                    preferred_element_type=F32)
        n_ref[h:h + 1, :] = decay * n_prev + jnp.sum(kw, axis=0, keepdims=True)
        m_ref[h:h + 1, :] = jnp.broadcast_to(m_new, (1, LANES))
        y = hh * lax.rsqrt(jnp.mean(hh * hh, axis=-1, keepdims=True) + EPS)
        outs.append(jax.nn.sigmoid(og) * y)
    o_ref[...] = (jnp.concatenate(outs, axis=1) * nw).astype(BF16)


def _mlstm(om, omg, gb, nw, S, L):
    return pl.pallas_call(
        _mlstm_kernel,
        grid=(S // L,),
        in_specs=[pl.BlockSpec((L, 4 * D_M), lambda i: (i, 0)), pl.BlockSpec((L, LANES), lambda i: (i, 0)),
                  pl.BlockSpec((1, LANES), lambda i: (0, 0)), pl.BlockSpec((1, D_M), lambda i: (0, 0))],
        out_specs=pl.BlockSpec((L, D_M), lambda i: (i, 0)),
        out_shape=jax.ShapeDtypeStruct((S, D_M), BF16),
        scratch_shapes=[pltpu.VMEM((M_HEADS, HEAD_DIM, HEAD_DIM), F32),
                        pltpu.VMEM((SUBLANES, HEAD_DIM), F32),
                        pltpu.VMEM((SUBLANES, LANES), F32)],
        compiler_params=_params(("arbitrary",)),
        name="mlstm",
    )(om, omg, gb, nw)


def _ssd_kernel(os_ref, odt_ref, cw_ref, cb_ref, dtb_ref, alog_ref, dsk_ref, nw_ref, o_ref,
                carry_ref, st_ref):
    L = os_ref.shape[0]

    @pl.when(pl.program_id(0) == 0)
    def _():
        carry_ref[...] = jnp.zeros_like(carry_ref)
        st_ref[...] = jnp.zeros_like(st_ref)

    causal, tril, triu = _tri(L)
    z = os_ref[:, 0:D_S]
    raw = os_ref[:, D_S:D_S + CONV_DIM]
    ext = jnp.concatenate([carry_ref[...], raw], axis=0)
    cw = cw_ref[...]
    xbc = cb_ref[...] + cw[S_CONV - 1:S_CONV, :] * raw
    for j in range(S_CONV - 1):
        off = SUBLANES - (S_CONV - 1) + j
        xbc = xbc + cw[j:j + 1, :] * ext[off:off + L, :]
    carry_ref[...] = raw[L - SUBLANES:L, :]
    xbc = xbc * jax.nn.sigmoid(xbc)

    dtr = odt_ref[...] + dtb_ref[...]
    DT = jnp.maximum(dtr, 0.0) + jnp.log1p(jnp.exp(-jnp.abs(dtr)))
    dA = DT * (-jnp.exp(alog_ref[...]))
    Ac = _cumsum_cols(tril, dA)
    ArT = _cumsum_rows(dA.T, triu)
    DTT = DT.T
    hpg = S_HEADS // S_GROUPS
    ys = []
    for g in range(S_GROUPS):
        Bg = xbc[:, D_S + S_STATE * g:D_S + S_STATE * (g + 1)]
        Cg = xbc[:, D_S + S_GROUPS * S_STATE + S_STATE * g:D_S + S_GROUPS * S_STATE + S_STATE * (g + 1)]
        Cb = Cg.astype(BF16)
        CB = lax.dot_general(Cb, Bg.astype(BF16), (((1,), (1,)), ((), ())), preferred_element_type=F32)
        BgT = Bg.T
        for hh in range(hpg):
            h = g * hpg + hh
            xpair = xbc[:, LANES * (h // 2):LANES * (h // 2 + 1)]
            xh = xpair[:, HEAD_DIM * (h % 2):HEAD_DIM * (h % 2 + 1)]
            ac_col = Ac[:, h:h + 1]
            ac_row = ArT[h:h + 1, :]
            dec = jnp.exp(jnp.where(causal, ac_col - ac_row, -jnp.inf))
            sc = (CB * dec).astype(BF16)
            xdt = (xh * DT[:, h:h + 1]).astype(BF16)
            st = st_ref[h]
            y = (jnp.dot(sc, xdt, preferred_element_type=F32)
                 + jnp.dot(Cb, st.astype(BF16), preferred_element_type=F32) * jnp.exp(ac_col))
            a_last = Ac[L - 1:L, h:h + 1]
            w_row = jnp.exp(a_last - ac_row) * DTT[h:h + 1, :]
            st_ref[h] = st * jnp.exp(a_last) + jnp.dot((BgT * w_row).astype(BF16), xh.astype(BF16),
                                                       preferred_element_type=F32)
            ys.append(y)
    Y = jnp.concatenate(ys, axis=1) + dsk_ref[...] * xbc[:, 0:D_S]
    gated = Y * (z * jax.nn.sigmoid(z))
    gw = D_S // S_GROUPS
    outs = []
    for g in range(S_GROUPS):
        gg = gated[:, gw * g:gw * (g + 1)]
        outs.append(gg * lax.rsqrt(jnp.mean(gg * gg, axis=-1, keepdims=True) + EPS))
    o_ref[...] = (jnp.concatenate(outs, axis=1) * nw_ref[...]).astype(BF16)


def _ssd(os_, odt, cw, cb, dtb, alog, dsk, nw, S, L):
    full = lambda r, c: pl.BlockSpec((r, c), lambda i: (0, 0))
    return pl.pallas_call(
        _ssd_kernel,
        grid=(S // L,),
        in_specs=[pl.BlockSpec((L, D_S + CONV_DIM), lambda i: (i, 0)), pl.BlockSpec((L, LANES), lambda i: (i, 0)),
                  full(S_CONV, CONV_DIM), full(1, CONV_DIM), full(1, LANES), full(1, LANES),
                  full(1, D_S), full(1, D_S)],
        out_specs=pl.BlockSpec((L, D_S), lambda i: (i, 0)),
        out_shape=jax.ShapeDtypeStruct((S, D_S), BF16),
        scratch_shapes=[pltpu.VMEM((SUBLANES, CONV_DIM), F32),
                        pltpu.VMEM((S_HEADS, S_STATE, HEAD_DIM), F32)],
        compiler_params=_params(("arbitrary",)),
        name="ssd",
    )(os_, odt, cw, cb, dtb, alog, dsk, nw)


def _count_ge(ref, nblk, rb, thr_row, tq):
    thr8 = jnp.broadcast_to(thr_row, (SUBLANES, tq))
    n_acc = 4

    def body(b, accs):
        slab = ref[pl.ds(pl.multiple_of(b * rb, rb), rb), :]
        accs = list(accs)
        for j in range(rb // SUBLANES):
            blk = slab[SUBLANES * j:SUBLANES * (j + 1)]
            a = accs[j % n_acc]
            accs[j % n_acc] = jnp.where(blk >= thr8, a + 1.0, a)
        return tuple(accs)

    accs = lax.fori_loop(0, nblk, body, tuple(jnp.zeros((SUBLANES, tq), F32) for _ in range(n_acc)))
    return jnp.sum((accs[0] + accs[1]) + (accs[2] + accs[3]), axis=0, keepdims=True)


def _avg_floor(lo, hi):
    return (lo >> 1) + (hi >> 1) + (lo & hi & 1)


def _dsa_kernel(aqT_ref, iqT_ref, iwT_ref, ik_ref, ak_ref, avT_ref, o_ref, key_ref, gm_ref, acc_ref, *, top_k):
    TQ = aqT_ref.shape[1]
    q0 = pl.program_id(0) * TQ
    nb = (q0 + TQ + DSA_KB - 1) // DSA_KB
    qpos = q0 + lax.broadcasted_iota(I32, (1, TQ), 1)
    kf = float(top_k)
    w_scale = (IDX_HEADS ** -0.5) * (IDX_DIM ** -0.5)

    iqT = iqT_ref[...]
    zpad = jnp.zeros((LANES - IDX_DIM, TQ), BF16)
    iq_pad = [jnp.concatenate([iqT[IDX_DIM * h:IDX_DIM * (h + 1)], zpad], axis=0) for h in range(IDX_HEADS)]
    w_rows = [iwT_ref[h:h + 1, :] * w_scale for h in range(IDX_HEADS)]

    gm_ref[pl.ds(pl.multiple_of(nb * DSA_GB, DSA_GB), DSA_GCHUNK - DSA_GB), :] = jnp.full(
        (DSA_GCHUNK - DSA_GB, TQ), INT_MIN, I32)

    half = DSA_KB // 2

    def p1(b, c):
        gms = []
        for s2 in range(2):
            k0 = pl.multiple_of(b * DSA_KB + s2 * half, half)
            ikb = ik_ref[pl.ds(k0, half), :]
            sc = jnp.zeros((half, TQ), F32)
            for h in range(IDX_HEADS):
                d = jnp.dot(ikb, iq_pad[h], preferred_element_type=F32)
                sc = sc + w_rows[h] * jnp.maximum(d, 0.0)
            bits = lax.bitcast_convert_type(sc, I32)
            key = jnp.where(bits < 0, INT_MIN - bits, bits)
            kpos = k0 + lax.broadcasted_iota(I32, (half, 1), 0)
            key = jnp.where(kpos <= qpos, key, INT_MIN)
            key_ref[pl.ds(k0, half), :] = key
            span = DSA_GROUP * SUBLANES
            for g in range(half // span):
                m = key[span * g:span * g + SUBLANES]
                for j in range(1, DSA_GROUP):
                    m = jnp.maximum(m, key[span * g + SUBLANES * j:span * g + SUBLANES * (j + 1)])
                gms.append(m)
        gm_ref[pl.ds(pl.multiple_of(b * DSA_GB, DSA_GB), DSA_GB), :] = jnp.concatenate(gms, axis=0)
        return c

    lax.fori_loop(0, nb, p1, 0)

    ngc = (nb * DSA_GB + DSA_GCHUNK - 1) // DSA_GCHUNK

    def gm_bit(it, prefix):
        cand = prefix | jnp.left_shift(jnp.int32(1), 31 - it)
        cnt = _count_ge(gm_ref, ngc, DSA_GCHUNK, cand ^ INT_MIN, TQ)
        return jnp.where(cnt >= kf, cand, prefix)

    lo0 = lax.fori_loop(0, 32, gm_bit, jnp.zeros((1, TQ), I32)) ^ INT_MIN

    def gm_max(c, m):
        r0 = pl.multiple_of(c * DSA_GCHUNK, DSA_GCHUNK)
        for j in range(DSA_GCHUNK // SUBLANES):
            m = jnp.maximum(m, gm_ref[pl.ds(r0 + SUBLANES * j, SUBLANES), :])
        return m

    rmax = jnp.max(lax.fori_loop(0, ngc, gm_max, jnp.full((SUBLANES, TQ), INT_MIN, I32)), axis=0, keepdims=True)
    hi0 = rmax + 1

    def w_cond(st):
        return jnp.logical_and(st[0] < DSA_MAX_STEPS, st[3] > 0.0)

    def w_body(st):
        it, lo, hi, _ = st
        probe = jnp.where(it < 2, jnp.minimum(it, 1), INT_MIN)
        mid = jnp.where((lo < probe) & (probe < hi), probe, _avg_floor(lo, hi))
        c = _count_ge(key_ref, nb, DSA_KB, mid, TQ)
        ge = c >= kf
        lo2 = jnp.where(ge, mid, lo)
        hi2 = jnp.where(c == kf, mid + 1, jnp.where(ge, hi, mid))
        active = jnp.max(jnp.where(_avg_floor(lo2, hi2) != lo2, 1.0, 0.0))
        return it + 1, lo2, hi2, active

    _, tau, _, _ = lax.while_loop(w_cond, w_body, (jnp.int32(0), lo0, hi0, jnp.float32(1.0)))
    r = jnp.where(tau == INT_MIN, 0.0, kf - _count_ge(key_ref, nb, DSA_KB, tau + 1, TQ))

    AB = DSA_AB
    rowi = lax.broadcasted_iota(I32, (AB, AB), 0)
    coli = lax.broadcasted_iota(I32, (AB, AB), 1)
    tril = jnp.where(coli <= rowi, 1.0, 0.0).astype(BF16)
    aqT = aqT_ref[...]
    hrow = lax.broadcasted_iota(I32, (LANES, TQ), 0) // HEAD_DIM
    q_pad = []
    for h in range(A_HEADS):
        pair = aqT[LANES * (h // 2):LANES * (h // 2 + 1)]
        q_pad.append(jnp.where(hrow == (h % 2), pair, jnp.zeros_like(pair)))
    acc_ref[...] = jnp.zeros_like(acc_ref)

    nsub = DSA_KB // AB

    def p3(b, carry):
        cnt, ms, ls = carry
        k0 = pl.multiple_of(b * DSA_KB, DSA_KB)
        bias = []
        for s2 in range(nsub):
            key = key_ref[pl.ds(k0 + s2 * AB, AB), :]
            eq = key == tau
            pref = jnp.dot(tril, jnp.where(eq, 1.0, 0.0).astype(BF16), preferred_element_type=F32) + cnt
            sel = jnp.where(key > tau, 0.0, jnp.where(eq, pref, 1e9)) <= r
            bias.append(jnp.where(sel, 0.0, NEG_BIG))
            cnt = pref[AB - 1:AB, :]
        lgs = [[jnp.dot(ak_ref[pl.ds(k0 + s2 * AB, AB), pl.ds(LANES * (h // 2), LANES)], q_pad[h],
                        preferred_element_type=F32) for s2 in range(nsub)] for h in range(A_HEADS)]
        ms2, ls2, alphas, ps = [], [], [], []
        for h in range(A_HEADS):
            lm = [lgs[h][s2] + bias[s2] for s2 in range(nsub)]
            bm = functools.reduce(jnp.maximum, [jnp.max(t, axis=0, keepdims=True) for t in lm])
            m_new = jnp.maximum(ms[h], bm)
            alpha = jnp.exp2(ms[h] - m_new)
            p = [jnp.exp2(lm[s2] - m_new) for s2 in range(nsub)]
            ls2.append(alpha * ls[h] + sum(jnp.sum(t, axis=0, keepdims=True) for t in p))
            ps.append(jnp.concatenate([t.astype(BF16) for t in p], axis=0))
            ms2.append(m_new)
            alphas.append(alpha)
        for h in range(A_HEADS):
            avt = avT_ref[pl.ds(HEAD_DIM * h, HEAD_DIM), pl.ds(k0, DSA_KB)]
            pv = jnp.dot(avt, ps[h], preferred_element_type=F32)
            hs = pl.ds(HEAD_DIM * h, HEAD_DIM)
            acc_ref[hs, :] = alphas[h] * acc_ref[hs, :] + pv
        return cnt, tuple(ms2), tuple(ls2)

    init = (jnp.zeros((1, TQ), F32),
            tuple(jnp.full((1, TQ), M_FLOOR, F32) for _ in range(A_HEADS)),
            tuple(jnp.zeros((1, TQ), F32) for _ in range(A_HEADS)))
    _, _, ls = lax.fori_loop(0, nb, p3, init)
    outT = jnp.concatenate([acc_ref[pl.ds(HEAD_DIM * h, HEAD_DIM), :] / ls[h] for h in range(A_HEADS)], axis=0)
    o_ref[...] = outT.T.astype(BF16)


def _dsa(aqT, iqT, iwT, ik, ak, avT, S, tq):
    top_k = min(TOPK_MAX, S // 4)
    colT = lambda n: pl.BlockSpec((n, tq), lambda i: (0, i))
    return pl.pallas_call(
        functools.partial(_dsa_kernel, top_k=top_k),
        grid=(S // tq,),
        in_specs=[colT(D_A), colT(D_A), colT(SUBLANES),
                  _resident((S, LANES), lambda i: (0, 0)),
                  _resident((S, D_A), lambda i: (0, 0)),
                  _resident((D_A, S), lambda i: (0, 0))],
        out_specs=pl.BlockSpec((tq, D_A), lambda i: (i, 0)),
        out_shape=jax.ShapeDtypeStruct((S, D_A), BF16),
        scratch_shapes=[pltpu.VMEM((S, tq), I32),
                        pltpu.VMEM((S // DSA_GROUP + DSA_GCHUNK, tq), I32),
                        pltpu.VMEM((D_A, tq), F32)],
        compiler_params=_params(("arbitrary",)),
        name="dsa",
    )(aqT, iqT, iwT, ik, ak, avT)


FF_CHUNKS = ((0, 768), (768, 768), (1536, 768), (2304, 512))


def _rms(v, w):
    return v * lax.rsqrt(jnp.mean(v * v, axis=-1, keepdims=True) + EPS) * w


def _outffn_kernel(x_ref, hm_ref, ha_ref, hs_ref, nw_ref, wo_ref, wg_ref, wu_ref, wd_ref, o_ref):
    mix = (jnp.dot(hm_ref[...], wo_ref[0:D_M, :], preferred_element_type=F32)
           + jnp.dot(ha_ref[...], wo_ref[D_M:D_M + D_A, :], preferred_element_type=F32)
           + jnp.dot(hs_ref[...], wo_ref[D_M + D_A:, :], preferred_element_type=F32))
    x1 = x_ref[...] + _rms(mix, nw_ref[1:2, :])
    h2 = _rms(x1, nw_ref[2:3, :]).astype(BF16)
    ff = jnp.zeros_like(x1)
    for c0, n in FF_CHUNKS:
        g = jnp.dot(h2, wg_ref[:, c0:c0 + n], preferred_element_type=F32)
        u = jnp.dot(h2, wu_ref[:, c0:c0 + n], preferred_element_type=F32)
        act = (g * jax.nn.sigmoid(g) * u).astype(BF16)
        ff = ff + jnp.dot(act, wd_ref[c0:c0 + n, :], preferred_element_type=F32)
    o_ref[...] = x1 + _rms(ff, nw_ref[3:4, :])


def _outffn(x2, hm, ha, hs, nw4, wo, wg, wu, wd, S, tm):
    row = lambda n: pl.BlockSpec((tm, n), lambda i: (i, 0))
    return pl.pallas_call(
        _outffn_kernel,
        grid=(S // tm,),
        in_specs=[row(D_MODEL), row(D_M), row(D_A), row(D_S),
                  pl.BlockSpec((4, D_MODEL), lambda i: (0, 0)),
                  _resident((D_MODEL, D_MODEL), lambda i: (0, 0)),
                  _resident((D_MODEL, D_FF), lambda i: (0, 0)),
                  _resident((D_MODEL, D_FF), lambda i: (0, 0)),
                  _resident((D_FF, D_MODEL), lambda i: (0, 0))],
        out_specs=row(D_MODEL),
        out_shape=jax.ShapeDtypeStruct((S, D_MODEL), F32),
        compiler_params=_params(("arbitrary",)),
        name="outffn",
    )(x2, hm, ha, hs, nw4, wo, wg, wu, wd)


def _plan(S):
    assert S % DSA_KB == 0
    return dict(tm=min(S, 512), lm=min(S, 256), ls=min(S, 256), tq=min(S, 256))


def _pad_lanes(v):
    return jnp.pad(v, [(0, 0)] * (v.ndim - 1) + [(0, LANES - v.shape[-1])])


def kernel(x, positions, norm_w, w_in, mlstm_gate_bias, mlstm_norm_w, conv_w, conv_b, dt_bias, a_log,
           d_skip, ssd_norm_w, w_out, w_gate, w_up, w_down):
    B, S, D = x.shape
    assert B == 1 and D == D_MODEL
    depth = w_in.shape[0]
    plan = _plan(S)

    o = np.cumsum([0, D_M, D_M, D_M, D_M, M_HEADS, M_HEADS, D_A, D_A, D_A, IDX_HEADS * IDX_DIM, IDX_DIM,
                   IDX_HEADS, D_S, CONV_DIM, S_HEADS])
    wp = jnp.concatenate([
        w_in[:, :, o[0]:o[4]], w_in[:, :, o[6]:o[10]], w_in[:, :, o[12]:o[13]], w_in[:, :, o[13]:o[14]],
        _pad_lanes(w_in[:, :, o[4]:o[6]]), _pad_lanes(w_in[:, :, o[10]:o[12]]), _pad_lanes(w_in[:, :, o[14]:o[15]]),
    ], axis=-1).astype(BF16)
    wo = w_out.astype(BF16)
    wg = w_gate.astype(BF16)
    wu = w_up.astype(BF16)
    wd = w_down.astype(BF16)
    gb = _pad_lanes(mlstm_gate_bias[:, None, :])
    dtb = _pad_lanes(dt_bias[:, None, :])
    alog = _pad_lanes(a_log[:, None, :])
    dsk = jnp.repeat(d_skip, HEAD_DIM, axis=-1)[:, None, :]

    cf, sa, sb = _rope_tables(positions.astype(I32), S)
    x2 = x.reshape(S, D)
    for l in range(depth):
        om, omg, os_, odt, aqT, iqT, iwT, ik, ak, avT = _inproj(
            x2, norm_w[l, 0:1], wp[l], cf, sa, sb, S, plan["tm"])
        hm = _mlstm(om, omg, gb[l], mlstm_norm_w[l][None, :], S, plan["lm"])
        hs = _ssd(os_, odt, conv_w[l], conv_b[l][None, :], dtb[l], alog[l], dsk[l], ssd_norm_w[l][None, :],
                  S, plan["ls"])
        ha = _dsa(aqT, iqT, iwT, ik, ak, avT, S, plan["tq"])
        x2 = _outffn(x2, hm, ha, hs, norm_w[l], wo[l], wg[l], wu[l], wd[l], S, plan["tm"])
    return x2.reshape(B, S, D)
```

```python
import functools

import numpy as np
import jax
import jax.numpy as jnp
from jax import lax
from jax.experimental import pallas as pl
from jax.experimental.pallas import tpu as pltpu

F32 = jnp.float32
BF16 = jnp.bfloat16
I32 = jnp.int32

D_MODEL = 1024
HEAD_DIM = 64
D_M = 256
M_HEADS = 4
D_A = 256
A_HEADS = 4
IDX_HEADS = 4
IDX_DIM = 64
TOPK_MAX = 256
D_S = 512
S_HEADS = 8
S_GROUPS = 2
S_STATE = 128
S_CONV = 4
CONV_DIM = D_S + 2 * S_GROUPS * S_STATE
ROPE_THETA = 500000.0
ROPE_DIM = HEAD_DIM // 4
ROPE_HALF = ROPE_DIM // 2
D_FF = 2816
EPS = 1e-6

LANES = 128
SUBLANES = 8
PACK = 16
VMEM_LIMIT = 56 * 1024 * 1024

C_M = 0
C_A = C_M + 4 * D_M
C_Z = C_A + 4 * D_A
C_XBC = C_Z + D_S
C_MG = C_XBC + CONV_DIM
C_IK = C_MG + LANES
C_DT = C_IK + LANES
N_P = C_DT + LANES

INT_MIN = -2 ** 31
NEG_BIG = -1e30
M_FLOOR = -1e29
LOG2E = 1.4426950408889634

DSA_KB = 512
DSA_AB = 256
DSA_GROUP = 16
DSA_GB = DSA_KB // DSA_GROUP
DSA_GCHUNK = 128
V_EXT = HEAD_DIM + PACK
DSA_MAX_STEPS = 40
TIE_ALL = 1e6


def _params(sem):
    return pltpu.CompilerParams(dimension_semantics=sem, vmem_limit_bytes=VMEM_LIMIT)


def _resident(shape, index_map):
    return pl.BlockSpec(shape, index_map, pipeline_mode=pl.Buffered(1))


def _split3(x):
    h = x.astype(BF16)
    r = x - h.astype(F32)
    m = r.astype(BF16)
    lo = (r - m.astype(F32)).astype(BF16)
    return h, m, lo


def _cumsum_cols(tril, x):
    return sum(jnp.dot(tril, t, preferred_element_type=F32) for t in _split3(x))


def _cumsum_rows(x, triu):
    return sum(jnp.dot(t, triu, preferred_element_type=F32) for t in _split3(x))


def _tri(L):
    row = lax.broadcasted_iota(I32, (L, L), 0)
    col = lax.broadcasted_iota(I32, (L, L), 1)
    causal = col <= row
    tril = jnp.where(causal, 1.0, 0.0).astype(BF16)
    triu = jnp.where(row <= col, 1.0, 0.0).astype(BF16)
    return causal, tril, triu


def _rope_tables_kernel(pos_ref, inv_ref, cf_ref, sa_ref, sb_ref):
    ang = pos_ref[...].astype(F32) * inv_ref[...]
    c = jnp.cos(ang)
    s = jnp.sin(ang)
    j = lax.broadcasted_iota(I32, ang.shape, 1) & (HEAD_DIM - 1)
    cf_ref[...] = jnp.where(j < ROPE_DIM, c, 1.0)
    sa_ref[...] = jnp.where(j < ROPE_HALF, -s, 0.0)
    sb_ref[...] = jnp.where(j < ROPE_HALF, 0.0, jnp.where(j < ROPE_DIM, s, 0.0))


def _rope_tables(positions, S):
    tb = min(S, 1024)
    inv = np.power(np.float32(ROPE_THETA), -np.arange(ROPE_HALF, dtype=np.float32) / np.float32(ROPE_HALF))
    lane = np.arange(LANES) % HEAD_DIM
    inv_lanes = np.where(lane < ROPE_DIM, inv[lane % ROPE_HALF], np.float32(0)).astype(np.float32)[None, :]
    tab = jax.ShapeDtypeStruct((S, LANES), F32)
    row = pl.BlockSpec((tb, LANES), lambda i: (i, 0))
    return pl.pallas_call(
        _rope_tables_kernel,
        grid=(S // tb,),
        in_specs=[pl.BlockSpec((tb, 1), lambda i: (i, 0)), pl.BlockSpec((1, LANES), lambda i: (0, 0))],
        out_specs=[row, row, row],
        out_shape=[tab, tab, tab],
        compiler_params=_params(("arbitrary",)),
        name="rope_tables",
    )(positions.reshape(S, 1), jnp.asarray(inv_lanes))


def _inproj_kernel(x_ref, nw_ref, w_ref, cf_ref, sa_ref, sb_ref,
                   om_ref, omg_ref, os_ref, odt_ref,
                   aqT_ref, iqT_ref, iwT_ref, ik_ref, ak_ref, avT_ref):
    x = x_ref[...]
    ms = jnp.mean(x * x, axis=-1, keepdims=True)
    h = (x * lax.rsqrt(ms + EPS) * nw_ref[...]).astype(BF16)

    def proj(c0, n):
        return jnp.dot(h, w_ref[:, c0:c0 + n], preferred_element_type=F32)

    om_ref[...] = proj(C_M, 4 * D_M)
    os_ref[...] = proj(C_Z, D_S + CONV_DIM)
    omg_ref[...] = proj(C_MG, LANES)
    odt_ref[...] = proj(C_DT, LANES)

    cf = cf_ref[...]
    sa = sa_ref[...]
    sb = sb_ref[...]

    def rope(c, cf=cf, sa=sa, sb=sb):
        return c * cf + pltpu.roll(c, LANES - ROPE_HALF, 1) * sa + pltpu.roll(c, ROPE_HALF, 1) * sb

    def rope2(a2):
        return jnp.concatenate([rope(a2[:, :LANES]), rope(a2[:, LANES:])], axis=1)

    a = proj(C_A, 4 * D_A)
    aqT_ref[...] = (rope2(a[:, 0:D_A]) * (HEAD_DIM ** -0.5 * LOG2E)).T.astype(BF16)
    ak_ref[...] = rope2(a[:, D_A:2 * D_A]).astype(BF16)
    vT = a[:, 2 * D_A:3 * D_A].T.astype(BF16)
    ones = jnp.ones((V_EXT - HEAD_DIM, vT.shape[1]), BF16)
    avT_ref[...] = jnp.concatenate(
        [t for h in range(A_HEADS) for t in (vT[HEAD_DIM * h:HEAD_DIM * (h + 1)], ones)], axis=0)
    iqT_ref[...] = rope2(a[:, 3 * D_A:4 * D_A]).T.astype(BF16)

    ikw = proj(C_IK, LANES)
    is_ik = lax.broadcasted_iota(I32, ikw.shape, 1) < IDX_DIM
    ikr = rope(ikw, jnp.where(is_ik, cf, 1.0), jnp.where(is_ik, sa, 0.0), jnp.where(is_ik, sb, 0.0))
    ik_ref[...] = jnp.where(is_ik, ikr, 0.0).astype(BF16)
    iwT_ref[...] = ikw.T[IDX_DIM:IDX_DIM + SUBLANES, :]


def _inproj(x2, nw, wp, cf, sa, sb, S, tm):
    row = lambda n: pl.BlockSpec((tm, n), lambda i: (i, 0))
    colT = lambda n: pl.BlockSpec((n, tm), lambda i: (0, i))
    f = lambda n, dt=F32: jax.ShapeDtypeStruct((S, n), dt)
    fT = lambda n, dt=BF16: jax.ShapeDtypeStruct((n, S), dt)
    return pl.pallas_call(
        _inproj_kernel,
        grid=(S // tm,),
        in_specs=[row(D_MODEL), pl.BlockSpec((1, D_MODEL), lambda i: (0, 0)),
                  _resident((D_MODEL, N_P), lambda i: (0, 0)),
                  row(LANES), row(LANES), row(LANES)],
        out_specs=[row(4 * D_M), row(LANES), row(D_S + CONV_DIM), row(LANES),
                   colT(D_A), colT(D_A), colT(SUBLANES), row(LANES), row(D_A), colT(A_HEADS * V_EXT)],
        out_shape=[f(4 * D_M), f(LANES), f(D_S + CONV_DIM), f(LANES),
                   fT(D_A), fT(D_A), fT(SUBLANES, F32), f(LANES, BF16), f(D_A, BF16), fT(A_HEADS * V_EXT)],
        compiler_params=_params(("arbitrary",)),
        name="inproj",
    )(x2, nw, wp, cf, sa, sb)


def _mlstm_kernel(om_ref, omg_ref, gb_ref, nw_ref, o_ref, C_ref, n_ref, m_ref):
    L = om_ref.shape[0]

    @pl.when(pl.program_id(0) == 0)
    def _():
        C_ref[...] = jnp.zeros_like(C_ref)
        n_ref[...] = jnp.zeros_like(n_ref)
        m_ref[...] = jnp.zeros_like(m_ref)

    causal, tril, triu = _tri(L)
    G = omg_ref[...] + gb_ref[...]
    LF = jnp.minimum(G, 0.0) - jnp.log1p(jnp.exp(-jnp.abs(G)))
    Bc = _cumsum_cols(tril, LF)
    GT = G.T
    BrT = _cumsum_rows(LF.T, triu)
    nw = nw_ref[...]
    outs = []
    for h in range(M_HEADS):
        lo = HEAD_DIM * (h % 2)

        def head(c0, h=h, lo=lo):
            blk = om_ref[:, pl.ds(c0 + LANES * (h // 2), LANES)]
            return blk[:, lo:lo + HEAD_DIM]

        qf = head(0) * (HEAD_DIM ** -0.5)
        kf = head(D_M)
        vf = head(2 * D_M)
        og = head(3 * D_M)
        q = qf.astype(BF16)
        k = kf.astype(BF16)
        v = vf.astype(BF16)
        f_l = M_HEADS + h
        b_col = Bc[:, f_l:f_l + 1]
        b_row = BrT[f_l:f_l + 1, :]
        i_col = G[:, h:h + 1]
        i_row = GT[h:h + 1, :]
        m_prev = m_ref[h:h + 1, 0:1]
        logd = jnp.where(causal, b_col + (i_row - b_row), -jnp.inf)
        m_inter = b_col + m_prev
        m_t = jnp.maximum(m_inter, jnp.max(logd, axis=-1, keepdims=True))
        s = lax.dot_general(q, k, (((1,), (1,)), ((), ())), preferred_element_type=F32) * jnp.exp(logd - m_t)
        scale = jnp.exp(m_inter - m_t)
        C_prev = C_ref[h]
        n_prev = n_ref[h:h + 1, :]
        num = (jnp.dot(s.astype(BF16), v, preferred_element_type=F32)
               + scale * jnp.dot(q, C_prev.astype(BF16), preferred_element_type=F32))
        den = jnp.sum(s, axis=-1, keepdims=True) + scale * jnp.sum(qf * n_prev, axis=-1, keepdims=True)
        hh = num / jnp.maximum(jnp.abs(den), jnp.exp(-m_t))
        b_last = Bc[L - 1:L, f_l:f_l + 1]
        logw_row = b_last - b_row + i_row
        m_new = jnp.maximum(b_last + m_prev, jnp.max(logw_row, axis=-1, keepdims=True))
        w_col = jnp.exp(b_last - b_col + i_col - m_new)
        decay = jnp.exp(b_last + m_prev - m_new)
        kw = kf * w_col
        C_ref[h] = decay * C_prev + lax.dot_general(kw.astype(BF16), v, (((0,), (0,)), ((), ())),
                                                    preferred_element_type=F32)
        n_ref[h:h + 1, :] = decay * n_prev + jnp.sum(kw, axis=0, keepdims=True)
        m_ref[h:h + 1, :] = jnp.broadcast_to(m_new, (1, LANES))
        y = hh * lax.rsqrt(jnp.mean(hh * hh, axis=-1, keepdims=True) + EPS)
        outs.append(jax.nn.sigmoid(og) * y)
    o_ref[...] = (jnp.concatenate(outs, axis=1) * nw).astype(BF16)


def _mlstm(om, omg, gb, nw, S, L):
    return pl.pallas_call(
        _mlstm_kernel,
        grid=(S // L,),
        in_specs=[pl.BlockSpec((L, 4 * D_M), lambda i: (i, 0)), pl.BlockSpec((L, LANES), lambda i: (i, 0)),
                  pl.BlockSpec((1, LANES), lambda i: (0, 0)), pl.BlockSpec((1, D_M), lambda i: (0, 0))],
        out_specs=pl.BlockSpec((L, D_M), lambda i: (i, 0)),
        out_shape=jax.ShapeDtypeStruct((S, D_M), BF16),
        scratch_shapes=[pltpu.VMEM((M_HEADS, HEAD_DIM, HEAD_DIM), F32),
                        pltpu.VMEM((SUBLANES, HEAD_DIM), F32),
                        pltpu.VMEM((SUBLANES, LANES), F32)],
        compiler_params=_params(("arbitrary",)),
        name="mlstm",
    )(om, omg, gb, nw)


def _ssd_kernel(os_ref, odt_ref, cw_ref, cb_ref, dtb_ref, alog_ref, dsk_ref, nw_ref, o_ref,
                carry_ref, st_ref):
    L = os_ref.shape[0]

    @pl.when(pl.program_id(0) == 0)
    def _():
        carry_ref[...] = jnp.zeros_like(carry_ref)
        st_ref[...] = jnp.zeros_like(st_ref)

    causal, tril, triu = _tri(L)
    z = os_ref[:, 0:D_S]
    raw = os_ref[:, D_S:D_S + CONV_DIM]
    ext = jnp.concatenate([carry_ref[...], raw], axis=0)
    cw = cw_ref[...]
    xbc = cb_ref[...] + cw[S_CONV - 1:S_CONV, :] * raw
    for j in range(S_CONV - 1):
        off = SUBLANES - (S_CONV - 1) + j
        xbc = xbc + cw[j:j + 1, :] * ext[off:off + L, :]
    carry_ref[...] = raw[L - SUBLANES:L, :]
    xbc = xbc * jax.nn.sigmoid(xbc)

    dtr = odt_ref[...] + dtb_ref[...]
    DT = jnp.maximum(dtr, 0.0) + jnp.log1p(jnp.exp(-jnp.abs(dtr)))
    dA = DT * (-jnp.exp(alog_ref[...]))
    Ac = _cumsum_cols(tril, dA)
    ArT = _cumsum_rows(dA.T, triu)
    DTT = DT.T
    hpg = S_HEADS // S_GROUPS
    ys = []
    for g in range(S_GROUPS):
        Bg = xbc[:, D_S + S_STATE * g:D_S + S_STATE * (g + 1)]
        Cg = xbc[:, D_S + S_GROUPS * S_STATE + S_STATE * g:D_S + S_GROUPS * S_STATE + S_STATE * (g + 1)]
        Cb = Cg.astype(BF16)
        CB = lax.dot_general(Cb, Bg.astype(BF16), (((1,), (1,)), ((), ())), preferred_element_type=F32)
        BgT = Bg.T
        for hh in range(hpg):
            h = g * hpg + hh
            xpair = xbc[:, LANES * (h // 2):LANES * (h // 2 + 1)]
            xh = xpair[:, HEAD_DIM * (h % 2):HEAD_DIM * (h % 2 + 1)]
            ac_col = Ac[:, h:h + 1]
            ac_row = ArT[h:h + 1, :]
            dec = jnp.exp(jnp.where(causal, ac_col - ac_row, -jnp.inf))
            sc = (CB * dec).astype(BF16)
            xdt = (xh * DT[:, h:h + 1]).astype(BF16)
            st = st_ref[h]
            y = (jnp.dot(sc, xdt, preferred_element_type=F32)
                 + jnp.dot(Cb, st.astype(BF16), preferred_element_type=F32) * jnp.exp(ac_col))
            a_last = Ac[L - 1:L, h:h + 1]
            w_row = jnp.exp(a_last - ac_row) * DTT[h:h + 1, :]
            st_ref[h] = st * jnp.exp(a_last) + jnp.dot((BgT * w_row).astype(BF16), xh.astype(BF16),
                                                       preferred_element_type=F32)
            ys.append(y)
    Y = jnp.concatenate(ys, axis=1) + dsk_ref[...] * xbc[:, 0:D_S]
    gated = Y * (z * jax.nn.sigmoid(z))
    gw = D_S // S_GROUPS
    outs = []
    for g in range(S_GROUPS):
        gg = gated[:, gw * g:gw * (g + 1)]
        outs.append(gg * lax.rsqrt(jnp.mean(gg * gg, axis=-1, keepdims=True) + EPS))
    o_ref[...] = (jnp.concatenate(outs, axis=1) * nw_ref[...]).astype(BF16)


def _ssd(os_, odt, cw, cb, dtb, alog, dsk, nw, S, L):
    full = lambda r, c: pl.BlockSpec((r, c), lambda i: (0, 0))
    return pl.pallas_call(
        _ssd_kernel,
        grid=(S // L,),
        in_specs=[pl.BlockSpec((L, D_S + CONV_DIM), lambda i: (i, 0)), pl.BlockSpec((L, LANES), lambda i: (i, 0)),
                  full(S_CONV, CONV_DIM), full(1, CONV_DIM), full(1, LANES), full(1, LANES),
                  full(1, D_S), full(1, D_S)],
        out_specs=pl.BlockSpec((L, D_S), lambda i: (i, 0)),
        out_shape=jax.ShapeDtypeStruct((S, D_S), BF16),
        scratch_shapes=[pltpu.VMEM((SUBLANES, CONV_DIM), F32),
                        pltpu.VMEM((S_HEADS, S_STATE, HEAD_DIM), F32)],
        compiler_params=_params(("arbitrary",)),
        name="ssd",
    )(os_, odt, cw, cb, dtb, alog, dsk, nw)


def _count_ge(ref, nblk, rb, thr_row, tq):
    thr8 = jnp.broadcast_to(thr_row, (SUBLANES, tq))
    n_acc = 4

    def body(b, accs):
        slab = ref[pl.ds(pl.multiple_of(b * rb, rb), rb), :]
        accs = list(accs)
        for j in range(rb // SUBLANES):
            blk = slab[SUBLANES * j:SUBLANES * (j + 1)]
            a = accs[j % n_acc]
            accs[j % n_acc] = jnp.where(blk >= thr8, a + 1.0, a)
        return tuple(accs)

    accs = lax.fori_loop(0, nblk, body, tuple(jnp.zeros((SUBLANES, tq), F32) for _ in range(n_acc)))
    return jnp.sum((accs[0] + accs[1]) + (accs[2] + accs[3]), axis=0, keepdims=True)


def _avg_floor(lo, hi):
    return (lo >> 1) + (hi >> 1) + (lo & hi & 1)


def _dsa_kernel(aqT_ref, iqT_ref, iwT_ref, ik_ref, ak_ref, avT_ref, o_ref, key_ref, gm_ref, acc_ref, *, top_k):
    TQ = aqT_ref.shape[1]
    q0 = pl.program_id(0) * TQ
    nb = 2 * ((q0 + TQ + 2 * DSA_KB - 1) // (2 * DSA_KB))
    qpos = q0 + lax.broadcasted_iota(I32, (1, TQ), 1)
    kf = float(top_k)
    w_scale = (IDX_HEADS ** -0.5) * (IDX_DIM ** -0.5)

    iqT = iqT_ref[...]
    zpad = jnp.zeros((LANES - IDX_DIM, TQ), BF16)
    iq_pad = [jnp.concatenate([iqT[IDX_DIM * h:IDX_DIM * (h + 1)], zpad], axis=0) for h in range(IDX_HEADS)]
    w_rows = [iwT_ref[h:h + 1, :] * w_scale for h in range(IDX_HEADS)]

    gm_ref[pl.ds(pl.multiple_of(nb * DSA_GB, DSA_GB), DSA_GCHUNK - DSA_GB), :] = jnp.full(
        (DSA_GCHUNK - DSA_GB, TQ), INT_MIN, I32)

    half = DSA_KB // 2

    def p1(b, c, masked):
        gms = []
        for s2 in range(2):
            k0 = pl.multiple_of(b * DSA_KB + s2 * half, half)
            ikb = ik_ref[pl.ds(k0, half), :]
            sc = jnp.zeros((half, TQ), F32)
            for h in range(IDX_HEADS):
                d = jnp.dot(ikb, iq_pad[h], preferred_element_type=F32)
                sc = sc + w_rows[h] * jnp.maximum(d, 0.0)
            bits = lax.bitcast_convert_type(sc, I32)
            key = jnp.where(bits < 0, INT_MIN - bits, bits)
            if masked:
                kpos = k0 + lax.broadcasted_iota(I32, (half, 1), 0)
                key = jnp.where(kpos <= qpos, key, INT_MIN)
            key_ref[pl.ds(k0, half), :] = key
            span = DSA_GROUP * SUBLANES
            for g in range(half // span):
                m = key[span * g:span * g + SUBLANES]
                for j in range(1, DSA_GROUP):
                    m = jnp.maximum(m, key[span * g + SUBLANES * j:span * g + SUBLANES * (j + 1)])
                gms.append(m)
        gm_ref[pl.ds(pl.multiple_of(b * DSA_GB, DSA_GB), DSA_GB), :] = jnp.concatenate(gms, axis=0)
        return c

    nfull = q0 // DSA_KB
    lax.fori_loop(0, nfull, functools.partial(p1, masked=False), 0)
    lax.fori_loop(nfull, nb, functools.partial(p1, masked=True), 0)

    ngc = (nb * DSA_GB + DSA_GCHUNK - 1) // DSA_GCHUNK

    def gm_bit(it, prefix):
        cand = prefix | jnp.left_shift(jnp.int32(1), 31 - it)
        cnt = _count_ge(gm_ref, ngc, DSA_GCHUNK, cand ^ INT_MIN, TQ)
        return jnp.where(cnt >= kf, cand, prefix)

    lo0 = lax.fori_loop(0, 32, gm_bit, jnp.zeros((1, TQ), I32)) ^ INT_MIN

    def gm_max(c, m):
        r0 = pl.multiple_of(c * DSA_GCHUNK, DSA_GCHUNK)
        for j in range(DSA_GCHUNK // SUBLANES):
            m = jnp.maximum(m, gm_ref[pl.ds(r0 + SUBLANES * j, SUBLANES), :])
        return m

    rmax = jnp.max(lax.fori_loop(0, ngc, gm_max, jnp.full((SUBLANES, TQ), INT_MIN, I32)), axis=0, keepdims=True)
    hi0 = rmax + 1

    def w_cond(st):
        return jnp.logical_and(st[0] < DSA_MAX_STEPS, st[4] > 0.0)

    def w_body(st):
        it, lo, hi, chi, _ = st
        probe = jnp.where(it < 2, jnp.minimum(it, 1), INT_MIN)
        mid = jnp.where((lo < probe) & (probe < hi), probe, _avg_floor(lo, hi))
        c = _count_ge(key_ref, nb, DSA_KB, mid, TQ)
        ge = c >= kf
        ex = c == kf
        lo2 = jnp.where(ge, mid, lo)
        hi2 = jnp.where(ex, mid + 1, jnp.where(ge, hi, mid))
        chi2 = jnp.where(ex, -1e9, jnp.where(ge, chi, c))
        active = jnp.max(jnp.where(_avg_floor(lo2, hi2) != lo2, 1.0, 0.0))
        return it + 1, lo2, hi2, chi2, active

    _, tau, _, chi, _ = lax.while_loop(
        w_cond, w_body, (jnp.int32(0), lo0, hi0, jnp.zeros((1, TQ), F32), jnp.float32(1.0)))
    r = jnp.where(tau == INT_MIN, 0.0, jnp.minimum(kf - chi, TIE_ALL))

    AB = DSA_AB
    rowi = lax.broadcasted_iota(I32, (AB, AB), 0)
    coli = lax.broadcasted_iota(I32, (AB, AB), 1)
    tril = jnp.where(coli <= rowi, 1.0, 0.0).astype(BF16)
    aqT = aqT_ref[...]
    hrow = lax.broadcasted_iota(I32, (LANES, TQ), 0) // HEAD_DIM
    q_pad = []
    for h in range(A_HEADS):
        pair = aqT[LANES * (h // 2):LANES * (h // 2 + 1)]
        q_pad.append(jnp.where(hrow == (h % 2), pair, jnp.zeros_like(pair)))
    acc_ref[...] = jnp.zeros_like(acc_ref)

    nsub = 2 * DSA_KB // AB

    def p3(b, carry):
        cnt, ms = carry
        k0 = pl.multiple_of(b * 2 * DSA_KB, 2 * DSA_KB)
        bias = []
        for s2 in range(nsub):
            key = key_ref[pl.ds(k0 + s2 * AB, AB), :]
            eq = key == tau
            pref = jnp.dot(tril, jnp.where(eq, 1.0, 0.0).astype(BF16), preferred_element_type=F32) + cnt
            sel = jnp.where(key > tau, 0.0, jnp.where(eq, pref, 1e9)) <= r
            bias.append(jnp.where(sel, 0.0, NEG_BIG))
            cnt = pref[AB - 1:AB, :]
        lms = [[jnp.dot(ak_ref[pl.ds(k0 + s2 * AB, AB), pl.ds(LANES * (h // 2), LANES)], q_pad[h],
                        preferred_element_type=F32) + bias[s2] for s2 in range(nsub)] for h in range(A_HEADS)]
        ms2, alphas, ps = [], [], []
        for h in range(A_HEADS):
            lm = lms[h]
            bm = functools.reduce(jnp.maximum, [jnp.max(t, axis=0, keepdims=True) for t in lm])
            m_new = jnp.maximum(ms[h], bm)
            ps.append(jnp.concatenate([jnp.exp2(t - m_new).astype(BF16) for t in lm], axis=0))
            alphas.append(jnp.exp2(ms[h] - m_new))
            ms2.append(m_new)
        for h in range(A_HEADS):
            avt = avT_ref[pl.ds(V_EXT * h, V_EXT), pl.ds(k0, 2 * DSA_KB)]
            hs = pl.ds(V_EXT * h, V_EXT)
            acc_ref[hs, :] = alphas[h] * acc_ref[hs, :] + jnp.dot(avt, ps[h], preferred_element_type=F32)
        return cnt, tuple(ms2)

    init = (jnp.zeros((1, TQ), F32), tuple(jnp.full((1, TQ), M_FLOOR, F32) for _ in range(A_HEADS)))
    lax.fori_loop(0, nb // 2, p3, init)
    outT = jnp.concatenate([acc_ref[pl.ds(V_EXT * h, HEAD_DIM), :] / acc_ref[pl.ds(V_EXT * h + HEAD_DIM, 1), :]
                            for h in range(A_HEADS)], axis=0)
    o_ref[...] = outT.T.astype(BF16)


def _dsa(aqT, iqT, iwT, ik, ak, avT, S, tq):
    top_k = min(TOPK_MAX, S // 4)
    colT = lambda n: pl.BlockSpec((n, tq), lambda i: (0, i))
    return pl.pallas_call(
        functools.partial(_dsa_kernel, top_k=top_k),
        grid=(S // tq,),
        in_specs=[colT(D_A), colT(D_A), colT(SUBLANES),
                  _resident((S, LANES), lambda i: (0, 0)),
                  _resident((S, D_A), lambda i: (0, 0)),
                  _resident((A_HEADS * V_EXT, S), lambda i: (0, 0))],
        out_specs=pl.BlockSpec((tq, D_A), lambda i: (i, 0)),
        out_shape=jax.ShapeDtypeStruct((S, D_A), BF16),
        scratch_shapes=[pltpu.VMEM((S, tq), I32),
                        pltpu.VMEM((S // DSA_GROUP + DSA_GCHUNK, tq), I32),
                        pltpu.VMEM((A_HEADS * V_EXT, tq), F32)],
        compiler_params=_params(("arbitrary",)),
        name="dsa",
    )(aqT, iqT, iwT, ik, ak, avT)


FF_CHUNKS = ((0, 768), (768, 768), (1536, 768), (2304, 512))


def _rms(v, w):
    return v * lax.rsqrt(jnp.mean(v * v, axis=-1, keepdims=True) + EPS) * w


def _outffn_kernel(x_ref, hm_ref, ha_ref, hs_ref, nw_ref, wo_ref, wg_ref, wu_ref, wd_ref, o_ref):
    mix = (jnp.dot(hm_ref[...], wo_ref[0:D_M, :], preferred_element_type=F32)
           + jnp.dot(ha_ref[...], wo_ref[D_M:D_M + D_A, :], preferred_element_type=F32)
           + jnp.dot(hs_ref[...], wo_ref[D_M + D_A:, :], preferred_element_type=F32))
    x1 = x_ref[...] + _rms(mix, nw_ref[1:2, :])
    h2 = _rms(x1, nw_ref[2:3, :]).astype(BF16)
    ff = jnp.zeros_like(x1)
    for c0, n in FF_CHUNKS:
        g = jnp.dot(h2, wg_ref[:, c0:c0 + n], preferred_element_type=F32)
        u = jnp.dot(h2, wu_ref[:, c0:c0 + n], preferred_element_type=F32)
        act = (g * jax.nn.sigmoid(g) * u).astype(BF16)
        ff = ff + jnp.dot(act, wd_ref[c0:c0 + n, :], preferred_element_type=F32)
    o_ref[...] = x1 + _rms(ff, nw_ref[3:4, :])


def _outffn(x2, hm, ha, hs, nw4, wo, wg, wu, wd, S, tm):
    row = lambda n: pl.BlockSpec((tm, n), lambda i: (i, 0))
    return pl.pallas_call(
        _outffn_kernel,
        grid=(S // tm,),
        in_specs=[row(D_MODEL), row(D_M), row(D_A), row(D_S),
                  pl.BlockSpec((4, D_MODEL), lambda i: (0, 0)),
                  _resident((D_MODEL, D_MODEL), lambda i: (0, 0)),
                  _resident((D_MODEL, D_FF), lambda i: (0, 0)),
                  _resident((D_MODEL, D_FF), lambda i: (0, 0)),
                  _resident((D_FF, D_MODEL), lambda i: (0, 0))],
        out_specs=row(D_MODEL),
        out_shape=jax.ShapeDtypeStruct((S, D_MODEL), F32),
        compiler_params=_params(("arbitrary",)),
        name="outffn",
    )(x2, hm, ha, hs, nw4, wo, wg, wu, wd)


def _plan(S):
    assert S % (2 * DSA_KB) == 0
    return dict(tm=min(S, 512), lm=min(S, 256), ls=min(S, 256), tq=min(S, 256))


def _pad_lanes(v):
    return jnp.pad(v, [(0, 0)] * (v.ndim - 1) + [(0, LANES - v.shape[-1])])


def kernel(x, positions, norm_w, w_in, mlstm_gate_bias, mlstm_norm_w, conv_w, conv_b, dt_bias, a_log,
           d_skip, ssd_norm_w, w_out, w_gate, w_up, w_down):
    B, S, D = x.shape
    assert B == 1 and D == D_MODEL
    depth = w_in.shape[0]
    plan = _plan(S)

    o = np.cumsum([0, D_M, D_M, D_M, D_M, M_HEADS, M_HEADS, D_A, D_A, D_A, IDX_HEADS * IDX_DIM, IDX_DIM,
                   IDX_HEADS, D_S, CONV_DIM, S_HEADS])
    wp = jnp.concatenate([
        w_in[:, :, o[0]:o[4]], w_in[:, :, o[6]:o[10]], w_in[:, :, o[12]:o[13]], w_in[:, :, o[13]:o[14]],
        _pad_lanes(w_in[:, :, o[4]:o[6]]), _pad_lanes(w_in[:, :, o[10]:o[12]]), _pad_lanes(w_in[:, :, o[14]:o[15]]),
    ], axis=-1).astype(BF16)
    wo = w_out.astype(BF16)
    wg = w_gate.astype(BF16)
    wu = w_up.astype(BF16)
    wd = w_down.astype(BF16)
    gb = _pad_lanes(mlstm_gate_bias[:, None, :])
    dtb = _pad_lanes(dt_bias[:, None, :])
    alog = _pad_lanes(a_log[:, None, :])
    dsk = jnp.repeat(d_skip, HEAD_DIM, axis=-1)[:, None, :]

    cf, sa, sb = _rope_tables(positions.astype(I32), S)
    x2 = x.reshape(S, D)
    for l in range(depth):
        om, omg, os_, odt, aqT, iqT, iwT, ik, ak, avT = _inproj(
            x2, norm_w[l, 0:1], wp[l], cf, sa, sb, S, plan["tm"])
        hm = _mlstm(om, omg, gb[l], mlstm_norm_w[l][None, :], S, plan["lm"])
        hs = _ssd(os_, odt, conv_w[l], conv_b[l][None, :], dtb[l], alog[l], dsk[l], ssd_norm_w[l][None, :],
                  S, plan["ls"])
        ha = _dsa(aqT, iqT, iwT, ik, ak, avT, S, plan["tq"])
        x2 = _outffn(x2, hm, ha, hs, norm_w[l], wo[l], wg[l], wu[l], wd[l], S, plan["tm"])
    return x2.reshape(B, S, D)
```

```python
import functools

import numpy as np
import jax
import jax.numpy as jnp
from jax import lax
from jax.experimental import pallas as pl
from jax.experimental.pallas import tpu as pltpu

F32 = jnp.float32
BF16 = jnp.bfloat16
I32 = jnp.int32

D_MODEL = 1024
HEAD_DIM = 64
D_M = 256
M_HEADS = 4
D_A = 256
A_HEADS = 4
IDX_HEADS = 4
IDX_DIM = 64
TOPK_MAX = 256
D_S = 512
S_HEADS = 8
S_GROUPS = 2
S_STATE = 128
S_CONV = 4
CONV_DIM = D_S + 2 * S_GROUPS * S_STATE
ROPE_THETA = 500000.0
ROPE_DIM = HEAD_DIM // 4
ROPE_HALF = ROPE_DIM // 2
D_FF = 2816
EPS = 1e-6

LANES = 128
SUBLANES = 8
PACK = 16
VMEM_LIMIT = 56 * 1024 * 1024

C_M = 0
C_A = C_M + 4 * D_M
C_Z = C_A + 4 * D_A
C_XBC = C_Z + D_S
C_MG = C_XBC + CONV_DIM
C_IK = C_MG + LANES
C_DT = C_IK + LANES
N_P = C_DT + LANES

INT_MIN = -2 ** 31
NEG_BIG = -1e30
M_FLOOR = -1e29
LOG2E = 1.4426950408889634

DSA_KB = 512
DSA_AB = 256
DSA_GROUP = 16
DSA_GB = DSA_KB // DSA_GROUP
DSA_GCHUNK = 128
V_EXT = HEAD_DIM + PACK
DSA_MAX_STEPS = 40
TIE_ALL = 1e6
EXACT = -1e9
I16_MIN = -2 ** 15


def _params(sem):
    return pltpu.CompilerParams(dimension_semantics=sem, vmem_limit_bytes=VMEM_LIMIT)


def _resident(shape, index_map):
    return pl.BlockSpec(shape, index_map, pipeline_mode=pl.Buffered(1))


def _split3(x):
    h = x.astype(BF16)
    r = x - h.astype(F32)
    m = r.astype(BF16)
    lo = (r - m.astype(F32)).astype(BF16)
    return h, m, lo


def _cumsum_cols(tril, x):
    return sum(jnp.dot(tril, t, preferred_element_type=F32) for t in _split3(x))


def _cumsum_rows(x, triu):
    return sum(jnp.dot(t, triu, preferred_element_type=F32) for t in _split3(x))


def _tri(L):
    row = lax.broadcasted_iota(I32, (L, L), 0)
    col = lax.broadcasted_iota(I32, (L, L), 1)
    causal = col <= row
    tril = jnp.where(causal, 1.0, 0.0).astype(BF16)
    triu = jnp.where(row <= col, 1.0, 0.0).astype(BF16)
    return causal, tril, triu


def _rope_tables_kernel(pos_ref, inv_ref, cf_ref, sa_ref, sb_ref):
    ang = pos_ref[...].astype(F32) * inv_ref[...]
    c = jnp.cos(ang)
    s = jnp.sin(ang)
    j = lax.broadcasted_iota(I32, ang.shape, 1) & (HEAD_DIM - 1)
    cf_ref[...] = jnp.where(j < ROPE_DIM, c, 1.0)
    sa_ref[...] = jnp.where(j < ROPE_HALF, -s, 0.0)
    sb_ref[...] = jnp.where(j < ROPE_HALF, 0.0, jnp.where(j < ROPE_DIM, s, 0.0))


def _rope_tables(positions, S):
    tb = min(S, 1024)
    inv = np.power(np.float32(ROPE_THETA), -np.arange(ROPE_HALF, dtype=np.float32) / np.float32(ROPE_HALF))
    lane = np.arange(LANES) % HEAD_DIM
    inv_lanes = np.where(lane < ROPE_DIM, inv[lane % ROPE_HALF], np.float32(0)).astype(np.float32)[None, :]
    tab = jax.ShapeDtypeStruct((S, LANES), F32)
    row = pl.BlockSpec((tb, LANES), lambda i: (i, 0))
    return pl.pallas_call(
        _rope_tables_kernel,
        grid=(S // tb,),
        in_specs=[pl.BlockSpec((tb, 1), lambda i: (i, 0)), pl.BlockSpec((1, LANES), lambda i: (0, 0))],
        out_specs=[row, row, row],
        out_shape=[tab, tab, tab],
        compiler_params=_params(("arbitrary",)),
        name="rope_tables",
    )(positions.reshape(S, 1), jnp.asarray(inv_lanes))


def _inproj_kernel(x_ref, nw_ref, w_ref, cf_ref, sa_ref, sb_ref,
                   om_ref, omg_ref, os_ref, odt_ref,
                   aqT_ref, iqT_ref, iwT_ref, ik_ref, ak_ref, avT_ref):
    x = x_ref[...]
    ms = jnp.mean(x * x, axis=-1, keepdims=True)
    h = (x * lax.rsqrt(ms + EPS) * nw_ref[...]).astype(BF16)

    def proj(c0, n):
        return jnp.dot(h, w_ref[:, c0:c0 + n], preferred_element_type=F32)

    om_ref[...] = proj(C_M, 4 * D_M)
    os_ref[...] = proj(C_Z, D_S + CONV_DIM)
    omg_ref[...] = proj(C_MG, LANES)
    odt_ref[...] = proj(C_DT, LANES)

    cf = cf_ref[...]
    sa = sa_ref[...]
    sb = sb_ref[...]

    def rope(c, cf=cf, sa=sa, sb=sb):
        return c * cf + pltpu.roll(c, LANES - ROPE_HALF, 1) * sa + pltpu.roll(c, ROPE_HALF, 1) * sb

    def rope2(a2):
        return jnp.concatenate([rope(a2[:, :LANES]), rope(a2[:, LANES:])], axis=1)

    a = proj(C_A, 4 * D_A)
    aqT_ref[...] = (rope2(a[:, 0:D_A]) * (HEAD_DIM ** -0.5 * LOG2E)).T.astype(BF16)
    ak_ref[...] = rope2(a[:, D_A:2 * D_A]).astype(BF16)
    vT = a[:, 2 * D_A:3 * D_A].T.astype(BF16)
    ones = jnp.ones((V_EXT - HEAD_DIM, vT.shape[1]), BF16)
    avT_ref[...] = jnp.concatenate(
        [t for h in range(A_HEADS) for t in (vT[HEAD_DIM * h:HEAD_DIM * (h + 1)], ones)], axis=0)
    iqT_ref[...] = rope2(a[:, 3 * D_A:4 * D_A]).T.astype(BF16)

    ikw = proj(C_IK, LANES)
    is_ik = lax.broadcasted_iota(I32, ikw.shape, 1) < IDX_DIM
    ikr = rope(ikw, jnp.where(is_ik, cf, 1.0), jnp.where(is_ik, sa, 0.0), jnp.where(is_ik, sb, 0.0))
    ik_ref[...] = jnp.where(is_ik, ikr, 0.0).astype(BF16)
    iwT_ref[...] = ikw.T[IDX_DIM:IDX_DIM + SUBLANES, :]


def _inproj(x2, nw, wp, cf, sa, sb, S, tm):
    row = lambda n: pl.BlockSpec((tm, n), lambda i: (i, 0))
    colT = lambda n: pl.BlockSpec((n, tm), lambda i: (0, i))
    f = lambda n, dt=F32: jax.ShapeDtypeStruct((S, n), dt)
    fT = lambda n, dt=BF16: jax.ShapeDtypeStruct((n, S), dt)
    return pl.pallas_call(
        _inproj_kernel,
        grid=(S // tm,),
        in_specs=[row(D_MODEL), pl.BlockSpec((1, D_MODEL), lambda i: (0, 0)),
                  _resident((D_MODEL, N_P), lambda i: (0, 0)),
                  row(LANES), row(LANES), row(LANES)],
        out_specs=[row(4 * D_M), row(LANES), row(D_S + CONV_DIM), row(LANES),
                   colT(D_A), colT(D_A), colT(SUBLANES), row(LANES), row(D_A), colT(A_HEADS * V_EXT)],
        out_shape=[f(4 * D_M), f(LANES), f(D_S + CONV_DIM), f(LANES),
                   fT(D_A), fT(D_A), fT(SUBLANES, F32), f(LANES, BF16), f(D_A, BF16), fT(A_HEADS * V_EXT)],
        compiler_params=_params(("arbitrary",)),
        name="inproj",
    )(x2, nw, wp, cf, sa, sb)


def _mlstm_kernel(om_ref, omg_ref, gb_ref, nw_ref, o_ref, C_ref, n_ref, m_ref):
    L = om_ref.shape[0]

    @pl.when(pl.program_id(0) == 0)
    def _():
        C_ref[...] = jnp.zeros_like(C_ref)
        n_ref[...] = jnp.zeros_like(n_ref)
        m_ref[...] = jnp.zeros_like(m_ref)

    causal, tril, triu = _tri(L)
    G = omg_ref[...] + gb_ref[...]
    LF = jnp.minimum(G, 0.0) - jnp.log1p(jnp.exp(-jnp.abs(G)))
    Bc = _cumsum_cols(tril, LF)
    GT = G.T
    BrT = _cumsum_rows(LF.T, triu)
    nw = nw_ref[...]
    outs = []
    for h in range(M_HEADS):
        lo = HEAD_DIM * (h % 2)

        def head(c0, h=h, lo=lo):
            blk = om_ref[:, pl.ds(c0 + LANES * (h // 2), LANES)]
            return blk[:, lo:lo + HEAD_DIM]

        qf = head(0) * (HEAD_DIM ** -0.5)
        kf = head(D_M)
        vf = head(2 * D_M)
        og = head(3 * D_M)
        q = qf.astype(BF16)
        k = kf.astype(BF16)
        v = vf.astype(BF16)
        f_l = M_HEADS + h
        b_col = Bc[:, f_l:f_l + 1]
        b_row = BrT[f_l:f_l + 1, :]
        i_col = G[:, h:h + 1]
        i_row = GT[h:h + 1, :]
        m_prev = m_ref[h:h + 1, 0:1]
        logd = jnp.where(causal, b_col + (i_row - b_row), -jnp.inf)
        m_inter = b_col + m_prev
        m_t = jnp.maximum(m_inter, jnp.max(logd, axis=-1, keepdims=True))
        s = lax.dot_general(q, k, (((1,), (1,)), ((), ())), preferred_element_type=F32) * jnp.exp(logd - m_t)
        scale = jnp.exp(m_inter - m_t)
        C_prev = C_ref[h]
        n_prev = n_ref[h:h + 1, :]
        num = (jnp.dot(s.astype(BF16), v, preferred_element_type=F32)
               + scale * jnp.dot(q, C_prev.astype(BF16), preferred_element_type=F32))
        den = jnp.sum(s, axis=-1, keepdims=True) + scale * jnp.sum(qf * n_prev, axis=-1, keepdims=True)
        hh = num / jnp.maximum(jnp.abs(den), jnp.exp(-m_t))
        b_last = Bc[L - 1:L, f_l:f_l + 1]
        logw_row = b_last - b_row + i_row
        m_new = jnp.maximum(b_last + m_prev, jnp.max(logw_row, axis=-1, keepdims=True))
        w_col = jnp.exp(b_last - b_col + i_col - m_new)
        decay = jnp.exp(b_last + m_prev - m_new)
        kw = kf * w_col
        C_ref[h] = decay * C_prev + lax.dot_general(kw.astype(BF16), v, (((0,), (0,)), ((), ())),
                                                    preferred_element_type=F32)
        n_ref[h:h + 1, :] = decay * n_prev + jnp.sum(kw, axis=0, keepdims=True)
        m_ref[h:h + 1, :] = jnp.broadcast_to(m_new, (1, LANES))
        y = hh * lax.rsqrt(jnp.mean(hh * hh, axis=-1, keepdims=True) + EPS)
        outs.append(jax.nn.sigmoid(og) * y)
    o_ref[...] = (jnp.concatenate(outs, axis=1) * nw).astype(BF16)


def _mlstm(om, omg, gb, nw, S, L):
    return pl.pallas_call(
        _mlstm_kernel,
        grid=(S // L,),
        in_specs=[pl.BlockSpec((L, 4 * D_M), lambda i: (i, 0)), pl.BlockSpec((L, LANES), lambda i: (i, 0)),
                  pl.BlockSpec((1, LANES), lambda i: (0, 0)), pl.BlockSpec((1, D_M), lambda i: (0, 0))],
        out_specs=pl.BlockSpec((L, D_M), lambda i: (i, 0)),
        out_shape=jax.ShapeDtypeStruct((S, D_M), BF16),
        scratch_shapes=[pltpu.VMEM((M_HEADS, HEAD_DIM, HEAD_DIM), F32),
                        pltpu.VMEM((SUBLANES, HEAD_DIM), F32),
                        pltpu.VMEM((SUBLANES, LANES), F32)],
        compiler_params=_params(("arbitrary",)),
        name="mlstm",
    )(om, omg, gb, nw)


def _ssd_kernel(os_ref, odt_ref, cw_ref, cb_ref, dtb_ref, alog_ref, dsk_ref, nw_ref, o_ref,
                carry_ref, st_ref):
    L = os_ref.shape[0]

    @pl.when(pl.program_id(0) == 0)
    def _():
        carry_ref[...] = jnp.zeros_like(carry_ref)
        st_ref[...] = jnp.zeros_like(st_ref)

    causal, tril, triu = _tri(L)
    z = os_ref[:, 0:D_S]
    raw = os_ref[:, D_S:D_S + CONV_DIM]
    ext = jnp.concatenate([carry_ref[...], raw], axis=0)
    cw = cw_ref[...]
    xbc = cb_ref[...] + cw[S_CONV - 1:S_CONV, :] * raw
    for j in range(S_CONV - 1):
        off = SUBLANES - (S_CONV - 1) + j
        xbc = xbc + cw[j:j + 1, :] * ext[off:off + L, :]
    carry_ref[...] = raw[L - SUBLANES:L, :]
    xbc = xbc * jax.nn.sigmoid(xbc)

    dtr = odt_ref[...] + dtb_ref[...]
    DT = jnp.maximum(dtr, 0.0) + jnp.log1p(jnp.exp(-jnp.abs(dtr)))
    dA = DT * (-jnp.exp(alog_ref[...]))
    Ac = _cumsum_cols(tril, dA)
    ArT = _cumsum_rows(dA.T, triu)
    DTT = DT.T
    hpg = S_HEADS // S_GROUPS
    ys = []
    for g in range(S_GROUPS):
        Bg = xbc[:, D_S + S_STATE * g:D_S + S_STATE * (g + 1)]
        Cg = xbc[:, D_S + S_GROUPS * S_STATE + S_STATE * g:D_S + S_GROUPS * S_STATE + S_STATE * (g + 1)]
        Cb = Cg.astype(BF16)
        CB = lax.dot_general(Cb, Bg.astype(BF16), (((1,), (1,)), ((), ())), preferred_element_type=F32)
        BgT = Bg.T
        for hh in range(hpg):
            h = g * hpg + hh
            xpair = xbc[:, LANES * (h // 2):LANES * (h // 2 + 1)]
            xh = xpair[:, HEAD_DIM * (h % 2):HEAD_DIM * (h % 2 + 1)]
            ac_col = Ac[:, h:h + 1]
            ac_row = ArT[h:h + 1, :]
            dec = jnp.exp(jnp.where(causal, ac_col - ac_row, -jnp.inf))
            sc = (CB * dec).astype(BF16)
            xdt = (xh * DT[:, h:h + 1]).astype(BF16)
            st = st_ref[h]
            y = (jnp.dot(sc, xdt, preferred_element_type=F32)
                 + jnp.dot(Cb, st.astype(BF16), preferred_element_type=F32) * jnp.exp(ac_col))
            a_last = Ac[L - 1:L, h:h + 1]
            w_row = jnp.exp(a_last - ac_row) * DTT[h:h + 1, :]
            st_ref[h] = st * jnp.exp(a_last) + jnp.dot((BgT * w_row).astype(BF16), xh.astype(BF16),
                                                       preferred_element_type=F32)
            ys.append(y)
    Y = jnp.concatenate(ys, axis=1) + dsk_ref[...] * xbc[:, 0:D_S]
    gated = Y * (z * jax.nn.sigmoid(z))
    gw = D_S // S_GROUPS
    outs = []
    for g in range(S_GROUPS):
        gg = gated[:, gw * g:gw * (g + 1)]
        outs.append(gg * lax.rsqrt(jnp.mean(gg * gg, axis=-1, keepdims=True) + EPS))
    o_ref[...] = (jnp.concatenate(outs, axis=1) * nw_ref[...]).astype(BF16)


def _ssd(os_, odt, cw, cb, dtb, alog, dsk, nw, S, L):
    full = lambda r, c: pl.BlockSpec((r, c), lambda i: (0, 0))
    return pl.pallas_call(
        _ssd_kernel,
        grid=(S // L,),
        in_specs=[pl.BlockSpec((L, D_S + CONV_DIM), lambda i: (i, 0)), pl.BlockSpec((L, LANES), lambda i: (i, 0)),
                  full(S_CONV, CONV_DIM), full(1, CONV_DIM), full(1, LANES), full(1, LANES),
                  full(1, D_S), full(1, D_S)],
        out_specs=pl.BlockSpec((L, D_S), lambda i: (i, 0)),
        out_shape=jax.ShapeDtypeStruct((S, D_S), BF16),
        scratch_shapes=[pltpu.VMEM((SUBLANES, CONV_DIM), F32),
                        pltpu.VMEM((S_HEADS, S_STATE, HEAD_DIM), F32)],
        compiler_params=_params(("arbitrary",)),
        name="ssd",
    )(os_, odt, cw, cb, dtb, alog, dsk, nw)


def _count_ge(ref, nblk, rb, thr_row, tq):
    thr8 = jnp.broadcast_to(thr_row, (SUBLANES, tq))
    n_acc = 4

    def body(b, accs):
        slab = ref[pl.ds(pl.multiple_of(b * rb, rb), rb), :]
        accs = list(accs)
        for j in range(rb // SUBLANES):
            blk = slab[SUBLANES * j:SUBLANES * (j + 1)]
            a = accs[j % n_acc]
            accs[j % n_acc] = jnp.where(blk >= thr8, a + 1.0, a)
        return tuple(accs)

    accs = lax.fori_loop(0, nblk, body, tuple(jnp.zeros((SUBLANES, tq), F32) for _ in range(n_acc)))
    return jnp.sum((accs[0] + accs[1]) + (accs[2] + accs[3]), axis=0, keepdims=True)


def _count_ge16(ref, nblk, rb, thr_row, tq):
    thr = jnp.broadcast_to(thr_row.astype(jnp.int16), (PACK, tq))
    n_acc = 4
    one = jnp.ones((PACK, tq), jnp.int16)

    def body(b, accs):
        slab = ref[pl.ds(pl.multiple_of(b * rb, rb), rb), :]
        accs = list(accs)
        for j in range(rb // PACK):
            blk = slab[PACK * j:PACK * (j + 1)]
            a = accs[j % n_acc]
            accs[j % n_acc] = jnp.where(blk >= thr, a + one, a)
        return tuple(accs)

    accs = lax.fori_loop(0, nblk, body, tuple(jnp.zeros((PACK, tq), jnp.int16) for _ in range(n_acc)))
    tot = (accs[0].astype(I32) + accs[1].astype(I32)) + (accs[2].astype(I32) + accs[3].astype(I32))
    return jnp.sum(tot.astype(F32), axis=0, keepdims=True)


def _avg_floor(lo, hi):
    return (lo >> 1) + (hi >> 1) + (lo & hi & 1)


def _dsa_kernel(aqT_ref, iqT_ref, iwT_ref, ik_ref, ak_ref, avT_ref, o_ref, key_ref, k16_ref, gm_ref, acc_ref,
                *, top_k):
    TQ = aqT_ref.shape[1]
    q0 = pl.program_id(0) * TQ
    nb = 2 * ((q0 + TQ + 2 * DSA_KB - 1) // (2 * DSA_KB))
    qpos = q0 + lax.broadcasted_iota(I32, (1, TQ), 1)
    kf = float(top_k)
    w_scale = (IDX_HEADS ** -0.5) * (IDX_DIM ** -0.5)

    iqT = iqT_ref[...]
    zpad = jnp.zeros((LANES - IDX_DIM, TQ), BF16)
    iq_pad = [jnp.concatenate([iqT[IDX_DIM * h:IDX_DIM * (h + 1)], zpad], axis=0) for h in range(IDX_HEADS)]
    w_rows = [iwT_ref[h:h + 1, :] * w_scale for h in range(IDX_HEADS)]

    gm_ref[pl.ds(pl.multiple_of(nb * DSA_GB, DSA_GB), DSA_GCHUNK - DSA_GB), :] = jnp.full(
        (DSA_GCHUNK - DSA_GB, TQ), INT_MIN, I32)

    half = DSA_KB // 2

    def p1(b, c, masked):
        gms = []
        for s2 in range(2):
            k0 = pl.multiple_of(b * DSA_KB + s2 * half, half)
            ikb = ik_ref[pl.ds(k0, half), :]
            sc = jnp.zeros((half, TQ), F32)
            for h in range(IDX_HEADS):
                d = jnp.dot(ikb, iq_pad[h], preferred_element_type=F32)
                sc = sc + w_rows[h] * jnp.maximum(d, 0.0)
            bits = lax.bitcast_convert_type(sc, I32)
            key = jnp.where(bits < 0, INT_MIN - bits, bits)
            if masked:
                kpos = k0 + lax.broadcasted_iota(I32, (half, 1), 0)
                key = jnp.where(kpos <= qpos, key, INT_MIN)
            key_ref[pl.ds(k0, half), :] = key
            k16_ref[pl.ds(k0, half), :] = (key >> 16).astype(jnp.int16)
            span = DSA_GROUP * SUBLANES
            for g in range(half // span):
                m = key[span * g:span * g + SUBLANES]
                for j in range(1, DSA_GROUP):
                    m = jnp.maximum(m, key[span * g + SUBLANES * j:span * g + SUBLANES * (j + 1)])
                gms.append(m)
        gm_ref[pl.ds(pl.multiple_of(b * DSA_GB, DSA_GB), DSA_GB), :] = jnp.concatenate(gms, axis=0)
        return c

    nfull = q0 // DSA_KB
    lax.fori_loop(0, nfull, functools.partial(p1, masked=False), 0)
    lax.fori_loop(nfull, nb, functools.partial(p1, masked=True), 0)

    ngc = (nb * DSA_GB + DSA_GCHUNK - 1) // DSA_GCHUNK

    def gm_bit(it, prefix):
        cand = prefix | jnp.left_shift(jnp.int32(1), 31 - it)
        cnt = _count_ge(gm_ref, ngc, DSA_GCHUNK, cand ^ INT_MIN, TQ)
        return jnp.where(cnt >= kf, cand, prefix)

    lo0 = lax.fori_loop(0, 32, gm_bit, jnp.zeros((1, TQ), I32)) ^ INT_MIN

    def gm_max(c, m):
        r0 = pl.multiple_of(c * DSA_GCHUNK, DSA_GCHUNK)
        for j in range(DSA_GCHUNK // SUBLANES):
            m = jnp.maximum(m, gm_ref[pl.ds(r0 + SUBLANES * j, SUBLANES), :])
        return m

    rmax = jnp.max(lax.fori_loop(0, ngc, gm_max, jnp.full((SUBLANES, TQ), INT_MIN, I32)), axis=0, keepdims=True)
    hi0 = rmax + 1

    def halve(lo_0, hi_0, want, first_probe):
        def w_cond(st):
            return jnp.logical_and(st[0] < DSA_MAX_STEPS, st[4] > 0.0)

        def w_body(st):
            it, lo, hi, chi, _ = st
            probe = jnp.where(it < 1, first_probe, INT_MIN)
            mid = jnp.where((lo < probe) & (probe < hi), probe, _avg_floor(lo, hi))
            c = _count_ge16(k16_ref, nb, DSA_KB, mid, TQ)
            ge = c >= want
            ex = jnp.logical_and(c == want, mid != lo)
            lo2 = jnp.where(ge, mid, lo)
            hi2 = jnp.where(ex, mid + 1, jnp.where(ge, hi, mid))
            chi2 = jnp.where(ex, EXACT, jnp.where(ge, chi, c))
            active = jnp.max(jnp.where(_avg_floor(lo2, hi2) != lo2, 1.0, 0.0))
            return it + 1, lo2, hi2, chi2, active

        _, lo, _, chi, _ = lax.while_loop(
            w_cond, w_body, (jnp.int32(0), lo_0, hi_0, jnp.zeros((1, TQ), F32), jnp.float32(1.0)))
        return lo, chi

    tau_h, chi_h = halve(lo0 >> 16, (rmax >> 16) + 1, kf, INT_MIN)
    exact_h = chi_h == EXACT
    want_l = kf - chi_h

    def build(b, c):
        k0 = pl.multiple_of(b * DSA_KB, DSA_KB)
        key = key_ref[pl.ds(k0, DSA_KB), :]
        low = jnp.where((key >> 16) == tau_h, (key & 0xFFFF) + I16_MIN, I16_MIN)
        k16_ref[pl.ds(k0, DSA_KB), :] = low.astype(jnp.int16)
        return c

    lax.fori_loop(0, nb, build, 0)
    tau_l, chi_l = halve(jnp.where(exact_h, 0, I16_MIN), jnp.where(exact_h, 1, -I16_MIN), want_l, I16_MIN + 1)
    tau = jnp.where(exact_h, tau_h << 16, (tau_h << 16) + (tau_l - I16_MIN))
    r = jnp.where(tau == INT_MIN, 0.0,
                  jnp.where(jnp.logical_or(exact_h, chi_l == EXACT), TIE_ALL, want_l - chi_l))

    AB = DSA_AB
    rowi = lax.broadcasted_iota(I32, (AB, AB), 0)
    coli = lax.broadcasted_iota(I32, (AB, AB), 1)
    tril = jnp.where(coli <= rowi, 1.0, 0.0).astype(BF16)
    aqT = aqT_ref[...]
    hrow = lax.broadcasted_iota(I32, (LANES, TQ), 0) // HEAD_DIM
    q_pad = []
    for h in range(A_HEADS):
        pair = aqT[LANES * (h // 2):LANES * (h // 2 + 1)]
        q_pad.append(jnp.where(hrow == (h % 2), pair, jnp.zeros_like(pair)))
    acc_ref[...] = jnp.zeros_like(acc_ref)

    nsub = 2 * DSA_KB // AB

    def p3(b, carry):
        cnt, ms = carry
        k0 = pl.multiple_of(b * 2 * DSA_KB, 2 * DSA_KB)
        bias = []
        for s2 in range(nsub):
            key = key_ref[pl.ds(k0 + s2 * AB, AB), :]
            eq = key == tau
            pref = jnp.dot(tril, jnp.where(eq, 1.0, 0.0).astype(BF16), preferred_element_type=F32) + cnt
            sel = jnp.where(key > tau, 0.0, jnp.where(eq, pref, 1e9)) <= r
            bias.append(jnp.where(sel, 0.0, NEG_BIG))
            cnt = pref[AB - 1:AB, :]
        lms = [[jnp.dot(ak_ref[pl.ds(k0 + s2 * AB, AB), pl.ds(LANES * (h // 2), LANES)], q_pad[h],
                        preferred_element_type=F32) + bias[s2] for s2 in range(nsub)] for h in range(A_HEADS)]
        ms2, alphas, ps = [], [], []
        for h in range(A_HEADS):
            lm = lms[h]
            bm = functools.reduce(jnp.maximum, [jnp.max(t, axis=0, keepdims=True) for t in lm])
            m_new = jnp.maximum(ms[h], bm)
            ps.append(jnp.concatenate([jnp.exp2(t - m_new).astype(BF16) for t in lm], axis=0))
            alphas.append(jnp.exp2(ms[h] - m_new))
            ms2.append(m_new)
        for h in range(A_HEADS):
            avt = avT_ref[pl.ds(V_EXT * h, V_EXT), pl.ds(k0, 2 * DSA_KB)]
            hs = pl.ds(V_EXT * h, V_EXT)
            acc_ref[hs, :] = alphas[h] * acc_ref[hs, :] + jnp.dot(avt, ps[h], preferred_element_type=F32)
        return cnt, tuple(ms2)

    init = (jnp.zeros((1, TQ), F32), tuple(jnp.full((1, TQ), M_FLOOR, F32) for _ in range(A_HEADS)))
    lax.fori_loop(0, nb // 2, p3, init)
    outT = jnp.concatenate([acc_ref[pl.ds(V_EXT * h, HEAD_DIM), :] / acc_ref[pl.ds(V_EXT * h + HEAD_DIM, 1), :]
                            for h in range(A_HEADS)], axis=0)
    o_ref[...] = outT.T.astype(BF16)


def _dsa(aqT, iqT, iwT, ik, ak, avT, S, tq):
    top_k = min(TOPK_MAX, S // 4)
    colT = lambda n: pl.BlockSpec((n, tq), lambda i: (0, i))
    return pl.pallas_call(
        functools.partial(_dsa_kernel, top_k=top_k),
        grid=(S // tq,),
        in_specs=[colT(D_A), colT(D_A), colT(SUBLANES),
                  _resident((S, LANES), lambda i: (0, 0)),
                  _resident((S, D_A), lambda i: (0, 0)),
                  _resident((A_HEADS * V_EXT, S), lambda i: (0, 0))],
        out_specs=pl.BlockSpec((tq, D_A), lambda i: (i, 0)),
        out_shape=jax.ShapeDtypeStruct((S, D_A), BF16),
        scratch_shapes=[pltpu.VMEM((S, tq), I32),
                        pltpu.VMEM((S, tq), jnp.int16),
                        pltpu.VMEM((S // DSA_GROUP + DSA_GCHUNK, tq), I32),
                        pltpu.VMEM((A_HEADS * V_EXT, tq), F32)],
        compiler_params=_params(("arbitrary",)),
        name="dsa",
    )(aqT, iqT, iwT, ik, ak, avT)


FF_CHUNKS = ((0, 768), (768, 768), (1536, 768), (2304, 512))


def _rms(v, w):
    return v * lax.rsqrt(jnp.mean(v * v, axis=-1, keepdims=True) + EPS) * w


def _outffn_kernel(x_ref, hm_ref, ha_ref, hs_ref, nw_ref, wo_ref, wg_ref, wu_ref, wd_ref, o_ref):
    mix = (jnp.dot(hm_ref[...], wo_ref[0:D_M, :], preferred_element_type=F32)
           + jnp.dot(ha_ref[...], wo_ref[D_M:D_M + D_A, :], preferred_element_type=F32)
           + jnp.dot(hs_ref[...], wo_ref[D_M + D_A:, :], preferred_element_type=F32))
    x1 = x_ref[...] + _rms(mix, nw_ref[1:2, :])
    h2 = _rms(x1, nw_ref[2:3, :]).astype(BF16)
    ff = jnp.zeros_like(x1)
    for c0, n in FF_CHUNKS:
        g = jnp.dot(h2, wg_ref[:, c0:c0 + n], preferred_element_type=F32)
        u = jnp.dot(h2, wu_ref[:, c0:c0 + n], preferred_element_type=F32)
        act = (g * jax.nn.sigmoid(g) * u).astype(BF16)
        ff = ff + jnp.dot(act, wd_ref[c0:c0 + n, :], preferred_element_type=F32)
    o_ref[...] = x1 + _rms(ff, nw_ref[3:4, :])


def _outffn(x2, hm, ha, hs, nw4, wo, wg, wu, wd, S, tm):
    row = lambda n: pl.BlockSpec((tm, n), lambda i: (i, 0))
    return pl.pallas_call(
        _outffn_kernel,
        grid=(S // tm,),
        in_specs=[row(D_MODEL), row(D_M), row(D_A), row(D_S),
                  pl.BlockSpec((4, D_MODEL), lambda i: (0, 0)),
                  _resident((D_MODEL, D_MODEL), lambda i: (0, 0)),
                  _resident((D_MODEL, D_FF), lambda i: (0, 0)),
                  _resident((D_MODEL, D_FF), lambda i: (0, 0)),
                  _resident((D_FF, D_MODEL), lambda i: (0, 0))],
        out_specs=row(D_MODEL),
        out_shape=jax.ShapeDtypeStruct((S, D_MODEL), F32),
        compiler_params=_params(("arbitrary",)),
        name="outffn",
    )(x2, hm, ha, hs, nw4, wo, wg, wu, wd)


def _plan(S):
    assert S % (2 * DSA_KB) == 0
    return dict(tm=min(S, 512), lm=min(S, 256), ls=min(S, 256), tq=min(S, 256))


def _pad_lanes(v):
    return jnp.pad(v, [(0, 0)] * (v.ndim - 1) + [(0, LANES - v.shape[-1])])


def kernel(x, positions, norm_w, w_in, mlstm_gate_bias, mlstm_norm_w, conv_w, conv_b, dt_bias, a_log,
           d_skip, ssd_norm_w, w_out, w_gate, w_up, w_down):
    B, S, D = x.shape
    assert B == 1 and D == D_MODEL
    depth = w_in.shape[0]
    plan = _plan(S)

    o = np.cumsum([0, D_M, D_M, D_M, D_M, M_HEADS, M_HEADS, D_A, D_A, D_A, IDX_HEADS * IDX_DIM, IDX_DIM,
                   IDX_HEADS, D_S, CONV_DIM, S_HEADS])
    wp = jnp.concatenate([
        w_in[:, :, o[0]:o[4]], w_in[:, :, o[6]:o[10]], w_in[:, :, o[12]:o[13]], w_in[:, :, o[13]:o[14]],
        _pad_lanes(w_in[:, :, o[4]:o[6]]), _pad_lanes(w_in[:, :, o[10]:o[12]]), _pad_lanes(w_in[:, :, o[14]:o[15]]),
    ], axis=-1).astype(BF16)
    wo = w_out.astype(BF16)
    wg = w_gate.astype(BF16)
    wu = w_up.astype(BF16)
    wd = w_down.astype(BF16)
    gb = _pad_lanes(mlstm_gate_bias[:, None, :])
    dtb = _pad_lanes(dt_bias[:, None, :])
    alog = _pad_lanes(a_log[:, None, :])
    dsk = jnp.repeat(d_skip, HEAD_DIM, axis=-1)[:, None, :]

    cf, sa, sb = _rope_tables(positions.astype(I32), S)
    x2 = x.reshape(S, D)
    for l in range(depth):
        om, omg, os_, odt, aqT, iqT, iwT, ik, ak, avT = _inproj(
            x2, norm_w[l, 0:1], wp[l], cf, sa, sb, S, plan["tm"])
        hm = _mlstm(om, omg, gb[l], mlstm_norm_w[l][None, :], S, plan["lm"])
        hs = _ssd(os_, odt, conv_w[l], conv_b[l][None, :], dtb[l], alog[l], dsk[l], ssd_norm_w[l][None, :],
                  S, plan["ls"])
        ha = _dsa(aqT, iqT, iwT, ik, ak, avT, S, plan["tq"])
        x2 = _outffn(x2, hm, ha, hs, norm_w[l], wo[l], wg[l], wu[l], wd[l], S, plan["tm"])
    return x2.reshape(B, S, D)
```

```python
import functools

import numpy as np
import jax
import jax.numpy as jnp
from jax import lax
from jax.experimental import pallas as pl
from jax.experimental.pallas import tpu as pltpu

F32 = jnp.float32
BF16 = jnp.bfloat16
I32 = jnp.int32

D_MODEL = 1024
HEAD_DIM = 64
D_M = 256
M_HEADS = 4
D_A = 256
A_HEADS = 4
IDX_HEADS = 4
IDX_DIM = 64
TOPK_MAX = 256
D_S = 512
S_HEADS = 8
S_GROUPS = 2
S_STATE = 128
S_CONV = 4
CONV_DIM = D_S + 2 * S_GROUPS * S_STATE
ROPE_THETA = 500000.0
ROPE_DIM = HEAD_DIM // 4
ROPE_HALF = ROPE_DIM // 2
D_FF = 2816
EPS = 1e-6

LANES = 128
SUBLANES = 8
PACK = 16
VMEM_LIMIT = 56 * 1024 * 1024

C_M = 0
C_A = C_M + 4 * D_M
C_Z = C_A + 4 * D_A
C_XBC = C_Z + D_S
C_MG = C_XBC + CONV_DIM
C_IK = C_MG + LANES
C_DT = C_IK + LANES
N_P = C_DT + LANES

INT_MIN = -2 ** 31
NEG_BIG = -1e30
M_FLOOR = -1e29
LOG2E = 1.4426950408889634

DSA_KB = 512
DSA_AB = 256
DSA_GROUP = 16
DSA_GB = DSA_KB // DSA_GROUP
DSA_GCHUNK = 128
V_EXT = HEAD_DIM + PACK
DSA_MAX_STEPS = 24
DSA_FREE_STEPS = 7
TIE_ALL = 1e6
EXACT = -1e9
I16_MIN = -2 ** 15


def _params(sem):
    return pltpu.CompilerParams(dimension_semantics=sem, vmem_limit_bytes=VMEM_LIMIT)


def _resident(shape, index_map):
    return pl.BlockSpec(shape, index_map, pipeline_mode=pl.Buffered(1))


def _split3(x):
    h = x.astype(BF16)
    r = x - h.astype(F32)
    m = r.astype(BF16)
    lo = (r - m.astype(F32)).astype(BF16)
    return h, m, lo


def _cumsum_cols(tril, x):
    return sum(jnp.dot(tril, t, preferred_element_type=F32) for t in _split3(x))


def _cumsum_rows(x, triu):
    return sum(jnp.dot(t, triu, preferred_element_type=F32) for t in _split3(x))


def _tri(L):
    row = lax.broadcasted_iota(I32, (L, L), 0)
    col = lax.broadcasted_iota(I32, (L, L), 1)
    causal = col <= row
    tril = jnp.where(causal, 1.0, 0.0).astype(BF16)
    triu = jnp.where(row <= col, 1.0, 0.0).astype(BF16)
    return causal, tril, triu


def _rope_tables_kernel(pos_ref, inv_ref, cf_ref, sa_ref, sb_ref):
    ang = pos_ref[...].astype(F32) * inv_ref[...]
    c = jnp.cos(ang)
    s = jnp.sin(ang)
    j = lax.broadcasted_iota(I32, ang.shape, 1) & (HEAD_DIM - 1)
    cf_ref[...] = jnp.where(j < ROPE_DIM, c, 1.0)
    sa_ref[...] = jnp.where(j < ROPE_HALF, -s, 0.0)
    sb_ref[...] = jnp.where(j < ROPE_HALF, 0.0, jnp.where(j < ROPE_DIM, s, 0.0))


def _rope_tables(positions, S):
    tb = min(S, 1024)
    inv = np.power(np.float32(ROPE_THETA), -np.arange(ROPE_HALF, dtype=np.float32) / np.float32(ROPE_HALF))
    lane = np.arange(LANES) % HEAD_DIM
    inv_lanes = np.where(lane < ROPE_DIM, inv[lane % ROPE_HALF], np.float32(0)).astype(np.float32)[None, :]
    tab = jax.ShapeDtypeStruct((S, LANES), F32)
    row = pl.BlockSpec((tb, LANES), lambda i: (i, 0))
    return pl.pallas_call(
        _rope_tables_kernel,
        grid=(S // tb,),
        in_specs=[pl.BlockSpec((tb, 1), lambda i: (i, 0)), pl.BlockSpec((1, LANES), lambda i: (0, 0))],
        out_specs=[row, row, row],
        out_shape=[tab, tab, tab],
        compiler_params=_params(("arbitrary",)),
        name="rope_tables",
    )(positions.reshape(S, 1), jnp.asarray(inv_lanes))


def _inproj_kernel(x_ref, nw_ref, w_ref, cf_ref, sa_ref, sb_ref,
                   om_ref, omg_ref, os_ref, odt_ref,
                   aqT_ref, iqT_ref, iwT_ref, ik_ref, ak_ref, avT_ref):
    x = x_ref[...]
    ms = jnp.mean(x * x, axis=-1, keepdims=True)
    h = (x * lax.rsqrt(ms + EPS) * nw_ref[...]).astype(BF16)

    def proj(c0, n):
        return jnp.dot(h, w_ref[:, c0:c0 + n], preferred_element_type=F32)

    om_ref[...] = proj(C_M, 4 * D_M)
    os_ref[...] = proj(C_Z, D_S + CONV_DIM)
    omg_ref[...] = proj(C_MG, LANES)
    odt_ref[...] = proj(C_DT, LANES)

    cf = cf_ref[...]
    sa = sa_ref[...]
    sb = sb_ref[...]

    def rope(c, cf=cf, sa=sa, sb=sb):
        return c * cf + pltpu.roll(c, LANES - ROPE_HALF, 1) * sa + pltpu.roll(c, ROPE_HALF, 1) * sb

    def rope2(a2):
        return jnp.concatenate([rope(a2[:, :LANES]), rope(a2[:, LANES:])], axis=1)

    a = proj(C_A, 4 * D_A)
    aqT_ref[...] = (rope2(a[:, 0:D_A]) * (HEAD_DIM ** -0.5 * LOG2E)).T.astype(BF16)
    ak_ref[...] = rope2(a[:, D_A:2 * D_A]).astype(BF16)
    vT = a[:, 2 * D_A:3 * D_A].T.astype(BF16)
    ones = jnp.ones((V_EXT - HEAD_DIM, vT.shape[1]), BF16)
    avT_ref[...] = jnp.concatenate(
        [t for h in range(A_HEADS) for t in (vT[HEAD_DIM * h:HEAD_DIM * (h + 1)], ones)], axis=0)
    iqT_ref[...] = rope2(a[:, 3 * D_A:4 * D_A]).T.astype(BF16)

    ikw = proj(C_IK, LANES)
    is_ik = lax.broadcasted_iota(I32, ikw.shape, 1) < IDX_DIM
    ikr = rope(ikw, jnp.where(is_ik, cf, 1.0), jnp.where(is_ik, sa, 0.0), jnp.where(is_ik, sb, 0.0))
    ik_ref[...] = jnp.where(is_ik, ikr, 0.0).astype(BF16)
    iwT_ref[...] = ikw.T[IDX_DIM:IDX_DIM + SUBLANES, :]


def _inproj(x2, nw, wp, cf, sa, sb, S, tm):
    row = lambda n: pl.BlockSpec((tm, n), lambda i: (i, 0))
    colT = lambda n: pl.BlockSpec((n, tm), lambda i: (0, i))
    f = lambda n, dt=F32: jax.ShapeDtypeStruct((S, n), dt)
    fT = lambda n, dt=BF16: jax.ShapeDtypeStruct((n, S), dt)
    return pl.pallas_call(
        _inproj_kernel,
        grid=(S // tm,),
        in_specs=[row(D_MODEL), pl.BlockSpec((1, D_MODEL), lambda i: (0, 0)),
                  _resident((D_MODEL, N_P), lambda i: (0, 0)),
                  row(LANES), row(LANES), row(LANES)],
        out_specs=[row(4 * D_M), row(LANES), row(D_S + CONV_DIM), row(LANES),
                   colT(D_A), colT(D_A), colT(SUBLANES), row(LANES), row(D_A), colT(A_HEADS * V_EXT)],
        out_shape=[f(4 * D_M), f(LANES), f(D_S + CONV_DIM), f(LANES),
                   fT(D_A), fT(D_A), fT(SUBLANES, F32), f(LANES, BF16), f(D_A, BF16), fT(A_HEADS * V_EXT)],
        compiler_params=_params(("arbitrary",)),
        name="inproj",
    )(x2, nw, wp, cf, sa, sb)


def _mlstm_kernel(om_ref, omg_ref, gb_ref, nw_ref, o_ref, C_ref, n_ref, m_ref):
    L = om_ref.shape[0]

    @pl.when(pl.program_id(0) == 0)
    def _():
        C_ref[...] = jnp.zeros_like(C_ref)
        n_ref[...] = jnp.zeros_like(n_ref)
        m_ref[...] = jnp.zeros_like(m_ref)

    causal, tril, triu = _tri(L)
    G = omg_ref[...] + gb_ref[...]
    LF = jnp.minimum(G, 0.0) - jnp.log1p(jnp.exp(-jnp.abs(G)))
    Bc = _cumsum_cols(tril, LF)
    GT = G.T
    BrT = _cumsum_rows(LF.T, triu)
    nw = nw_ref[...]
    outs = []
    for h in range(M_HEADS):
        lo = HEAD_DIM * (h % 2)

        def head(c0, h=h, lo=lo):
            blk = om_ref[:, pl.ds(c0 + LANES * (h // 2), LANES)]
            return blk[:, lo:lo + HEAD_DIM]

        qf = head(0) * (HEAD_DIM ** -0.5)
        kf = head(D_M)
        vf = head(2 * D_M)
        og = head(3 * D_M)
        q = qf.astype(BF16)
        k = kf.astype(BF16)
        v = vf.astype(BF16)
        f_l = M_HEADS + h
        b_col = Bc[:, f_l:f_l + 1]
        b_row = BrT[f_l:f_l + 1, :]
        i_col = G[:, h:h + 1]
        i_row = GT[h:h + 1, :]
        m_prev = m_ref[h:h + 1, 0:1]
        logd = jnp.where(causal, b_col + (i_row - b_row), -jnp.inf)
        m_inter = b_col + m_prev
        m_t = jnp.maximum(m_inter, jnp.max(logd, axis=-1, keepdims=True))
        s = lax.dot_general(q, k, (((1,), (1,)), ((), ())), preferred_element_type=F32) * jnp.exp(logd - m_t)
        scale = jnp.exp(m_inter - m_t)
        C_prev = C_ref[h]
        n_prev = n_ref[h:h + 1, :]
        num = (jnp.dot(s.astype(BF16), v, preferred_element_type=F32)
               + scale * jnp.dot(q, C_prev.astype(BF16), preferred_element_type=F32))
        den = jnp.sum(s, axis=-1, keepdims=True) + scale * jnp.sum(qf * n_prev, axis=-1, keepdims=True)
        hh = num / jnp.maximum(jnp.abs(den), jnp.exp(-m_t))
        b_last = Bc[L - 1:L, f_l:f_l + 1]
        logw_row = b_last - b_row + i_row
        m_new = jnp.maximum(b_last + m_prev, jnp.max(logw_row, axis=-1, keepdims=True))
        w_col = jnp.exp(b_last - b_col + i_col - m_new)
        decay = jnp.exp(b_last + m_prev - m_new)
        kw = kf * w_col
        C_ref[h] = decay * C_prev + lax.dot_general(kw.astype(BF16), v, (((0,), (0,)), ((), ())),
                                                    preferred_element_type=F32)
        n_ref[h:h + 1, :] = decay * n_prev + jnp.sum(kw, axis=0, keepdims=True)
        m_ref[h:h + 1, :] = jnp.broadcast_to(m_new, (1, LANES))
        y = hh * lax.rsqrt(jnp.mean(hh * hh, axis=-1, keepdims=True) + EPS)
        outs.append(jax.nn.sigmoid(og) * y)
    o_ref[...] = (jnp.concatenate(outs, axis=1) * nw).astype(BF16)


def _mlstm(om, omg, gb, nw, S, L):
    return pl.pallas_call(
        _mlstm_kernel,
        grid=(S // L,),
        in_specs=[pl.BlockSpec((L, 4 * D_M), lambda i: (i, 0)), pl.BlockSpec((L, LANES), lambda i: (i, 0)),
                  pl.BlockSpec((1, LANES), lambda i: (0, 0)), pl.BlockSpec((1, D_M), lambda i: (0, 0))],
        out_specs=pl.BlockSpec((L, D_M), lambda i: (i, 0)),
        out_shape=jax.ShapeDtypeStruct((S, D_M), BF16),
        scratch_shapes=[pltpu.VMEM((M_HEADS, HEAD_DIM, HEAD_DIM), F32),
                        pltpu.VMEM((SUBLANES, HEAD_DIM), F32),
                        pltpu.VMEM((SUBLANES, LANES), F32)],
        compiler_params=_params(("arbitrary",)),
        name="mlstm",
    )(om, omg, gb, nw)


def _ssd_kernel(os_ref, odt_ref, cw_ref, cb_ref, dtb_ref, alog_ref, dsk_ref, nw_ref, o_ref,
                carry_ref, st_ref):
    L = os_ref.shape[0]

    @pl.when(pl.program_id(0) == 0)
    def _():
        carry_ref[...] = jnp.zeros_like(carry_ref)
        st_ref[...] = jnp.zeros_like(st_ref)

    causal, tril, triu = _tri(L)
    z = os_ref[:, 0:D_S]
    raw = os_ref[:, D_S:D_S + CONV_DIM]
    ext = jnp.concatenate([carry_ref[...], raw], axis=0)
    cw = cw_ref[...]
    xbc = cb_ref[...] + cw[S_CONV - 1:S_CONV, :] * raw
    for j in range(S_CONV - 1):
        off = SUBLANES - (S_CONV - 1) + j
        xbc = xbc + cw[j:j + 1, :] * ext[off:off + L, :]
    carry_ref[...] = raw[L - SUBLANES:L, :]
    xbc = xbc * jax.nn.sigmoid(xbc)

    dtr = odt_ref[...] + dtb_ref[...]
    DT = jnp.maximum(dtr, 0.0) + jnp.log1p(jnp.exp(-jnp.abs(dtr)))
    dA = DT * (-jnp.exp(alog_ref[...]))
    Ac = _cumsum_cols(tril, dA)
    ArT = _cumsum_rows(dA.T, triu)
    DTT = DT.T
    hpg = S_HEADS // S_GROUPS
    ys = []
    for g in range(S_GROUPS):
        Bg = xbc[:, D_S + S_STATE * g:D_S + S_STATE * (g + 1)]
        Cg = xbc[:, D_S + S_GROUPS * S_STATE + S_STATE * g:D_S + S_GROUPS * S_STATE + S_STATE * (g + 1)]
        Cb = Cg.astype(BF16)
        CB = lax.dot_general(Cb, Bg.astype(BF16), (((1,), (1,)), ((), ())), preferred_element_type=F32)
        BgT = Bg.T
        for hh in range(hpg):
            h = g * hpg + hh
            xpair = xbc[:, LANES * (h // 2):LANES * (h // 2 + 1)]
            xh = xpair[:, HEAD_DIM * (h % 2):HEAD_DIM * (h % 2 + 1)]
            ac_col = Ac[:, h:h + 1]
            ac_row = ArT[h:h + 1, :]
            dec = jnp.exp(jnp.where(causal, ac_col - ac_row, -jnp.inf))
            sc = (CB * dec).astype(BF16)
            xdt = (xh * DT[:, h:h + 1]).astype(BF16)
            st = st_ref[h]
            y = (jnp.dot(sc, xdt, preferred_element_type=F32)
                 + jnp.dot(Cb, st.astype(BF16), preferred_element_type=F32) * jnp.exp(ac_col))
            a_last = Ac[L - 1:L, h:h + 1]
            w_row = jnp.exp(a_last - ac_row) * DTT[h:h + 1, :]
            st_ref[h] = st * jnp.exp(a_last) + jnp.dot((BgT * w_row).astype(BF16), xh.astype(BF16),
                                                       preferred_element_type=F32)
            ys.append(y)
    Y = jnp.concatenate(ys, axis=1) + dsk_ref[...] * xbc[:, 0:D_S]
    gated = Y * (z * jax.nn.sigmoid(z))
    gw = D_S // S_GROUPS
    outs = []
    for g in range(S_GROUPS):
        gg = gated[:, gw * g:gw * (g + 1)]
        outs.append(gg * lax.rsqrt(jnp.mean(gg * gg, axis=-1, keepdims=True) + EPS))
    o_ref[...] = (jnp.concatenate(outs, axis=1) * nw_ref[...]).astype(BF16)


def _ssd(os_, odt, cw, cb, dtb, alog, dsk, nw, S, L):
    full = lambda r, c: pl.BlockSpec((r, c), lambda i: (0, 0))
    return pl.pallas_call(
        _ssd_kernel,
        grid=(S // L,),
        in_specs=[pl.BlockSpec((L, D_S + CONV_DIM), lambda i: (i, 0)), pl.BlockSpec((L, LANES), lambda i: (i, 0)),
                  full(S_CONV, CONV_DIM), full(1, CONV_DIM), full(1, LANES), full(1, LANES),
                  full(1, D_S), full(1, D_S)],
        out_specs=pl.BlockSpec((L, D_S), lambda i: (i, 0)),
        out_shape=jax.ShapeDtypeStruct((S, D_S), BF16),
        scratch_shapes=[pltpu.VMEM((SUBLANES, CONV_DIM), F32),
                        pltpu.VMEM((S_HEADS, S_STATE, HEAD_DIM), F32)],
        compiler_params=_params(("arbitrary",)),
        name="ssd",
    )(os_, odt, cw, cb, dtb, alog, dsk, nw)


def _max16(a, b):
    return jnp.where(a >= b, a, b)


def _count_ge16(ref, nblk, rb, thr_row, tq):
    thr = jnp.broadcast_to(thr_row.astype(jnp.int16), (PACK, tq))
    n_acc = 4
    one = jnp.ones((PACK, tq), jnp.int16)

    def body(b, accs):
        slab = ref[pl.ds(pl.multiple_of(b * rb, rb), rb), :]
        accs = list(accs)
        for j in range(rb // PACK):
            blk = slab[PACK * j:PACK * (j + 1)]
            a = accs[j % n_acc]
            accs[j % n_acc] = jnp.where(blk >= thr, a + one, a)
        return tuple(accs)

    accs = lax.fori_loop(0, nblk, body, tuple(jnp.zeros((PACK, tq), jnp.int16) for _ in range(n_acc)))
    tot = (accs[0].astype(I32) + accs[1].astype(I32)) + (accs[2].astype(I32) + accs[3].astype(I32))
    return jnp.sum(tot.astype(F32), axis=0, keepdims=True)


def _avg_floor(lo, hi):
    return (lo >> 1) + (hi >> 1) + (lo & hi & 1)


def _dsa_kernel(aqT_ref, iqT_ref, iwT_ref, ik_ref, ak_ref, avT_ref, o_ref, key_ref, k16_ref, gm_ref, acc_ref,
                *, top_k):
    TQ = aqT_ref.shape[1]
    q0 = pl.program_id(0) * TQ
    nb = (q0 + TQ + DSA_KB - 1) // DSA_KB
    qpos = q0 + lax.broadcasted_iota(I32, (1, TQ), 1)
    kf = float(top_k)
    w_scale = (IDX_HEADS ** -0.5) * (IDX_DIM ** -0.5)

    iqT = iqT_ref[...]
    zpad = jnp.zeros((LANES - IDX_DIM, TQ), BF16)
    iq_pad = [jnp.concatenate([iqT[IDX_DIM * h:IDX_DIM * (h + 1)], zpad], axis=0) for h in range(IDX_HEADS)]
    w_rows = [iwT_ref[h:h + 1, :] * w_scale for h in range(IDX_HEADS)]

    gm_ref[pl.ds(pl.multiple_of(nb * DSA_GB, DSA_GB), DSA_GCHUNK - DSA_GB), :] = jnp.full(
        (DSA_GCHUNK - DSA_GB, TQ), I16_MIN, jnp.int16)

    @pl.when(nb % 2 == 1)
    def _():
        key_ref[pl.ds(pl.multiple_of(nb * DSA_KB, DSA_KB), DSA_KB), :] = jnp.full((DSA_KB, TQ), INT_MIN, I32)

    half = DSA_KB // 2

    def p1(b, c, masked):
        gms = []
        for s2 in range(2):
            k0 = pl.multiple_of(b * DSA_KB + s2 * half, half)
            ikb = ik_ref[pl.ds(k0, half), :]
            sc = jnp.zeros((half, TQ), F32)
            for h in range(IDX_HEADS):
                d = jnp.dot(ikb, iq_pad[h], preferred_element_type=F32)
                sc = sc + w_rows[h] * jnp.maximum(d, 0.0)
            bits = lax.bitcast_convert_type(sc, I32)
            key = jnp.where(bits < 0, INT_MIN - bits, bits)
            if masked:
                kpos = k0 + lax.broadcasted_iota(I32, (half, 1), 0)
                key = jnp.where(kpos <= qpos, key, INT_MIN)
            key_ref[pl.ds(k0, half), :] = key
            k16 = (key >> 16).astype(jnp.int16)
            k16_ref[pl.ds(k0, half), :] = k16
            span = DSA_GROUP * PACK
            for g in range(half // span):
                m = k16[span * g:span * g + PACK]
                for j in range(1, DSA_GROUP):
                    m = _max16(m, k16[span * g + PACK * j:span * g + PACK * (j + 1)])
                gms.append(m)
        gm_ref[pl.ds(pl.multiple_of(b * DSA_GB, DSA_GB), DSA_GB), :] = jnp.concatenate(gms, axis=0)
        return c

    nfull = q0 // DSA_KB
    lax.fori_loop(0, nfull, functools.partial(p1, masked=False), 0)
    lax.fori_loop(nfull, nb, functools.partial(p1, masked=True), 0)

    ngc = (nb * DSA_GB + DSA_GCHUNK - 1) // DSA_GCHUNK

    def gm_bit(it, prefix):
        cand = prefix | jnp.left_shift(jnp.int32(1), 15 - it)
        cnt = _count_ge16(gm_ref, ngc, DSA_GCHUNK, cand + I16_MIN, TQ)
        return jnp.where(cnt >= kf, cand, prefix)

    lo_h = lax.fori_loop(0, 16, gm_bit, jnp.zeros((1, TQ), I32)) + I16_MIN

    def gm_max(c, m):
        r0 = pl.multiple_of(c * DSA_GCHUNK, DSA_GCHUNK)
        for j in range(DSA_GCHUNK // PACK):
            m = _max16(m, gm_ref[pl.ds(r0 + PACK * j, PACK), :])
        return m

    gmax = lax.fori_loop(0, ngc, gm_max, jnp.full((PACK, TQ), I16_MIN, jnp.int16))
    hi_h = jnp.max(gmax.astype(I32), axis=0, keepdims=True) + 1

    def halve(lo_0, hi_0, want, first_probe, n_free):
        def step(it, lo, hi, chi):
            probe = jnp.where(it < 1, first_probe, INT_MIN)
            mid = jnp.where((lo < probe) & (probe < hi), probe, _avg_floor(lo, hi))
            c = _count_ge16(k16_ref, nb, DSA_KB, mid, TQ)
            ge = c >= want
            ex = jnp.logical_and(c == want, mid != lo)
            lo2 = jnp.where(ge, mid, lo)
            hi2 = jnp.where(ex, mid + 1, jnp.where(ge, hi, mid))
            chi2 = jnp.where(ex, EXACT, jnp.where(ge, chi, c))
            return lo2, hi2, chi2

        def active(lo, hi):
            return jnp.max(jnp.where(_avg_floor(lo, hi) != lo, 1.0, 0.0))

        def w_cond(st):
            return jnp.logical_and(st[0] < DSA_MAX_STEPS, st[4] > 0.0)

        def w_body(st):
            lo2, hi2, chi2 = step(st[0], st[1], st[2], st[3])
            return st[0] + 1, lo2, hi2, chi2, active(lo2, hi2)

        lo, hi, chi = lax.fori_loop(0, n_free, lambda it, st: step(it, *st),
                                    (lo_0, hi_0, jnp.zeros((1, TQ), F32)))
        _, lo, _, chi, _ = lax.while_loop(w_cond, w_body, (jnp.int32(n_free), lo, hi, chi, active(lo, hi)))
        return lo, chi

    tau_h, chi_h = halve(lo_h, hi_h, kf, INT_MIN, DSA_FREE_STEPS)
    exact_h = chi_h == EXACT
    want_l = kf - chi_h

    def build(b, c):
        k0 = pl.multiple_of(b * DSA_KB, DSA_KB)
        key = key_ref[pl.ds(k0, DSA_KB), :]
        low = jnp.where((key >> 16) == tau_h, (key & 0xFFFF) + I16_MIN, I16_MIN)
        k16_ref[pl.ds(k0, DSA_KB), :] = low.astype(jnp.int16)
        return c

    lax.fori_loop(0, nb, build, 0)
    tau_l, chi_l = halve(jnp.where(exact_h, 0, I16_MIN), jnp.where(exact_h, 1, -I16_MIN), want_l, I16_MIN + 1,
                         DSA_FREE_STEPS)
    tau = jnp.where(exact_h, tau_h << 16, (tau_h << 16) + (tau_l - I16_MIN))
    r = jnp.where(tau == INT_MIN, 0.0,
                  jnp.where(jnp.logical_or(exact_h, chi_l == EXACT), TIE_ALL, want_l - chi_l))

    AB = DSA_AB
    rowi = lax.broadcasted_iota(I32, (AB, AB), 0)
    coli = lax.broadcasted_iota(I32, (AB, AB), 1)
    tril = jnp.where(coli <= rowi, 1.0, 0.0).astype(BF16)
    aqT = aqT_ref[...]
    hrow = lax.broadcasted_iota(I32, (LANES, TQ), 0) // HEAD_DIM
    q_pad = []
    for h in range(A_HEADS):
        pair = aqT[LANES * (h // 2):LANES * (h // 2 + 1)]
        q_pad.append(jnp.where(hrow == (h % 2), pair, jnp.zeros_like(pair)))
    acc_ref[...] = jnp.zeros_like(acc_ref)

    nsub = 2 * DSA_KB // AB

    def p3(b, carry):
        cnt, ms = carry
        k0 = pl.multiple_of(b * 2 * DSA_KB, 2 * DSA_KB)
        bias = []
        for s2 in range(nsub):
            key = key_ref[pl.ds(k0 + s2 * AB, AB), :]
            eq = key == tau
            pref = jnp.dot(tril, jnp.where(eq, 1.0, 0.0).astype(BF16), preferred_element_type=F32) + cnt
            sel = jnp.where(key > tau, 0.0, jnp.where(eq, pref, 1e9)) <= r
            bias.append(jnp.where(sel, 0.0, NEG_BIG))
            cnt = pref[AB - 1:AB, :]
        lms = [[jnp.dot(ak_ref[pl.ds(k0 + s2 * AB, AB), pl.ds(LANES * (h // 2), LANES)], q_pad[h],
                        preferred_element_type=F32) + bias[s2] for s2 in range(nsub)] for h in range(A_HEADS)]
        ms2, alphas, ps = [], [], []
        for h in range(A_HEADS):
            lm = lms[h]
            bm = functools.reduce(jnp.maximum, [jnp.max(t, axis=0, keepdims=True) for t in lm])
            m_new = jnp.maximum(ms[h], bm)
            ps.append(jnp.concatenate([jnp.exp2(t - m_new).astype(BF16) for t in lm], axis=0))
            alphas.append(jnp.exp2(ms[h] - m_new))
            ms2.append(m_new)
        for h in range(A_HEADS):
            avt = avT_ref[pl.ds(V_EXT * h, V_EXT), pl.ds(k0, 2 * DSA_KB)]
            hs = pl.ds(V_EXT * h, V_EXT)
            acc_ref[hs, :] = alphas[h] * acc_ref[hs, :] + jnp.dot(avt, ps[h], preferred_element_type=F32)
        return cnt, tuple(ms2)

    init = (jnp.zeros((1, TQ), F32), tuple(jnp.full((1, TQ), M_FLOOR, F32) for _ in range(A_HEADS)))
    lax.fori_loop(0, (nb + 1) // 2, p3, init)
    outT = jnp.concatenate([acc_ref[pl.ds(V_EXT * h, HEAD_DIM), :] / acc_ref[pl.ds(V_EXT * h + HEAD_DIM, 1), :]
                            for h in range(A_HEADS)], axis=0)
    o_ref[...] = outT.T.astype(BF16)


def _dsa(aqT, iqT, iwT, ik, ak, avT, S, tq):
    top_k = min(TOPK_MAX, S // 4)
    colT = lambda n: pl.BlockSpec((n, tq), lambda i: (0, i))
    return pl.pallas_call(
        functools.partial(_dsa_kernel, top_k=top_k),
        grid=(S // tq,),
        in_specs=[colT(D_A), colT(D_A), colT(SUBLANES),
                  _resident((S, LANES), lambda i: (0, 0)),
                  _resident((S, D_A), lambda i: (0, 0)),
                  _resident((A_HEADS * V_EXT, S), lambda i: (0, 0))],
        out_specs=pl.BlockSpec((tq, D_A), lambda i: (i, 0)),
        out_shape=jax.ShapeDtypeStruct((S, D_A), BF16),
        scratch_shapes=[pltpu.VMEM((S, tq), I32),
                        pltpu.VMEM((S, tq), jnp.int16),
                        pltpu.VMEM((S // DSA_GROUP + DSA_GCHUNK, tq), jnp.int16),
                        pltpu.VMEM((A_HEADS * V_EXT, tq), F32)],
        compiler_params=_params(("arbitrary",)),
        name="dsa",
    )(aqT, iqT, iwT, ik, ak, avT)


FF_CHUNKS = ((0, 768), (768, 768), (1536, 768), (2304, 512))


def _rms(v, w):
    return v * lax.rsqrt(jnp.mean(v * v, axis=-1, keepdims=True) + EPS) * w


def _outffn_kernel(x_ref, hm_ref, ha_ref, hs_ref, nw_ref, wo_ref, wg_ref, wu_ref, wd_ref, o_ref):
    mix = (jnp.dot(hm_ref[...], wo_ref[0:D_M, :], preferred_element_type=F32)
           + jnp.dot(ha_ref[...], wo_ref[D_M:D_M + D_A, :], preferred_element_type=F32)
           + jnp.dot(hs_ref[...], wo_ref[D_M + D_A:, :], preferred_element_type=F32))
    x1 = x_ref[...] + _rms(mix, nw_ref[1:2, :])
    h2 = _rms(x1, nw_ref[2:3, :]).astype(BF16)
    ff = jnp.zeros_like(x1)
    for c0, n in FF_CHUNKS:
        g = jnp.dot(h2, wg_ref[:, c0:c0 + n], preferred_element_type=F32)
        u = jnp.dot(h2, wu_ref[:, c0:c0 + n], preferred_element_type=F32)
        act = (g * jax.nn.sigmoid(g) * u).astype(BF16)
        ff = ff + jnp.dot(act, wd_ref[c0:c0 + n, :], preferred_element_type=F32)
    o_ref[...] = x1 + _rms(ff, nw_ref[3:4, :])


def _outffn(x2, hm, ha, hs, nw4, wo, wg, wu, wd, S, tm):
    row = lambda n: pl.BlockSpec((tm, n), lambda i: (i, 0))
    return pl.pallas_call(
        _outffn_kernel,
        grid=(S // tm,),
        in_specs=[row(D_MODEL), row(D_M), row(D_A), row(D_S),
                  pl.BlockSpec((4, D_MODEL), lambda i: (0, 0)),
                  _resident((D_MODEL, D_MODEL), lambda i: (0, 0)),
                  _resident((D_MODEL, D_FF), lambda i: (0, 0)),
                  _resident((D_MODEL, D_FF), lambda i: (0, 0)),
                  _resident((D_FF, D_MODEL), lambda i: (0, 0))],
        out_specs=row(D_MODEL),
        out_shape=jax.ShapeDtypeStruct((S, D_MODEL), F32),
        compiler_params=_params(("arbitrary",)),
        name="outffn",
    )(x2, hm, ha, hs, nw4, wo, wg, wu, wd)


def _plan(S):
    assert S % (2 * DSA_KB) == 0
    return dict(tm=min(S, 512), lm=min(S, 256), ls=min(S, 256), tq=min(S, 256))


def _pad_lanes(v):
    return jnp.pad(v, [(0, 0)] * (v.ndim - 1) + [(0, LANES - v.shape[-1])])


def kernel(x, positions, norm_w, w_in, mlstm_gate_bias, mlstm_norm_w, conv_w, conv_b, dt_bias, a_log,
           d_skip, ssd_norm_w, w_out, w_gate, w_up, w_down):
    B, S, D = x.shape
    assert B == 1 and D == D_MODEL
    depth = w_in.shape[0]
    plan = _plan(S)

    o = np.cumsum([0, D_M, D_M, D_M, D_M, M_HEADS, M_HEADS, D_A, D_A, D_A, IDX_HEADS * IDX_DIM, IDX_DIM,
                   IDX_HEADS, D_S, CONV_DIM, S_HEADS])
    wp = jnp.concatenate([
        w_in[:, :, o[0]:o[4]], w_in[:, :, o[6]:o[10]], w_in[:, :, o[12]:o[13]], w_in[:, :, o[13]:o[14]],
        _pad_lanes(w_in[:, :, o[4]:o[6]]), _pad_lanes(w_in[:, :, o[10]:o[12]]), _pad_lanes(w_in[:, :, o[14]:o[15]]),
    ], axis=-1).astype(BF16)
    wo = w_out.astype(BF16)
    wg = w_gate.astype(BF16)
    wu = w_up.astype(BF16)
    wd = w_down.astype(BF16)
    gb = _pad_lanes(mlstm_gate_bias[:, None, :])
    dtb = _pad_lanes(dt_bias[:, None, :])
    alog = _pad_lanes(a_log[:, None, :])
    dsk = jnp.repeat(d_skip, HEAD_DIM, axis=-1)[:, None, :]

    cf, sa, sb = _rope_tables(positions.astype(I32), S)
    x2 = x.reshape(S, D)
    for l in range(depth):
        om, omg, os_, odt, aqT, iqT, iwT, ik, ak, avT = _inproj(
            x2, norm_w[l, 0:1], wp[l], cf, sa, sb, S, plan["tm"])
        hm = _mlstm(om, omg, gb[l], mlstm_norm_w[l][None, :], S, plan["lm"])
        hs = _ssd(os_, odt, conv_w[l], conv_b[l][None, :], dtb[l], alog[l], dsk[l], ssd_norm_w[l][None, :],
                  S, plan["ls"])
        ha = _dsa(aqT, iqT, iwT, ik, ak, avT, S, plan["tq"])
        x2 = _outffn(x2, hm, ha, hs, norm_w[l], wo[l], wg[l], wu[l], wd[l], S, plan["tm"])
    return x2.reshape(B, S, D)
```

```python
import functools

import numpy as np
import jax
import jax.numpy as jnp
from jax import lax
from jax.experimental import pallas as pl
from jax.experimental.pallas import tpu as pltpu

F32 = jnp.float32
BF16 = jnp.bfloat16
I32 = jnp.int32

D_MODEL = 1024
HEAD_DIM = 64
D_M = 256
M_HEADS = 4
D_A = 256
A_HEADS = 4
IDX_HEADS = 4
IDX_DIM = 64
TOPK_MAX = 256
D_S = 512
S_HEADS = 8
S_GROUPS = 2
S_STATE = 128
S_CONV = 4
CONV_DIM = D_S + 2 * S_GROUPS * S_STATE
ROPE_THETA = 500000.0
ROPE_DIM = HEAD_DIM // 4
ROPE_HALF = ROPE_DIM // 2
D_FF = 2816
EPS = 1e-6

LANES = 128
SUBLANES = 8
PACK = 16
VMEM_LIMIT = 56 * 1024 * 1024

C_M = 0
C_A = C_M + 4 * D_M
C_Z = C_A + 4 * D_A
C_XBC = C_Z + D_S
C_MG = C_XBC + CONV_DIM
C_IK = C_MG + LANES
C_DT = C_IK + LANES
N_P = C_DT + LANES

INT_MIN = -2 ** 31
NEG_BIG = -1e30
M_FLOOR = -1e29
LOG2E = 1.4426950408889634

DSA_KB = 512
DSA_AB = 256
DSA_GROUP = 16
DSA_GB = DSA_KB // DSA_GROUP
DSA_GCHUNK = 128
V_EXT = HEAD_DIM + PACK
DSA_MAX_STEPS = 24
DSA_FREE_STEPS = 7
TIE_ALL = 1e6
EXACT = -1e9
I16_MIN = -2 ** 15


def _params(sem):
    return pltpu.CompilerParams(dimension_semantics=sem, vmem_limit_bytes=VMEM_LIMIT)


def _resident(shape, index_map):
    return pl.BlockSpec(shape, index_map, pipeline_mode=pl.Buffered(1))


def _split3(x):
    h = x.astype(BF16)
    r = x - h.astype(F32)
    m = r.astype(BF16)
    lo = (r - m.astype(F32)).astype(BF16)
    return h, m, lo


def _cumsum_cols(tril, x):
    return sum(jnp.dot(tril, t, preferred_element_type=F32) for t in _split3(x))


def _cumsum_rows(x, triu):
    return sum(jnp.dot(t, triu, preferred_element_type=F32) for t in _split3(x))


def _tri(L):
    row = lax.broadcasted_iota(I32, (L, L), 0)
    col = lax.broadcasted_iota(I32, (L, L), 1)
    causal = col <= row
    tril = jnp.where(causal, 1.0, 0.0).astype(BF16)
    triu = jnp.where(row <= col, 1.0, 0.0).astype(BF16)
    return causal, tril, triu


def _rope_tables_kernel(pos_ref, inv_ref, cf_ref, sa_ref, sb_ref):
    ang = pos_ref[...].astype(F32) * inv_ref[...]
    c = jnp.cos(ang)
    s = jnp.sin(ang)
    j = lax.broadcasted_iota(I32, ang.shape, 1) & (HEAD_DIM - 1)
    cf_ref[...] = jnp.where(j < ROPE_DIM, c, 1.0)
    sa_ref[...] = jnp.where(j < ROPE_HALF, -s, 0.0)
    sb_ref[...] = jnp.where(j < ROPE_HALF, 0.0, jnp.where(j < ROPE_DIM, s, 0.0))


def _rope_tables(positions, S):
    tb = min(S, 1024)
    inv = np.power(np.float32(ROPE_THETA), -np.arange(ROPE_HALF, dtype=np.float32) / np.float32(ROPE_HALF))
    lane = np.arange(LANES) % HEAD_DIM
    inv_lanes = np.where(lane < ROPE_DIM, inv[lane % ROPE_HALF], np.float32(0)).astype(np.float32)[None, :]
    tab = jax.ShapeDtypeStruct((S, LANES), F32)
    row = pl.BlockSpec((tb, LANES), lambda i: (i, 0))
    return pl.pallas_call(
        _rope_tables_kernel,
        grid=(S // tb,),
        in_specs=[pl.BlockSpec((tb, 1), lambda i: (i, 0)), pl.BlockSpec((1, LANES), lambda i: (0, 0))],
        out_specs=[row, row, row],
        out_shape=[tab, tab, tab],
        compiler_params=_params(("arbitrary",)),
        name="rope_tables",
    )(positions.reshape(S, 1), jnp.asarray(inv_lanes))


def _inproj_kernel(x_ref, nw_ref, w_ref, cf_ref, sa_ref, sb_ref,
                   om_ref, omg_ref, os_ref, odt_ref,
                   aqT_ref, iqT_ref, iwT_ref, ik_ref, ak_ref, avT_ref):
    x = x_ref[...]
    ms = jnp.mean(x * x, axis=-1, keepdims=True)
    h = (x * lax.rsqrt(ms + EPS) * nw_ref[...]).astype(BF16)

    def proj(c0, n):
        return jnp.dot(h, w_ref[:, c0:c0 + n], preferred_element_type=F32)

    om_ref[...] = proj(C_M, 4 * D_M)
    os_ref[...] = proj(C_Z, D_S + CONV_DIM)
    omg_ref[...] = proj(C_MG, LANES)
    odt_ref[...] = proj(C_DT, LANES)

    cf = cf_ref[...]
    sa = sa_ref[...]
    sb = sb_ref[...]

    def rope(c, cf=cf, sa=sa, sb=sb):
        return c * cf + pltpu.roll(c, LANES - ROPE_HALF, 1) * sa + pltpu.roll(c, ROPE_HALF, 1) * sb

    def rope2(a2):
        return jnp.concatenate([rope(a2[:, :LANES]), rope(a2[:, LANES:])], axis=1)

    a = proj(C_A, 4 * D_A)
    aqT_ref[...] = (rope2(a[:, 0:D_A]) * (HEAD_DIM ** -0.5 * LOG2E)).T.astype(BF16)
    ak_ref[...] = rope2(a[:, D_A:2 * D_A]).astype(BF16)
    vT = a[:, 2 * D_A:3 * D_A].T.astype(BF16)
    ones = jnp.ones((V_EXT - HEAD_DIM, vT.shape[1]), BF16)
    avT_ref[...] = jnp.concatenate(
        [t for h in range(A_HEADS) for t in (vT[HEAD_DIM * h:HEAD_DIM * (h + 1)], ones)], axis=0)
    iqT_ref[...] = rope2(a[:, 3 * D_A:4 * D_A]).T.astype(BF16)

    ikw = proj(C_IK, LANES)
    is_ik = lax.broadcasted_iota(I32, ikw.shape, 1) < IDX_DIM
    ikr = rope(ikw, jnp.where(is_ik, cf, 1.0), jnp.where(is_ik, sa, 0.0), jnp.where(is_ik, sb, 0.0))
    ik_ref[...] = jnp.where(is_ik, ikr, 0.0).astype(BF16)
    iwT_ref[...] = ikw.T[IDX_DIM:IDX_DIM + SUBLANES, :]


def _inproj(x2, nw, wp, cf, sa, sb, S, tm):
    row = lambda n: pl.BlockSpec((tm, n), lambda i: (i, 0))
    colT = lambda n: pl.BlockSpec((n, tm), lambda i: (0, i))
    f = lambda n, dt=F32: jax.ShapeDtypeStruct((S, n), dt)
    fT = lambda n, dt=BF16: jax.ShapeDtypeStruct((n, S), dt)
    return pl.pallas_call(
        _inproj_kernel,
        grid=(S // tm,),
        in_specs=[row(D_MODEL), pl.BlockSpec((1, D_MODEL), lambda i: (0, 0)),
                  _resident((D_MODEL, N_P), lambda i: (0, 0)),
                  row(LANES), row(LANES), row(LANES)],
        out_specs=[row(4 * D_M), row(LANES), row(D_S + CONV_DIM), row(LANES),
                   colT(D_A), colT(D_A), colT(SUBLANES), row(LANES), row(D_A), colT(A_HEADS * V_EXT)],
        out_shape=[f(4 * D_M), f(LANES), f(D_S + CONV_DIM), f(LANES),
                   fT(D_A), fT(D_A), fT(SUBLANES, F32), f(LANES, BF16), f(D_A, BF16), fT(A_HEADS * V_EXT)],
        compiler_params=_params(("arbitrary",)),
        name="inproj",
    )(x2, nw, wp, cf, sa, sb)


def _mlstm_kernel(om_ref, omg_ref, gb_ref, nw_ref, o_ref, C_ref, n_ref, m_ref):
    L = om_ref.shape[0]

    @pl.when(pl.program_id(0) == 0)
    def _():
        C_ref[...] = jnp.zeros_like(C_ref)
        n_ref[...] = jnp.zeros_like(n_ref)
        m_ref[...] = jnp.zeros_like(m_ref)

    causal, tril, triu = _tri(L)
    G = omg_ref[...] + gb_ref[...]
    LF = jnp.minimum(G, 0.0) - jnp.log1p(jnp.exp(-jnp.abs(G)))
    Bc = _cumsum_cols(tril, LF)
    GT = G.T
    BrT = _cumsum_rows(LF.T, triu)
    nw = nw_ref[...]
    heads = range(M_HEADS)

    def head(c0, h):
        blk = om_ref[:, pl.ds(c0 + LANES * (h // 2), LANES)]
        return blk[:, HEAD_DIM * (h % 2):HEAD_DIM * (h % 2 + 1)]

    qf = [head(0, h) * (HEAD_DIM ** -0.5) for h in heads]
    kf = [head(D_M, h) for h in heads]
    q = [t.astype(BF16) for t in qf]
    k = [t.astype(BF16) for t in kf]
    v = [head(2 * D_M, h).astype(BF16) for h in heads]
    C_prev = [C_ref[h] for h in heads]
    n_prev = [n_ref[h:h + 1, :] for h in heads]
    m_prev = [m_ref[h:h + 1, 0:1] for h in heads]
    qk = [lax.dot_general(q[h], k[h], (((1,), (1,)), ((), ())), preferred_element_type=F32) for h in heads]
    qC = [jnp.dot(q[h], C_prev[h].astype(BF16), preferred_element_type=F32) for h in heads]
    s, scale, m_t, kw, decay = [], [], [], [], []
    for h in heads:
        f_l = M_HEADS + h
        b_col = Bc[:, f_l:f_l + 1]
        b_row = BrT[f_l:f_l + 1, :]
        i_col = G[:, h:h + 1]
        i_row = GT[h:h + 1, :]
        logd = jnp.where(causal, b_col + (i_row - b_row), -jnp.inf)
        m_inter = b_col + m_prev[h]
        mt = jnp.maximum(m_inter, jnp.max(logd, axis=-1, keepdims=True))
        s.append(qk[h] * jnp.exp(logd - mt))
        scale.append(jnp.exp(m_inter - mt))
        m_t.append(mt)
        b_last = Bc[L - 1:L, f_l:f_l + 1]
        m_new = jnp.maximum(b_last + m_prev[h], jnp.max(b_last - b_row + i_row, axis=-1, keepdims=True))
        kw.append(kf[h] * jnp.exp(b_last - b_col + i_col - m_new))
        decay.append(jnp.exp(b_last + m_prev[h] - m_new))
        m_ref[h:h + 1, :] = jnp.broadcast_to(m_new, (1, LANES))
    sv = [jnp.dot(s[h].astype(BF16), v[h], preferred_element_type=F32) for h in heads]
    kv = [lax.dot_general(kw[h].astype(BF16), v[h], (((0,), (0,)), ((), ())), preferred_element_type=F32)
          for h in heads]
    outs = []
    for h in heads:
        num = sv[h] + scale[h] * qC[h]
        den = (jnp.sum(s[h], axis=-1, keepdims=True)
               + scale[h] * jnp.sum(qf[h] * n_prev[h], axis=-1, keepdims=True))
        hh = num / jnp.maximum(jnp.abs(den), jnp.exp(-m_t[h]))
        C_ref[h] = decay[h] * C_prev[h] + kv[h]
        n_ref[h:h + 1, :] = decay[h] * n_prev[h] + jnp.sum(kw[h], axis=0, keepdims=True)
        y = hh * lax.rsqrt(jnp.mean(hh * hh, axis=-1, keepdims=True) + EPS)
        outs.append(jax.nn.sigmoid(head(3 * D_M, h)) * y)
    o_ref[...] = (jnp.concatenate(outs, axis=1) * nw).astype(BF16)


def _mlstm(om, omg, gb, nw, S, L):
    return pl.pallas_call(
        _mlstm_kernel,
        grid=(S // L,),
        in_specs=[pl.BlockSpec((L, 4 * D_M), lambda i: (i, 0)), pl.BlockSpec((L, LANES), lambda i: (i, 0)),
                  pl.BlockSpec((1, LANES), lambda i: (0, 0)), pl.BlockSpec((1, D_M), lambda i: (0, 0))],
        out_specs=pl.BlockSpec((L, D_M), lambda i: (i, 0)),
        out_shape=jax.ShapeDtypeStruct((S, D_M), BF16),
        scratch_shapes=[pltpu.VMEM((M_HEADS, HEAD_DIM, HEAD_DIM), F32),
                        pltpu.VMEM((SUBLANES, HEAD_DIM), F32),
                        pltpu.VMEM((SUBLANES, LANES), F32)],
        compiler_params=_params(("arbitrary",)),
        name="mlstm",
    )(om, omg, gb, nw)


def _ssd_kernel(os_ref, odt_ref, cw_ref, cb_ref, dtb_ref, alog_ref, dsk_ref, nw_ref, o_ref,
                carry_ref, st_ref):
    L = os_ref.shape[0]

    @pl.when(pl.program_id(0) == 0)
    def _():
        carry_ref[...] = jnp.zeros_like(carry_ref)
        st_ref[...] = jnp.zeros_like(st_ref)

    causal, tril, triu = _tri(L)
    z = os_ref[:, 0:D_S]
    raw = os_ref[:, D_S:D_S + CONV_DIM]
    ext = jnp.concatenate([carry_ref[...], raw], axis=0)
    cw = cw_ref[...]
    xbc = cb_ref[...] + cw[S_CONV - 1:S_CONV, :] * raw
    for j in range(S_CONV - 1):
        off = SUBLANES - (S_CONV - 1) + j
        xbc = xbc + cw[j:j + 1, :] * ext[off:off + L, :]
    carry_ref[...] = raw[L - SUBLANES:L, :]
    xbc = xbc * jax.nn.sigmoid(xbc)

    dtr = odt_ref[...] + dtb_ref[...]
    DT = jnp.maximum(dtr, 0.0) + jnp.log1p(jnp.exp(-jnp.abs(dtr)))
    dA = DT * (-jnp.exp(alog_ref[...]))
    Ac = _cumsum_cols(tril, dA)
    ArT = _cumsum_rows(dA.T, triu)
    DTT = DT.T
    hpg = S_HEADS // S_GROUPS
    ys = []
    for g in range(S_GROUPS):
        Bg = xbc[:, D_S + S_STATE * g:D_S + S_STATE * (g + 1)]
        Cg = xbc[:, D_S + S_GROUPS * S_STATE + S_STATE * g:D_S + S_GROUPS * S_STATE + S_STATE * (g + 1)]
        Cb = Cg.astype(BF16)
        CB = lax.dot_general(Cb, Bg.astype(BF16), (((1,), (1,)), ((), ())), preferred_element_type=F32)
        BgT = Bg.T
        for hh in range(hpg):
            h = g * hpg + hh
            xpair = xbc[:, LANES * (h // 2):LANES * (h // 2 + 1)]
            xh = xpair[:, HEAD_DIM * (h % 2):HEAD_DIM * (h % 2 + 1)]
            ac_col = Ac[:, h:h + 1]
            ac_row = ArT[h:h + 1, :]
            dec = jnp.exp(jnp.where(causal, ac_col - ac_row, -jnp.inf))
            sc = (CB * dec).astype(BF16)
            xdt = (xh * DT[:, h:h + 1]).astype(BF16)
            st = st_ref[h]
            y = (jnp.dot(sc, xdt, preferred_element_type=F32)
                 + jnp.dot(Cb, st.astype(BF16), preferred_element_type=F32) * jnp.exp(ac_col))
            a_last = Ac[L - 1:L, h:h + 1]
            w_row = jnp.exp(a_last - ac_row) * DTT[h:h + 1, :]
            st_ref[h] = st * jnp.exp(a_last) + jnp.dot((BgT * w_row).astype(BF16), xh.astype(BF16),
                                                       preferred_element_type=F32)
            ys.append(y)
    Y = jnp.concatenate(ys, axis=1) + dsk_ref[...] * xbc[:, 0:D_S]
    gated = Y * (z * jax.nn.sigmoid(z))
    gw = D_S // S_GROUPS
    outs = []
    for g in range(S_GROUPS):
        gg = gated[:, gw * g:gw * (g + 1)]
        outs.append(gg * lax.rsqrt(jnp.mean(gg * gg, axis=-1, keepdims=True) + EPS))
    o_ref[...] = (jnp.concatenate(outs, axis=1) * nw_ref[...]).astype(BF16)


def _ssd(os_, odt, cw, cb, dtb, alog, dsk, nw, S, L):
    full = lambda r, c: pl.BlockSpec((r, c), lambda i: (0, 0))
    return pl.pallas_call(
        _ssd_kernel,
        grid=(S // L,),
        in_specs=[pl.BlockSpec((L, D_S + CONV_DIM), lambda i: (i, 0)), pl.BlockSpec((L, LANES), lambda i: (i, 0)),
                  full(S_CONV, CONV_DIM), full(1, CONV_DIM), full(1, LANES), full(1, LANES),
                  full(1, D_S), full(1, D_S)],
        out_specs=pl.BlockSpec((L, D_S), lambda i: (i, 0)),
        out_shape=jax.ShapeDtypeStruct((S, D_S), BF16),
        scratch_shapes=[pltpu.VMEM((SUBLANES, CONV_DIM), F32),
                        pltpu.VMEM((S_HEADS, S_STATE, HEAD_DIM), F32)],
        compiler_params=_params(("arbitrary",)),
        name="ssd",
    )(os_, odt, cw, cb, dtb, alog, dsk, nw)


def _max16(a, b):
    return jnp.where(a >= b, a, b)


def _count_ge16(ref, nblk, rb, thr_row, tq):
    thr = jnp.broadcast_to(thr_row.astype(jnp.int16), (PACK, tq))
    n_acc = 4
    one = jnp.ones((PACK, tq), jnp.int16)

    def body(b, accs):
        slab = ref[pl.ds(pl.multiple_of(b * rb, rb), rb), :]
        accs = list(accs)
        for j in range(rb // PACK):
            blk = slab[PACK * j:PACK * (j + 1)]
            a = accs[j % n_acc]
            accs[j % n_acc] = jnp.where(blk >= thr, a + one, a)
        return tuple(accs)

    accs = lax.fori_loop(0, nblk, body, tuple(jnp.zeros((PACK, tq), jnp.int16) for _ in range(n_acc)))
    tot = (accs[0].astype(I32) + accs[1].astype(I32)) + (accs[2].astype(I32) + accs[3].astype(I32))
    return jnp.sum(tot.astype(F32), axis=0, keepdims=True)


def _avg_floor(lo, hi):
    return (lo >> 1) + (hi >> 1) + (lo & hi & 1)


def _dsa_kernel(aqT_ref, iqT_ref, iwT_ref, ik_ref, ak_ref, avT_ref, o_ref, key_ref, k16_ref, gm_ref, acc_ref,
                *, top_k):
    TQ = aqT_ref.shape[1]
    q0 = pl.program_id(0) * TQ
    nb = (q0 + TQ + DSA_KB - 1) // DSA_KB
    qpos = q0 + lax.broadcasted_iota(I32, (1, TQ), 1)
    kf = float(top_k)
    w_scale = (IDX_HEADS ** -0.5) * (IDX_DIM ** -0.5)

    iqT = iqT_ref[...]
    zpad = jnp.zeros((LANES - IDX_DIM, TQ), BF16)
    iq_pad = [jnp.concatenate([iqT[IDX_DIM * h:IDX_DIM * (h + 1)], zpad], axis=0) for h in range(IDX_HEADS)]
    w_rows = [iwT_ref[h:h + 1, :] * w_scale for h in range(IDX_HEADS)]

    gm_ref[pl.ds(pl.multiple_of(nb * DSA_GB, DSA_GB), DSA_GCHUNK - DSA_GB), :] = jnp.full(
        (DSA_GCHUNK - DSA_GB, TQ), I16_MIN, jnp.int16)

    @pl.when(nb % 2 == 1)
    def _():
        key_ref[pl.ds(pl.multiple_of(nb * DSA_KB, DSA_KB), DSA_KB), :] = jnp.full((DSA_KB, TQ), INT_MIN, I32)

    half = DSA_KB // 2

    def p1(b, c, masked):
        gms = []
        for s2 in range(2):
            k0 = pl.multiple_of(b * DSA_KB + s2 * half, half)
            ikb = ik_ref[pl.ds(k0, half), :]
            sc = jnp.zeros((half, TQ), F32)
            for h in range(IDX_HEADS):
                d = jnp.dot(ikb, iq_pad[h], preferred_element_type=F32)
                sc = sc + w_rows[h] * jnp.maximum(d, 0.0)
            bits = lax.bitcast_convert_type(sc, I32)
            key = jnp.where(bits < 0, INT_MIN - bits, bits)
            if masked:
                kpos = k0 + lax.broadcasted_iota(I32, (half, 1), 0)
                key = jnp.where(kpos <= qpos, key, INT_MIN)
            key_ref[pl.ds(k0, half), :] = key
            k16 = (key >> 16).astype(jnp.int16)
            k16_ref[pl.ds(k0, half), :] = k16
            span = DSA_GROUP * PACK
            for g in range(half // span):
                m = k16[span * g:span * g + PACK]
                for j in range(1, DSA_GROUP):
                    m = _max16(m, k16[span * g + PACK * j:span * g + PACK * (j + 1)])
                gms.append(m)
        gm_ref[pl.ds(pl.multiple_of(b * DSA_GB, DSA_GB), DSA_GB), :] = jnp.concatenate(gms, axis=0)
        return c

    nfull = q0 // DSA_KB
    lax.fori_loop(0, nfull, functools.partial(p1, masked=False), 0)
    lax.fori_loop(nfull, nb, functools.partial(p1, masked=True), 0)

    ngc = (nb * DSA_GB + DSA_GCHUNK - 1) // DSA_GCHUNK

    def gm_bit(it, prefix):
        cand = prefix | jnp.left_shift(jnp.int32(1), 15 - it)
        cnt = _count_ge16(gm_ref, ngc, DSA_GCHUNK, cand + I16_MIN, TQ)
        return jnp.where(cnt >= kf, cand, prefix)

    lo_h = lax.fori_loop(0, 16, gm_bit, jnp.zeros((1, TQ), I32)) + I16_MIN

    def gm_max(c, m):
        r0 = pl.multiple_of(c * DSA_GCHUNK, DSA_GCHUNK)
        for j in range(DSA_GCHUNK // PACK):
            m = _max16(m, gm_ref[pl.ds(r0 + PACK * j, PACK), :])
        return m

    gmax = lax.fori_loop(0, ngc, gm_max, jnp.full((PACK, TQ), I16_MIN, jnp.int16))
    hi_h = jnp.max(gmax.astype(I32), axis=0, keepdims=True) + 1

    def halve(lo_0, hi_0, want, first_probe, n_free):
        def step(it, lo, hi, chi):
            probe = jnp.where(it < 1, first_probe, INT_MIN)
            mid = jnp.where((lo < probe) & (probe < hi), probe, _avg_floor(lo, hi))
            c = _count_ge16(k16_ref, nb, DSA_KB, mid, TQ)
            ge = c >= want
            ex = jnp.logical_and(c == want, mid != lo)
            lo2 = jnp.where(ge, mid, lo)
            hi2 = jnp.where(ex, mid + 1, jnp.where(ge, hi, mid))
            chi2 = jnp.where(ex, EXACT, jnp.where(ge, chi, c))
            return lo2, hi2, chi2

        def active(lo, hi):
            return jnp.max(jnp.where(_avg_floor(lo, hi) != lo, 1.0, 0.0))

        def w_cond(st):
            return jnp.logical_and(st[0] < DSA_MAX_STEPS, st[4] > 0.0)

        def w_body(st):
            lo2, hi2, chi2 = step(st[0], st[1], st[2], st[3])
            return st[0] + 1, lo2, hi2, chi2, active(lo2, hi2)

        lo, hi, chi = lax.fori_loop(0, n_free, lambda it, st: step(it, *st),
                                    (lo_0, hi_0, jnp.zeros((1, TQ), F32)))
        _, lo, _, chi, _ = lax.while_loop(w_cond, w_body, (jnp.int32(n_free), lo, hi, chi, active(lo, hi)))
        return lo, chi

    tau_h, chi_h = halve(lo_h, hi_h, kf, INT_MIN, DSA_FREE_STEPS)
    exact_h = chi_h == EXACT
    want_l = kf - chi_h

    def build(b, c):
        k0 = pl.multiple_of(b * DSA_KB, DSA_KB)
        key = key_ref[pl.ds(k0, DSA_KB), :]
        low = jnp.where((key >> 16) == tau_h, (key & 0xFFFF) + I16_MIN, I16_MIN)
        k16_ref[pl.ds(k0, DSA_KB), :] = low.astype(jnp.int16)
        return c

    lax.fori_loop(0, nb, build, 0)
    tau_l, chi_l = halve(jnp.where(exact_h, 0, I16_MIN), jnp.where(exact_h, 1, -I16_MIN), want_l, I16_MIN + 1,
                         DSA_FREE_STEPS)
    tau = jnp.where(exact_h, tau_h << 16, (tau_h << 16) + (tau_l - I16_MIN))
    r = jnp.where(tau == INT_MIN, 0.0,
                  jnp.where(jnp.logical_or(exact_h, chi_l == EXACT), TIE_ALL, want_l - chi_l))

    AB = DSA_AB
    rowi = lax.broadcasted_iota(I32, (AB, AB), 0)
    coli = lax.broadcasted_iota(I32, (AB, AB), 1)
    tril = jnp.where(coli <= rowi, 1.0, 0.0).astype(BF16)
    aqT = aqT_ref[...]
    hrow = lax.broadcasted_iota(I32, (LANES, TQ), 0) // HEAD_DIM
    q_pad = []
    for h in range(A_HEADS):
        pair = aqT[LANES * (h // 2):LANES * (h // 2 + 1)]
        q_pad.append(jnp.where(hrow == (h % 2), pair, jnp.zeros_like(pair)))
    acc_ref[...] = jnp.zeros_like(acc_ref)

    nsub = 2 * DSA_KB // AB

    def p3(b, carry):
        cnt, ms = carry
        k0 = pl.multiple_of(b * 2 * DSA_KB, 2 * DSA_KB)
        bias = []
        for s2 in range(nsub):
            key = key_ref[pl.ds(k0 + s2 * AB, AB), :]
            eq = key == tau
            pref = jnp.dot(tril, jnp.where(eq, 1.0, 0.0).astype(BF16), preferred_element_type=F32) + cnt
            sel = jnp.where(key > tau, 0.0, jnp.where(eq, pref, 1e9)) <= r
            bias.append(jnp.where(sel, 0.0, NEG_BIG))
            cnt = pref[AB - 1:AB, :]
        lms = [[jnp.dot(ak_ref[pl.ds(k0 + s2 * AB, AB), pl.ds(LANES * (h // 2), LANES)], q_pad[h],
                        preferred_element_type=F32) + bias[s2] for s2 in range(nsub)] for h in range(A_HEADS)]
        ms2, alphas, ps = [], [], []
        for h in range(A_HEADS):
            lm = lms[h]
            bm = functools.reduce(jnp.maximum, [jnp.max(t, axis=0, keepdims=True) for t in lm])
            m_new = jnp.maximum(ms[h], bm)
            ps.append(jnp.concatenate([jnp.exp2(t - m_new).astype(BF16) for t in lm], axis=0))
            alphas.append(jnp.exp2(ms[h] - m_new))
            ms2.append(m_new)
        for h in range(A_HEADS):
            avt = avT_ref[pl.ds(V_EXT * h, V_EXT), pl.ds(k0, 2 * DSA_KB)]
            hs = pl.ds(V_EXT * h, V_EXT)
            acc_ref[hs, :] = alphas[h] * acc_ref[hs, :] + jnp.dot(avt, ps[h], preferred_element_type=F32)
        return cnt, tuple(ms2)

    init = (jnp.zeros((1, TQ), F32), tuple(jnp.full((1, TQ), M_FLOOR, F32) for _ in range(A_HEADS)))
    lax.fori_loop(0, (nb + 1) // 2, p3, init)
    outT = jnp.concatenate([acc_ref[pl.ds(V_EXT * h, HEAD_DIM), :] / acc_ref[pl.ds(V_EXT * h + HEAD_DIM, 1), :]
                            for h in range(A_HEADS)], axis=0)
    o_ref[...] = outT.T.astype(BF16)


def _dsa(aqT, iqT, iwT, ik, ak, avT, S, tq):
    top_k = min(TOPK_MAX, S // 4)
    colT = lambda n: pl.BlockSpec((n, tq), lambda i: (0, i))
    return pl.pallas_call(
        functools.partial(_dsa_kernel, top_k=top_k),
        grid=(S // tq,),
        in_specs=[colT(D_A), colT(D_A), colT(SUBLANES),
                  _resident((S, LANES), lambda i: (0, 0)),
                  _resident((S, D_A), lambda i: (0, 0)),
                  _resident((A_HEADS * V_EXT, S), lambda i: (0, 0))],
        out_specs=pl.BlockSpec((tq, D_A), lambda i: (i, 0)),
        out_shape=jax.ShapeDtypeStruct((S, D_A), BF16),
        scratch_shapes=[pltpu.VMEM((S, tq), I32),
                        pltpu.VMEM((S, tq), jnp.int16),
                        pltpu.VMEM((S // DSA_GROUP + DSA_GCHUNK, tq), jnp.int16),
                        pltpu.VMEM((A_HEADS * V_EXT, tq), F32)],
        compiler_params=_params(("arbitrary",)),
        name="dsa",
    )(aqT, iqT, iwT, ik, ak, avT)


FF_CHUNKS = ((0, 768), (768, 768), (1536, 768), (2304, 512))


def _rms(v, w):
    return v * lax.rsqrt(jnp.mean(v * v, axis=-1, keepdims=True) + EPS) * w


def _outffn_kernel(x_ref, hm_ref, ha_ref, hs_ref, nw_ref, wo_ref, wg_ref, wu_ref, wd_ref, o_ref):
    mix = (jnp.dot(hm_ref[...], wo_ref[0:D_M, :], preferred_element_type=F32)
           + jnp.dot(ha_ref[...], wo_ref[D_M:D_M + D_A, :], preferred_element_type=F32)
           + jnp.dot(hs_ref[...], wo_ref[D_M + D_A:, :], preferred_element_type=F32))
    x1 = x_ref[...] + _rms(mix, nw_ref[1:2, :])
    h2 = _rms(x1, nw_ref[2:3, :]).astype(BF16)
    ff = jnp.zeros_like(x1)
    for c0, n in FF_CHUNKS:
        g = jnp.dot(h2, wg_ref[:, c0:c0 + n], preferred_element_type=F32)
        u = jnp.dot(h2, wu_ref[:, c0:c0 + n], preferred_element_type=F32)
        act = (g * jax.nn.sigmoid(g) * u).astype(BF16)
        ff = ff + jnp.dot(act, wd_ref[c0:c0 + n, :], preferred_element_type=F32)
    o_ref[...] = x1 + _rms(ff, nw_ref[3:4, :])


def _outffn(x2, hm, ha, hs, nw4, wo, wg, wu, wd, S, tm):
    row = lambda n: pl.BlockSpec((tm, n), lambda i: (i, 0))
    return pl.pallas_call(
        _outffn_kernel,
        grid=(S // tm,),
        in_specs=[row(D_MODEL), row(D_M), row(D_A), row(D_S),
                  pl.BlockSpec((4, D_MODEL), lambda i: (0, 0)),
                  _resident((D_MODEL, D_MODEL), lambda i: (0, 0)),
                  _resident((D_MODEL, D_FF), lambda i: (0, 0)),
                  _resident((D_MODEL, D_FF), lambda i: (0, 0)),
                  _resident((D_FF, D_MODEL), lambda i: (0, 0))],
        out_specs=row(D_MODEL),
        out_shape=jax.ShapeDtypeStruct((S, D_MODEL), F32),
        compiler_params=_params(("arbitrary",)),
        name="outffn",
    )(x2, hm, ha, hs, nw4, wo, wg, wu, wd)


def _plan(S):
    assert S % (2 * DSA_KB) == 0
    return dict(tm=min(S, 512), lm=min(S, 256), ls=min(S, 256), tq=min(S, 256))


def _pad_lanes(v):
    return jnp.pad(v, [(0, 0)] * (v.ndim - 1) + [(0, LANES - v.shape[-1])])


def kernel(x, positions, norm_w, w_in, mlstm_gate_bias, mlstm_norm_w, conv_w, conv_b, dt_bias, a_log,
           d_skip, ssd_norm_w, w_out, w_gate, w_up, w_down):
    B, S, D = x.shape
    assert B == 1 and D == D_MODEL
    depth = w_in.shape[0]
    plan = _plan(S)

    o = np.cumsum([0, D_M, D_M, D_M, D_M, M_HEADS, M_HEADS, D_A, D_A, D_A, IDX_HEADS * IDX_DIM, IDX_DIM,
                   IDX_HEADS, D_S, CONV_DIM, S_HEADS])
    wp = jnp.concatenate([
        w_in[:, :, o[0]:o[4]], w_in[:, :, o[6]:o[10]], w_in[:, :, o[12]:o[13]], w_in[:, :, o[13]:o[14]],
        _pad_lanes(w_in[:, :, o[4]:o[6]]), _pad_lanes(w_in[:, :, o[10]:o[12]]), _pad_lanes(w_in[:, :, o[14]:o[15]]),
    ], axis=-1).astype(BF16)
    wo = w_out.astype(BF16)
    wg = w_gate.astype(BF16)
    wu = w_up.astype(BF16)
    wd = w_down.astype(BF16)
    gb = _pad_lanes(mlstm_gate_bias[:, None, :])
    dtb = _pad_lanes(dt_bias[:, None, :])
    alog = _pad_lanes(a_log[:, None, :])
    dsk = jnp.repeat(d_skip, HEAD_DIM, axis=-1)[:, None, :]

    cf, sa, sb = _rope_tables(positions.astype(I32), S)
    x2 = x.reshape(S, D)
    for l in range(depth):
        om, omg, os_, odt, aqT, iqT, iwT, ik, ak, avT = _inproj(
            x2, norm_w[l, 0:1], wp[l], cf, sa, sb, S, plan["tm"])
        hm = _mlstm(om, omg, gb[l], mlstm_norm_w[l][None, :], S, plan["lm"])
        hs = _ssd(os_, odt, conv_w[l], conv_b[l][None, :], dtb[l], alog[l], dsk[l], ssd_norm_w[l][None, :],
                  S, plan["ls"])
        ha = _dsa(aqT, iqT, iwT, ik, ak, avT, S, plan["tq"])
        x2 = _outffn(x2, hm, ha, hs, norm_w[l], wo[l], wg[l], wu[l], wd[l], S, plan["tm"])
    return x2.reshape(B, S, D)
```

```python
import functools

import numpy as np
import jax
import jax.numpy as jnp
from jax import lax
from jax.experimental import pallas as pl
from jax.experimental.pallas import tpu as pltpu

F32 = jnp.float32
BF16 = jnp.bfloat16
I32 = jnp.int32

D_MODEL = 1024
HEAD_DIM = 64
D_M = 256
M_HEADS = 4
D_A = 256
A_HEADS = 4
IDX_HEADS = 4
IDX_DIM = 64
TOPK_MAX = 256
D_S = 512
S_HEADS = 8
S_GROUPS = 2
S_STATE = 128
S_CONV = 4
CONV_DIM = D_S + 2 * S_GROUPS * S_STATE
ROPE_THETA = 500000.0
ROPE_DIM = HEAD_DIM // 4
ROPE_HALF = ROPE_DIM // 2
D_FF = 2816
EPS = 1e-6

LANES = 128
SUBLANES = 8
PACK = 16
VMEM_LIMIT = 56 * 1024 * 1024

C_M = 0
C_A = C_M + 4 * D_M
C_Z = C_A + 4 * D_A
C_XBC = C_Z + D_S
C_MG = C_XBC + CONV_DIM
C_IK = C_MG + LANES
C_DT = C_IK + LANES
N_P = C_DT + LANES

INT_MIN = -2 ** 31
NEG_BIG = -1e30
M_FLOOR = -1e29
LOG2E = 1.4426950408889634
EXP2_HEADROOM = 64.0

DSA_KB = 512
DSA_AB = 256
DSA_GROUP = 16
DSA_GB = DSA_KB // DSA_GROUP
DSA_GCHUNK = 128
V_EXT = HEAD_DIM + PACK
DSA_MAX_STEPS = 24
DSA_FREE_STEPS = 7
TIE_ALL = 1e6
EXACT = -1e9
I16_MIN = -2 ** 15


def _params(sem):
    return pltpu.CompilerParams(dimension_semantics=sem, vmem_limit_bytes=VMEM_LIMIT)


def _resident(shape, index_map):
    return pl.BlockSpec(shape, index_map, pipeline_mode=pl.Buffered(1))


def _split3(x):
    h = x.astype(BF16)
    r = x - h.astype(F32)
    m = r.astype(BF16)
    lo = (r - m.astype(F32)).astype(BF16)
    return h, m, lo


def _cumsum_cols(tril, x):
    return sum(jnp.dot(tril, t, preferred_element_type=F32) for t in _split3(x))


def _cumsum_rows(x, triu):
    return sum(jnp.dot(t, triu, preferred_element_type=F32) for t in _split3(x))


def _tri(L):
    row = lax.broadcasted_iota(I32, (L, L), 0)
    col = lax.broadcasted_iota(I32, (L, L), 1)
    causal = col <= row
    tril = jnp.where(causal, 1.0, 0.0).astype(BF16)
    triu = jnp.where(row <= col, 1.0, 0.0).astype(BF16)
    return causal, tril, triu


def _rope_tables_kernel(pos_ref, inv_ref, cf_ref, sa_ref, sb_ref):
    ang = pos_ref[...].astype(F32) * inv_ref[...]
    c = jnp.cos(ang)
    s = jnp.sin(ang)
    j = lax.broadcasted_iota(I32, ang.shape, 1) & (HEAD_DIM - 1)
    cf_ref[...] = jnp.where(j < ROPE_DIM, c, 1.0)
    sa_ref[...] = jnp.where(j < ROPE_HALF, -s, 0.0)
    sb_ref[...] = jnp.where(j < ROPE_HALF, 0.0, jnp.where(j < ROPE_DIM, s, 0.0))


def _rope_tables(positions, S):
    tb = min(S, 1024)
    inv = np.power(np.float32(ROPE_THETA), -np.arange(ROPE_HALF, dtype=np.float32) / np.float32(ROPE_HALF))
    lane = np.arange(LANES) % HEAD_DIM
    inv_lanes = np.where(lane < ROPE_DIM, inv[lane % ROPE_HALF], np.float32(0)).astype(np.float32)[None, :]
    tab = jax.ShapeDtypeStruct((S, LANES), F32)
    row = pl.BlockSpec((tb, LANES), lambda i: (i, 0))
    return pl.pallas_call(
        _rope_tables_kernel,
        grid=(S // tb,),
        in_specs=[pl.BlockSpec((tb, 1), lambda i: (i, 0)), pl.BlockSpec((1, LANES), lambda i: (0, 0))],
        out_specs=[row, row, row],
        out_shape=[tab, tab, tab],
        compiler_params=_params(("arbitrary",)),
        name="rope_tables",
    )(positions.reshape(S, 1), jnp.asarray(inv_lanes))


def _inproj_kernel(x_ref, nw_ref, w_ref, cf_ref, sa_ref, sb_ref,
                   om_ref, omg_ref, os_ref, odt_ref,
                   aqT_ref, iqT_ref, iwT_ref, ik_ref, ak_ref, avT_ref):
    x = x_ref[...]
    ms = jnp.mean(x * x, axis=-1, keepdims=True)
    h = (x * lax.rsqrt(ms + EPS) * nw_ref[...]).astype(BF16)

    def proj(c0, n):
        return jnp.dot(h, w_ref[:, c0:c0 + n], preferred_element_type=F32)

    om_ref[...] = proj(C_M, 4 * D_M)
    os_ref[...] = proj(C_Z, D_S + CONV_DIM)
    omg_ref[...] = proj(C_MG, LANES)
    odt_ref[...] = proj(C_DT, LANES)

    cf = cf_ref[...]
    sa = sa_ref[...]
    sb = sb_ref[...]

    def rope(c, cf=cf, sa=sa, sb=sb):
        return c * cf + pltpu.roll(c, LANES - ROPE_HALF, 1) * sa + pltpu.roll(c, ROPE_HALF, 1) * sb

    def rope2(a2):
        return jnp.concatenate([rope(a2[:, :LANES]), rope(a2[:, LANES:])], axis=1)

    a = proj(C_A, 4 * D_A)
    aqT_ref[...] = (rope2(a[:, 0:D_A]) * (HEAD_DIM ** -0.5 * LOG2E)).T.astype(BF16)
    ak_ref[...] = rope2(a[:, D_A:2 * D_A]).astype(BF16)
    vT = a[:, 2 * D_A:3 * D_A].T.astype(BF16)
    ones = jnp.ones((V_EXT - HEAD_DIM, vT.shape[1]), BF16)
    avT_ref[...] = jnp.concatenate(
        [t for h in range(A_HEADS) for t in (vT[HEAD_DIM * h:HEAD_DIM * (h + 1)], ones)], axis=0)
    iqT_ref[...] = rope2(a[:, 3 * D_A:4 * D_A]).T.astype(BF16)

    ikw = proj(C_IK, LANES)
    is_ik = lax.broadcasted_iota(I32, ikw.shape, 1) < IDX_DIM
    ikr = rope(ikw, jnp.where(is_ik, cf, 1.0), jnp.where(is_ik, sa, 0.0), jnp.where(is_ik, sb, 0.0))
    ik_ref[...] = jnp.where(is_ik, ikr, 0.0).astype(BF16)
    iwT_ref[...] = ikw.T[IDX_DIM:IDX_DIM + SUBLANES, :]


def _inproj(x2, nw, wp, cf, sa, sb, S, tm):
    row = lambda n: pl.BlockSpec((tm, n), lambda i: (i, 0))
    colT = lambda n: pl.BlockSpec((n, tm), lambda i: (0, i))
    f = lambda n, dt=F32: jax.ShapeDtypeStruct((S, n), dt)
    fT = lambda n, dt=BF16: jax.ShapeDtypeStruct((n, S), dt)
    return pl.pallas_call(
        _inproj_kernel,
        grid=(S // tm,),
        in_specs=[row(D_MODEL), pl.BlockSpec((1, D_MODEL), lambda i: (0, 0)),
                  _resident((D_MODEL, N_P), lambda i: (0, 0)),
                  row(LANES), row(LANES), row(LANES)],
        out_specs=[row(4 * D_M), row(LANES), row(D_S + CONV_DIM), row(LANES),
                   colT(D_A), colT(D_A), colT(SUBLANES), row(LANES), row(D_A), colT(A_HEADS * V_EXT)],
        out_shape=[f(4 * D_M), f(LANES), f(D_S + CONV_DIM), f(LANES),
                   fT(D_A), fT(D_A), fT(SUBLANES, F32), f(LANES, BF16), f(D_A, BF16), fT(A_HEADS * V_EXT)],
        compiler_params=_params(("arbitrary",)),
        name="inproj",
    )(x2, nw, wp, cf, sa, sb)


def _mlstm_kernel(om_ref, omg_ref, gb_ref, nw_ref, o_ref, C_ref, n_ref, m_ref):
    L = om_ref.shape[0]

    @pl.when(pl.program_id(0) == 0)
    def _():
        C_ref[...] = jnp.zeros_like(C_ref)
        n_ref[...] = jnp.zeros_like(n_ref)
        m_ref[...] = jnp.zeros_like(m_ref)

    causal, tril, triu = _tri(L)
    G = omg_ref[...] + gb_ref[...]
    LF = jnp.minimum(G, 0.0) - jnp.log1p(jnp.exp(-jnp.abs(G)))
    Bc = _cumsum_cols(tril, LF)
    GT = G.T
    BrT = _cumsum_rows(LF.T, triu)
    nw = nw_ref[...]
    heads = range(M_HEADS)

    def head(c0, h):
        blk = om_ref[:, pl.ds(c0 + LANES * (h // 2), LANES)]
        return blk[:, HEAD_DIM * (h % 2):HEAD_DIM * (h % 2 + 1)]

    qf = [head(0, h) * (HEAD_DIM ** -0.5) for h in heads]
    kf = [head(D_M, h) for h in heads]
    q = [t.astype(BF16) for t in qf]
    k = [t.astype(BF16) for t in kf]
    v = [head(2 * D_M, h).astype(BF16) for h in heads]
    C_prev = [C_ref[h] for h in heads]
    n_prev = [n_ref[h:h + 1, :] for h in heads]
    m_prev = [m_ref[h:h + 1, 0:1] for h in heads]
    qk = [lax.dot_general(q[h], k[h], (((1,), (1,)), ((), ())), preferred_element_type=F32) for h in heads]
    qC = [jnp.dot(q[h], C_prev[h].astype(BF16), preferred_element_type=F32) for h in heads]
    s, scale, m_t, kw, decay = [], [], [], [], []
    for h in heads:
        f_l = M_HEADS + h
        b_col = Bc[:, f_l:f_l + 1]
        b_row = BrT[f_l:f_l + 1, :]
        i_col = G[:, h:h + 1]
        i_row = GT[h:h + 1, :]
        logd = jnp.where(causal, b_col + (i_row - b_row), -jnp.inf)
        m_inter = b_col + m_prev[h]
        mt = jnp.maximum(m_inter, jnp.max(logd, axis=-1, keepdims=True))
        s.append(qk[h] * jnp.exp(logd - mt))
        scale.append(jnp.exp(m_inter - mt))
        m_t.append(mt)
        b_last = Bc[L - 1:L, f_l:f_l + 1]
        m_new = jnp.maximum(b_last + m_prev[h], jnp.max(b_last - b_row + i_row, axis=-1, keepdims=True))
        kw.append(kf[h] * jnp.exp(b_last - b_col + i_col - m_new))
        decay.append(jnp.exp(b_last + m_prev[h] - m_new))
        m_ref[h:h + 1, :] = jnp.broadcast_to(m_new, (1, LANES))
    sv = [jnp.dot(s[h].astype(BF16), v[h], preferred_element_type=F32) for h in heads]
    kv = [lax.dot_general(kw[h].astype(BF16), v[h], (((0,), (0,)), ((), ())), preferred_element_type=F32)
          for h in heads]
    outs = []
    for h in heads:
        num = sv[h] + scale[h] * qC[h]
        den = (jnp.sum(s[h], axis=-1, keepdims=True)
               + scale[h] * jnp.sum(qf[h] * n_prev[h], axis=-1, keepdims=True))
        hh = num / jnp.maximum(jnp.abs(den), jnp.exp(-m_t[h]))
        C_ref[h] = decay[h] * C_prev[h] + kv[h]
        n_ref[h:h + 1, :] = decay[h] * n_prev[h] + jnp.sum(kw[h], axis=0, keepdims=True)
        y = hh * lax.rsqrt(jnp.mean(hh * hh, axis=-1, keepdims=True) + EPS)
        outs.append(jax.nn.sigmoid(head(3 * D_M, h)) * y)
    o_ref[...] = (jnp.concatenate(outs, axis=1) * nw).astype(BF16)


def _mlstm(om, omg, gb, nw, S, L):
    return pl.pallas_call(
        _mlstm_kernel,
        grid=(S // L,),
        in_specs=[pl.BlockSpec((L, 4 * D_M), lambda i: (i, 0)), pl.BlockSpec((L, LANES), lambda i: (i, 0)),
                  pl.BlockSpec((1, LANES), lambda i: (0, 0)), pl.BlockSpec((1, D_M), lambda i: (0, 0))],
        out_specs=pl.BlockSpec((L, D_M), lambda i: (i, 0)),
        out_shape=jax.ShapeDtypeStruct((S, D_M), BF16),
        scratch_shapes=[pltpu.VMEM((M_HEADS, HEAD_DIM, HEAD_DIM), F32),
                        pltpu.VMEM((SUBLANES, HEAD_DIM), F32),
                        pltpu.VMEM((SUBLANES, LANES), F32)],
        compiler_params=_params(("arbitrary",)),
        name="mlstm",
    )(om, omg, gb, nw)


def _ssd_kernel(os_ref, odt_ref, cw_ref, cb_ref, dtb_ref, alog_ref, dsk_ref, nw_ref, o_ref,
                carry_ref, st_ref):
    L = os_ref.shape[0]

    @pl.when(pl.program_id(0) == 0)
    def _():
        carry_ref[...] = jnp.zeros_like(carry_ref)
        st_ref[...] = jnp.zeros_like(st_ref)

    causal, tril, triu = _tri(L)
    z = os_ref[:, 0:D_S]
    raw = os_ref[:, D_S:D_S + CONV_DIM]
    ext = jnp.concatenate([carry_ref[...], raw], axis=0)
    cw = cw_ref[...]
    xbc = cb_ref[...] + cw[S_CONV - 1:S_CONV, :] * raw
    for j in range(S_CONV - 1):
        off = SUBLANES - (S_CONV - 1) + j
        xbc = xbc + cw[j:j + 1, :] * ext[off:off + L, :]
    carry_ref[...] = raw[L - SUBLANES:L, :]
    xbc = xbc * jax.nn.sigmoid(xbc)

    dtr = odt_ref[...] + dtb_ref[...]
    DT = jnp.maximum(dtr, 0.0) + jnp.log1p(jnp.exp(-jnp.abs(dtr)))
    dA = DT * (-jnp.exp(alog_ref[...]))
    Ac = _cumsum_cols(tril, dA)
    ArT = _cumsum_rows(dA.T, triu)
    DTT = DT.T
    hpg = S_HEADS // S_GROUPS
    ys = []
    for g in range(S_GROUPS):
        Bg = xbc[:, D_S + S_STATE * g:D_S + S_STATE * (g + 1)]
        Cg = xbc[:, D_S + S_GROUPS * S_STATE + S_STATE * g:D_S + S_GROUPS * S_STATE + S_STATE * (g + 1)]
        Cb = Cg.astype(BF16)
        CB = lax.dot_general(Cb, Bg.astype(BF16), (((1,), (1,)), ((), ())), preferred_element_type=F32)
        BgT = Bg.T
        for hh in range(hpg):
            h = g * hpg + hh
            xpair = xbc[:, LANES * (h // 2):LANES * (h // 2 + 1)]
            xh = xpair[:, HEAD_DIM * (h % 2):HEAD_DIM * (h % 2 + 1)]
            ac_col = Ac[:, h:h + 1]
            ac_row = ArT[h:h + 1, :]
            dec = jnp.exp(jnp.where(causal, ac_col - ac_row, -jnp.inf))
            sc = (CB * dec).astype(BF16)
            xdt = (xh * DT[:, h:h + 1]).astype(BF16)
            st = st_ref[h]
            y = (jnp.dot(sc, xdt, preferred_element_type=F32)
                 + jnp.dot(Cb, st.astype(BF16), preferred_element_type=F32) * jnp.exp(ac_col))
            a_last = Ac[L - 1:L, h:h + 1]
            w_row = jnp.exp(a_last - ac_row) * DTT[h:h + 1, :]
            st_ref[h] = st * jnp.exp(a_last) + jnp.dot((BgT * w_row).astype(BF16), xh.astype(BF16),
                                                       preferred_element_type=F32)
            ys.append(y)
    Y = jnp.concatenate(ys, axis=1) + dsk_ref[...] * xbc[:, 0:D_S]
    gated = Y * (z * jax.nn.sigmoid(z))
    gw = D_S // S_GROUPS
    outs = []
    for g in range(S_GROUPS):
        gg = gated[:, gw * g:gw * (g + 1)]
        outs.append(gg * lax.rsqrt(jnp.mean(gg * gg, axis=-1, keepdims=True) + EPS))
    o_ref[...] = (jnp.concatenate(outs, axis=1) * nw_ref[...]).astype(BF16)


def _ssd(os_, odt, cw, cb, dtb, alog, dsk, nw, S, L):
    full = lambda r, c: pl.BlockSpec((r, c), lambda i: (0, 0))
    return pl.pallas_call(
        _ssd_kernel,
        grid=(S // L,),
        in_specs=[pl.BlockSpec((L, D_S + CONV_DIM), lambda i: (i, 0)), pl.BlockSpec((L, LANES), lambda i: (i, 0)),
                  full(S_CONV, CONV_DIM), full(1, CONV_DIM), full(1, LANES), full(1, LANES),
                  full(1, D_S), full(1, D_S)],
        out_specs=pl.BlockSpec((L, D_S), lambda i: (i, 0)),
        out_shape=jax.ShapeDtypeStruct((S, D_S), BF16),
        scratch_shapes=[pltpu.VMEM((SUBLANES, CONV_DIM), F32),
                        pltpu.VMEM((S_HEADS, S_STATE, HEAD_DIM), F32)],
        compiler_params=_params(("arbitrary",)),
        name="ssd",
    )(os_, odt, cw, cb, dtb, alog, dsk, nw)


def _max16(a, b):
    return jnp.where(a >= b, a, b)


def _count_ge16(ref, nblk, rb, thr_row, tq):
    thr = jnp.broadcast_to(thr_row.astype(jnp.int16), (PACK, tq))
    n_acc = 4
    one = jnp.ones((PACK, tq), jnp.int16)

    def body(b, accs):
        slab = ref[pl.ds(pl.multiple_of(b * rb, rb), rb), :]
        accs = list(accs)
        for j in range(rb // PACK):
            blk = slab[PACK * j:PACK * (j + 1)]
            a = accs[j % n_acc]
            accs[j % n_acc] = jnp.where(blk >= thr, a + one, a)
        return tuple(accs)

    accs = lax.fori_loop(0, nblk, body, tuple(jnp.zeros((PACK, tq), jnp.int16) for _ in range(n_acc)))
    tot = (accs[0].astype(I32) + accs[1].astype(I32)) + (accs[2].astype(I32) + accs[3].astype(I32))
    return jnp.sum(tot.astype(F32), axis=0, keepdims=True)


def _avg_floor(lo, hi):
    return (lo >> 1) + (hi >> 1) + (lo & hi & 1)


def _dsa_kernel(aqT_ref, iqT_ref, iwT_ref, ik_ref, ak_ref, avT_ref, o_ref, key_ref, k16_ref, gm_ref, acc_ref,
                *, top_k):
    TQ = aqT_ref.shape[1]
    q0 = pl.program_id(0) * TQ
    nb = (q0 + TQ + DSA_KB - 1) // DSA_KB
    qpos = q0 + lax.broadcasted_iota(I32, (1, TQ), 1)
    kf = float(top_k)
    w_scale = (IDX_HEADS ** -0.5) * (IDX_DIM ** -0.5)

    iqT = iqT_ref[...]
    zpad = jnp.zeros((LANES - IDX_DIM, TQ), BF16)
    iq_pad = [jnp.concatenate([iqT[IDX_DIM * h:IDX_DIM * (h + 1)], zpad], axis=0) for h in range(IDX_HEADS)]
    w_rows = [iwT_ref[h:h + 1, :] * w_scale for h in range(IDX_HEADS)]

    gm_ref[pl.ds(pl.multiple_of(nb * DSA_GB, DSA_GB), DSA_GCHUNK - DSA_GB), :] = jnp.full(
        (DSA_GCHUNK - DSA_GB, TQ), I16_MIN, jnp.int16)

    @pl.when(nb % 2 == 1)
    def _():
        key_ref[pl.ds(pl.multiple_of(nb * DSA_KB, DSA_KB), DSA_KB), :] = jnp.full((DSA_KB, TQ), INT_MIN, I32)

    half = DSA_KB // 2

    def p1(b, c, masked):
        gms = []
        for s2 in range(2):
            k0 = pl.multiple_of(b * DSA_KB + s2 * half, half)
            ikb = ik_ref[pl.ds(k0, half), :]
            sc = jnp.zeros((half, TQ), F32)
            for h in range(IDX_HEADS):
                d = jnp.dot(ikb, iq_pad[h], preferred_element_type=F32)
                sc = sc + w_rows[h] * jnp.maximum(d, 0.0)
            bits = lax.bitcast_convert_type(sc, I32)
            key = jnp.where(bits < 0, INT_MIN - bits, bits)
            if masked:
                kpos = k0 + lax.broadcasted_iota(I32, (half, 1), 0)
                key = jnp.where(kpos <= qpos, key, INT_MIN)
            key_ref[pl.ds(k0, half), :] = key
            k16 = (key >> 16).astype(jnp.int16)
            k16_ref[pl.ds(k0, half), :] = k16
            span = DSA_GROUP * PACK
            for g in range(half // span):
                m = k16[span * g:span * g + PACK]
                for j in range(1, DSA_GROUP):
                    m = _max16(m, k16[span * g + PACK * j:span * g + PACK * (j + 1)])
                gms.append(m)
        gm_ref[pl.ds(pl.multiple_of(b * DSA_GB, DSA_GB), DSA_GB), :] = jnp.concatenate(gms, axis=0)
        return c

    nfull = q0 // DSA_KB
    lax.fori_loop(0, nfull, functools.partial(p1, masked=False), 0)
    lax.fori_loop(nfull, nb, functools.partial(p1, masked=True), 0)

    ngc = (nb * DSA_GB + DSA_GCHUNK - 1) // DSA_GCHUNK

    def gm_bit(it, prefix):
        cand = prefix | jnp.left_shift(jnp.int32(1), 15 - it)
        cnt = _count_ge16(gm_ref, ngc, DSA_GCHUNK, cand + I16_MIN, TQ)
        return jnp.where(cnt >= kf, cand, prefix)

    lo_h = lax.fori_loop(0, 16, gm_bit, jnp.zeros((1, TQ), I32)) + I16_MIN

    def gm_max(c, m):
        r0 = pl.multiple_of(c * DSA_GCHUNK, DSA_GCHUNK)
        for j in range(DSA_GCHUNK // PACK):
            m = _max16(m, gm_ref[pl.ds(r0 + PACK * j, PACK), :])
        return m

    gmax = lax.fori_loop(0, ngc, gm_max, jnp.full((PACK, TQ), I16_MIN, jnp.int16))
    hi_h = jnp.max(gmax.astype(I32), axis=0, keepdims=True) + 1

    def halve(lo_0, hi_0, want, first_probe, n_free):
        def step(it, lo, hi, chi):
            probe = jnp.where(it < 1, first_probe, INT_MIN)
            mid = jnp.where((lo < probe) & (probe < hi), probe, _avg_floor(lo, hi))
            c = _count_ge16(k16_ref, nb, DSA_KB, mid, TQ)
            ge = c >= want
            ex = jnp.logical_and(c == want, mid != lo)
            lo2 = jnp.where(ge, mid, lo)
            hi2 = jnp.where(ex, mid + 1, jnp.where(ge, hi, mid))
            chi2 = jnp.where(ex, EXACT, jnp.where(ge, chi, c))
            return lo2, hi2, chi2

        def active(lo, hi):
            return jnp.max(jnp.where(_avg_floor(lo, hi) != lo, 1.0, 0.0))

        def w_cond(st):
            return jnp.logical_and(st[0] < DSA_MAX_STEPS, st[4] > 0.0)

        def w_body(st):
            lo2, hi2, chi2 = step(st[0], st[1], st[2], st[3])
            return st[0] + 1, lo2, hi2, chi2, active(lo2, hi2)

        lo, hi, chi = lax.fori_loop(0, n_free, lambda it, st: step(it, *st),
                                    (lo_0, hi_0, jnp.zeros((1, TQ), F32)))
        _, lo, _, chi, _ = lax.while_loop(w_cond, w_body, (jnp.int32(n_free), lo, hi, chi, active(lo, hi)))
        return lo, chi

    tau_h, chi_h = halve(lo_h, hi_h, kf, INT_MIN, DSA_FREE_STEPS)
    exact_h = chi_h == EXACT
    want_l = kf - chi_h

    def build(b, c):
        k0 = pl.multiple_of(b * DSA_KB, DSA_KB)
        key = key_ref[pl.ds(k0, DSA_KB), :]
        low = jnp.where((key >> 16) == tau_h, (key & 0xFFFF) + I16_MIN, I16_MIN)
        k16_ref[pl.ds(k0, DSA_KB), :] = low.astype(jnp.int16)
        return c

    lax.fori_loop(0, nb, build, 0)
    tau_l, chi_l = halve(jnp.where(exact_h, 0, I16_MIN), jnp.where(exact_h, 1, -I16_MIN), want_l, I16_MIN + 1,
                         DSA_FREE_STEPS)
    tau = jnp.where(exact_h, tau_h << 16, (tau_h << 16) + (tau_l - I16_MIN))
    r = jnp.where(tau == INT_MIN, 0.0,
                  jnp.where(jnp.logical_or(exact_h, chi_l == EXACT), TIE_ALL, want_l - chi_l))

    AB = DSA_AB
    rowi = lax.broadcasted_iota(I32, (AB, AB), 0)
    coli = lax.broadcasted_iota(I32, (AB, AB), 1)
    tril = jnp.where(coli <= rowi, 1.0, 0.0).astype(BF16)
    aqT = aqT_ref[...]
    hrow = lax.broadcasted_iota(I32, (LANES, TQ), 0) // HEAD_DIM
    q_pad = []
    for h in range(A_HEADS):
        pair = aqT[LANES * (h // 2):LANES * (h // 2 + 1)]
        q_pad.append(jnp.where(hrow == (h % 2), pair, jnp.zeros_like(pair)))
    acc_ref[...] = jnp.zeros_like(acc_ref)

    nsub = 2 * DSA_KB // AB

    def p3(b, carry):
        cnt, ms = carry
        k0 = pl.multiple_of(b * 2 * DSA_KB, 2 * DSA_KB)
        bias = []
        for s2 in range(nsub):
            key = key_ref[pl.ds(k0 + s2 * AB, AB), :]
            eq = key == tau
            pref = jnp.dot(tril, jnp.where(eq, 1.0, 0.0).astype(BF16), preferred_element_type=F32) + cnt
            sel = jnp.where(key > tau, 0.0, jnp.where(eq, pref, 1e9)) <= r
            bias.append(jnp.where(sel, 0.0, NEG_BIG))
            cnt = pref[AB - 1:AB, :]
        def logits(h):
            return [jnp.dot(ak_ref[pl.ds(k0 + s2 * AB, AB), pl.ds(LANES * (h // 2), LANES)], q_pad[h],
                            preferred_element_type=F32) + bias[s2] for s2 in range(nsub)]

        def weights(lm, m):
            return jnp.concatenate([jnp.exp2(t - m).astype(BF16) for t in lm], axis=0)

        def values(h):
            return avT_ref[pl.ds(V_EXT * h, V_EXT), pl.ds(k0, 2 * DSA_KB)]

        lms = [logits(h) for h in range(A_HEADS)]
        ps, bms = [], []
        for h in range(A_HEADS):
            ps.append(weights(lms[h], ms[h]))
            bms.append(functools.reduce(jnp.maximum, [jnp.max(t, axis=0, keepdims=True) for t in lms[h]]))
        pvs = [jnp.dot(values(h), ps[h], preferred_element_type=F32) for h in range(A_HEADS)]
        ms2 = [jnp.maximum(ms[h], bms[h]) for h in range(A_HEADS)]
        rise = jnp.max(functools.reduce(jnp.maximum, [ms2[h] - ms[h] for h in range(A_HEADS)]))

        @pl.when(rise <= EXP2_HEADROOM)
        def _():
            for h in range(A_HEADS):
                hs = pl.ds(V_EXT * h, V_EXT)
                acc_ref[hs, :] = (acc_ref[hs, :] + pvs[h]) * jnp.exp2(ms[h] - ms2[h])

        @pl.when(rise > EXP2_HEADROOM)
        def _():
            for h in range(A_HEADS):
                hs = pl.ds(V_EXT * h, V_EXT)
                pv = jnp.dot(values(h), weights(logits(h), ms2[h]), preferred_element_type=F32)
                acc_ref[hs, :] = acc_ref[hs, :] * jnp.exp2(ms[h] - ms2[h]) + pv

        return cnt, tuple(ms2)

    init = (jnp.zeros((1, TQ), F32), tuple(jnp.full((1, TQ), M_FLOOR, F32) for _ in range(A_HEADS)))
    lax.fori_loop(0, (nb + 1) // 2, p3, init)
    outT = jnp.concatenate([acc_ref[pl.ds(V_EXT * h, HEAD_DIM), :] / acc_ref[pl.ds(V_EXT * h + HEAD_DIM, 1), :]
                            for h in range(A_HEADS)], axis=0)
    o_ref[...] = outT.T.astype(BF16)


def _dsa(aqT, iqT, iwT, ik, ak, avT, S, tq):
    top_k = min(TOPK_MAX, S // 4)
    colT = lambda n: pl.BlockSpec((n, tq), lambda i: (0, i))
    return pl.pallas_call(
        functools.partial(_dsa_kernel, top_k=top_k),
        grid=(S // tq,),
        in_specs=[colT(D_A), colT(D_A), colT(SUBLANES),
                  _resident((S, LANES), lambda i: (0, 0)),
                  _resident((S, D_A), lambda i: (0, 0)),
                  _resident((A_HEADS * V_EXT, S), lambda i: (0, 0))],
        out_specs=pl.BlockSpec((tq, D_A), lambda i: (i, 0)),
        out_shape=jax.ShapeDtypeStruct((S, D_A), BF16),
        scratch_shapes=[pltpu.VMEM((S, tq), I32),
                        pltpu.VMEM((S, tq), jnp.int16),
                        pltpu.VMEM((S // DSA_GROUP + DSA_GCHUNK, tq), jnp.int16),
                        pltpu.VMEM((A_HEADS * V_EXT, tq), F32)],
        compiler_params=_params(("arbitrary",)),
        name="dsa",
    )(aqT, iqT, iwT, ik, ak, avT)


FF_CHUNKS = ((0, 768), (768, 768), (1536, 768), (2304, 512))


def _rms(v, w):
    return v * lax.rsqrt(jnp.mean(v * v, axis=-1, keepdims=True) + EPS) * w


def _outffn_kernel(x_ref, hm_ref, ha_ref, hs_ref, nw_ref, wo_ref, wg_ref, wu_ref, wd_ref, o_ref):
    mix = (jnp.dot(hm_ref[...], wo_ref[0:D_M, :], preferred_element_type=F32)
           + jnp.dot(ha_ref[...], wo_ref[D_M:D_M + D_A, :], preferred_element_type=F32)
           + jnp.dot(hs_ref[...], wo_ref[D_M + D_A:, :], preferred_element_type=F32))
    x1 = x_ref[...] + _rms(mix, nw_ref[1:2, :])
    h2 = _rms(x1, nw_ref[2:3, :]).astype(BF16)
    ff = jnp.zeros_like(x1)
    for c0, n in FF_CHUNKS:
        g = jnp.dot(h2, wg_ref[:, c0:c0 + n], preferred_element_type=F32)
        u = jnp.dot(h2, wu_ref[:, c0:c0 + n], preferred_element_type=F32)
        act = (g * jax.nn.sigmoid(g) * u).astype(BF16)
        ff = ff + jnp.dot(act, wd_ref[c0:c0 + n, :], preferred_element_type=F32)
    o_ref[...] = x1 + _rms(ff, nw_ref[3:4, :])


def _outffn(x2, hm, ha, hs, nw4, wo, wg, wu, wd, S, tm):
    row = lambda n: pl.BlockSpec((tm, n), lambda i: (i, 0))
    return pl.pallas_call(
        _outffn_kernel,
        grid=(S // tm,),
        in_specs=[row(D_MODEL), row(D_M), row(D_A), row(D_S),
                  pl.BlockSpec((4, D_MODEL), lambda i: (0, 0)),
                  _resident((D_MODEL, D_MODEL), lambda i: (0, 0)),
                  _resident((D_MODEL, D_FF), lambda i: (0, 0)),
                  _resident((D_MODEL, D_FF), lambda i: (0, 0)),
                  _resident((D_FF, D_MODEL), lambda i: (0, 0))],
        out_specs=row(D_MODEL),
        out_shape=jax.ShapeDtypeStruct((S, D_MODEL), F32),
        compiler_params=_params(("arbitrary",)),
        name="outffn",
    )(x2, hm, ha, hs, nw4, wo, wg, wu, wd)


def _plan(S):
    assert S % (2 * DSA_KB) == 0
    return dict(tm=min(S, 512), lm=min(S, 256), ls=min(S, 256), tq=min(S, 256))


def _pad_lanes(v):
    return jnp.pad(v, [(0, 0)] * (v.ndim - 1) + [(0, LANES - v.shape[-1])])


def kernel(x, positions, norm_w, w_in, mlstm_gate_bias, mlstm_norm_w, conv_w, conv_b, dt_bias, a_log,
           d_skip, ssd_norm_w, w_out, w_gate, w_up, w_down):
    B, S, D = x.shape
    assert B == 1 and D == D_MODEL
    depth = w_in.shape[0]
    plan = _plan(S)

    o = np.cumsum([0, D_M, D_M, D_M, D_M, M_HEADS, M_HEADS, D_A, D_A, D_A, IDX_HEADS * IDX_DIM, IDX_DIM,
                   IDX_HEADS, D_S, CONV_DIM, S_HEADS])
    wp = jnp.concatenate([
        w_in[:, :, o[0]:o[4]], w_in[:, :, o[6]:o[10]], w_in[:, :, o[12]:o[13]], w_in[:, :, o[13]:o[14]],
        _pad_lanes(w_in[:, :, o[4]:o[6]]), _pad_lanes(w_in[:, :, o[10]:o[12]]), _pad_lanes(w_in[:, :, o[14]:o[15]]),
    ], axis=-1).astype(BF16)
    wo = w_out.astype(BF16)
    wg = w_gate.astype(BF16)
    wu = w_up.astype(BF16)
    wd = w_down.astype(BF16)
    gb = _pad_lanes(mlstm_gate_bias[:, None, :])
    dtb = _pad_lanes(dt_bias[:, None, :])
    alog = _pad_lanes(a_log[:, None, :])
    dsk = jnp.repeat(d_skip, HEAD_DIM, axis=-1)[:, None, :]

    cf, sa, sb = _rope_tables(positions.astype(I32), S)
    x2 = x.reshape(S, D)
    for l in range(depth):
        om, omg, os_, odt, aqT, iqT, iwT, ik, ak, avT = _inproj(
            x2, norm_w[l, 0:1], wp[l], cf, sa, sb, S, plan["tm"])
        hm = _mlstm(om, omg, gb[l], mlstm_norm_w[l][None, :], S, plan["lm"])
        hs = _ssd(os_, odt, conv_w[l], conv_b[l][None, :], dtb[l], alog[l], dsk[l], ssd_norm_w[l][None, :],
                  S, plan["ls"])
        ha = _dsa(aqT, iqT, iwT, ik, ak, avT, S, plan["tq"])
        x2 = _outffn(x2, hm, ha, hs, norm_w[l], wo[l], wg[l], wu[l], wd[l], S, plan["tm"])
    return x2.reshape(B, S, D)
```

```python
import functools

import numpy as np
import jax
import jax.numpy as jnp
from jax import lax
from jax.experimental import pallas as pl
from jax.experimental.pallas import tpu as pltpu

F32 = jnp.float32
BF16 = jnp.bfloat16
I32 = jnp.int32

D_MODEL = 1024
HEAD_DIM = 64
D_M = 256
M_HEADS = 4
D_A = 256
A_HEADS = 4
IDX_HEADS = 4
IDX_DIM = 64
TOPK_MAX = 256
D_S = 512
S_HEADS = 8
S_GROUPS = 2
S_STATE = 128
S_CONV = 4
CONV_DIM = D_S + 2 * S_GROUPS * S_STATE
ROPE_THETA = 500000.0
ROPE_DIM = HEAD_DIM // 4
ROPE_HALF = ROPE_DIM // 2
D_FF = 2816
EPS = 1e-6

LANES = 128
SUBLANES = 8
PACK = 16
VMEM_LIMIT = 60 * 1024 * 1024

C_M = 0
C_A = C_M + 4 * D_M
C_Z = C_A + 4 * D_A
C_XBC = C_Z + D_S
C_MG = C_XBC + CONV_DIM
C_IK = C_MG + LANES
C_DT = C_IK + LANES
N_P = C_DT + LANES

INT_MIN = -2 ** 31
NEG_BIG = -1e30
M_FLOOR = -1e29
LOG2E = 1.4426950408889634
EXP2_HEADROOM = 64.0

DSA_KB = 512
DSA_AB = 256
DSA_GROUP = 16
DSA_GB = DSA_KB // DSA_GROUP
DSA_GCHUNK = 128
V_EXT = HEAD_DIM + PACK
DSA_MAX_STEPS = 24
DSA_FREE_STEPS = 7
TIE_ALL = 1e6
EXACT = -1e9
I16_MIN = -2 ** 15


def _params(sem):
    return pltpu.CompilerParams(dimension_semantics=sem, vmem_limit_bytes=VMEM_LIMIT)


def _resident(shape, index_map):
    return pl.BlockSpec(shape, index_map, pipeline_mode=pl.Buffered(1))


def _split3(x):
    h = x.astype(BF16)
    r = x - h.astype(F32)
    m = r.astype(BF16)
    lo = (r - m.astype(F32)).astype(BF16)
    return h, m, lo


def _cumsum_cols(tril, x):
    return sum(jnp.dot(tril, t, preferred_element_type=F32) for t in _split3(x))


def _cumsum_rows(x, triu):
    return sum(jnp.dot(t, triu, preferred_element_type=F32) for t in _split3(x))


def _tri(L):
    row = lax.broadcasted_iota(I32, (L, L), 0)
    col = lax.broadcasted_iota(I32, (L, L), 1)
    causal = col <= row
    tril = jnp.where(causal, 1.0, 0.0).astype(BF16)
    triu = jnp.where(row <= col, 1.0, 0.0).astype(BF16)
    return causal, tril, triu


def _rope_tables_kernel(pos_ref, inv_ref, cf_ref, sa_ref, sb_ref):
    ang = pos_ref[...].astype(F32) * inv_ref[...]
    c = jnp.cos(ang)
    s = jnp.sin(ang)
    j = lax.broadcasted_iota(I32, ang.shape, 1) & (HEAD_DIM - 1)
    cf_ref[...] = jnp.where(j < ROPE_DIM, c, 1.0)
    sa_ref[...] = jnp.where(j < ROPE_HALF, -s, 0.0)
    sb_ref[...] = jnp.where(j < ROPE_HALF, 0.0, jnp.where(j < ROPE_DIM, s, 0.0))


def _rope_tables(positions, S):
    tb = min(S, 1024)
    inv = np.power(np.float32(ROPE_THETA), -np.arange(ROPE_HALF, dtype=np.float32) / np.float32(ROPE_HALF))
    lane = np.arange(LANES) % HEAD_DIM
    inv_lanes = np.where(lane < ROPE_DIM, inv[lane % ROPE_HALF], np.float32(0)).astype(np.float32)[None, :]
    tab = jax.ShapeDtypeStruct((S, LANES), F32)
    row = pl.BlockSpec((tb, LANES), lambda i: (i, 0))
    return pl.pallas_call(
        _rope_tables_kernel,
        grid=(S // tb,),
        in_specs=[pl.BlockSpec((tb, 1), lambda i: (i, 0)), pl.BlockSpec((1, LANES), lambda i: (0, 0))],
        out_specs=[row, row, row],
        out_shape=[tab, tab, tab],
        compiler_params=_params(("arbitrary",)),
        name="rope_tables",
    )(positions.reshape(S, 1), jnp.asarray(inv_lanes))


def _inproj_kernel(x_ref, nw_ref, w_ref, cf_ref, sa_ref, sb_ref,
                   om_ref, omg_ref, os_ref, odt_ref,
                   aqT_ref, iqT_ref, iwT_ref, ik_ref, ak_ref, avT_ref):
    x = x_ref[...]
    ms = jnp.mean(x * x, axis=-1, keepdims=True)
    h = (x * lax.rsqrt(ms + EPS) * nw_ref[...]).astype(BF16)

    def proj(c0, n):
        return jnp.dot(h, w_ref[:, c0:c0 + n], preferred_element_type=F32)

    om_ref[...] = proj(C_M, 4 * D_M)
    os_ref[...] = proj(C_Z, D_S + CONV_DIM)
    omg_ref[...] = proj(C_MG, LANES)
    odt_ref[...] = proj(C_DT, LANES)

    cf = cf_ref[...]
    sa = sa_ref[...]
    sb = sb_ref[...]

    def rope(c, cf=cf, sa=sa, sb=sb):
        return c * cf + pltpu.roll(c, LANES - ROPE_HALF, 1) * sa + pltpu.roll(c, ROPE_HALF, 1) * sb

    def rope2(a2):
        return jnp.concatenate([rope(a2[:, :LANES]), rope(a2[:, LANES:])], axis=1)

    a = proj(C_A, 4 * D_A)
    aqT_ref[...] = (rope2(a[:, 0:D_A]) * (HEAD_DIM ** -0.5 * LOG2E)).T.astype(BF16)
    ak_ref[...] = rope2(a[:, D_A:2 * D_A]).astype(BF16)
    vT = a[:, 2 * D_A:3 * D_A].T.astype(BF16)
    ones = jnp.ones((V_EXT - HEAD_DIM, vT.shape[1]), BF16)
    avT_ref[...] = jnp.concatenate(
        [t for h in range(A_HEADS) for t in (vT[HEAD_DIM * h:HEAD_DIM * (h + 1)], ones)], axis=0)
    iqT_ref[...] = rope2(a[:, 3 * D_A:4 * D_A]).T.astype(BF16)

    ikw = proj(C_IK, LANES)
    is_ik = lax.broadcasted_iota(I32, ikw.shape, 1) < IDX_DIM
    ikr = rope(ikw, jnp.where(is_ik, cf, 1.0), jnp.where(is_ik, sa, 0.0), jnp.where(is_ik, sb, 0.0))
    ik_ref[...] = jnp.where(is_ik, ikr, 0.0).astype(BF16)
    iwT_ref[...] = ikw.T[IDX_DIM:IDX_DIM + SUBLANES, :]


def _inproj(x2, nw, wp, cf, sa, sb, S, tm):
    row = lambda n: pl.BlockSpec((tm, n), lambda i: (i, 0))
    colT = lambda n: pl.BlockSpec((n, tm), lambda i: (0, i))
    f = lambda n, dt=F32: jax.ShapeDtypeStruct((S, n), dt)
    fT = lambda n, dt=BF16: jax.ShapeDtypeStruct((n, S), dt)
    return pl.pallas_call(
        _inproj_kernel,
        grid=(S // tm,),
        in_specs=[row(D_MODEL), pl.BlockSpec((1, D_MODEL), lambda i: (0, 0)),
                  _resident((D_MODEL, N_P), lambda i: (0, 0)),
                  row(LANES), row(LANES), row(LANES)],
        out_specs=[row(4 * D_M), row(LANES), row(D_S + CONV_DIM), row(LANES),
                   colT(D_A), colT(D_A), colT(SUBLANES), row(LANES), row(D_A), colT(A_HEADS * V_EXT)],
        out_shape=[f(4 * D_M), f(LANES), f(D_S + CONV_DIM), f(LANES),
                   fT(D_A), fT(D_A), fT(SUBLANES, F32), f(LANES, BF16), f(D_A, BF16), fT(A_HEADS * V_EXT)],
        compiler_params=_params(("arbitrary",)),
        name="inproj",
    )(x2, nw, wp, cf, sa, sb)


def _mlstm_kernel(om_ref, omg_ref, gb_ref, nw_ref, o_ref, C_ref, n_ref, m_ref):
    L = om_ref.shape[0]

    @pl.when(pl.program_id(0) == 0)
    def _():
        C_ref[...] = jnp.zeros_like(C_ref)
        n_ref[...] = jnp.zeros_like(n_ref)
        m_ref[...] = jnp.zeros_like(m_ref)

    causal, tril, triu = _tri(L)
    G = omg_ref[...] + gb_ref[...]
    LF = jnp.minimum(G, 0.0) - jnp.log1p(jnp.exp(-jnp.abs(G)))
    Bc = _cumsum_cols(tril, LF)
    GT = G.T
    BrT = _cumsum_rows(LF.T, triu)
    nw = nw_ref[...]
    heads = range(M_HEADS)

    def head(c0, h):
        blk = om_ref[:, pl.ds(c0 + LANES * (h // 2), LANES)]
        return blk[:, HEAD_DIM * (h % 2):HEAD_DIM * (h % 2 + 1)]

    qf = [head(0, h) * (HEAD_DIM ** -0.5) for h in heads]
    kf = [head(D_M, h) for h in heads]
    q = [t.astype(BF16) for t in qf]
    k = [t.astype(BF16) for t in kf]
    v = [head(2 * D_M, h).astype(BF16) for h in heads]
    C_prev = [C_ref[h] for h in heads]
    n_prev = [n_ref[h:h + 1, :] for h in heads]
    m_prev = [m_ref[h:h + 1, 0:1] for h in heads]
    qk = [lax.dot_general(q[h], k[h], (((1,), (1,)), ((), ())), preferred_element_type=F32) for h in heads]
    qC = [jnp.dot(q[h], C_prev[h].astype(BF16), preferred_element_type=F32) for h in heads]
    s, scale, m_t, kw, decay = [], [], [], [], []
    for h in heads:
        f_l = M_HEADS + h
        b_col = Bc[:, f_l:f_l + 1]
        b_row = BrT[f_l:f_l + 1, :]
        i_col = G[:, h:h + 1]
        i_row = GT[h:h + 1, :]
        logd = jnp.where(causal, b_col + (i_row - b_row), -jnp.inf)
        m_inter = b_col + m_prev[h]
        mt = jnp.maximum(m_inter, jnp.max(logd, axis=-1, keepdims=True))
        s.append(qk[h] * jnp.exp(logd - mt))
        scale.append(jnp.exp(m_inter - mt))
        m_t.append(mt)
        b_last = Bc[L - 1:L, f_l:f_l + 1]
        m_new = jnp.maximum(b_last + m_prev[h], jnp.max(b_last - b_row + i_row, axis=-1, keepdims=True))
        kw.append(kf[h] * jnp.exp(b_last - b_col + i_col - m_new))
        decay.append(jnp.exp(b_last + m_prev[h] - m_new))
        m_ref[h:h + 1, :] = jnp.broadcast_to(m_new, (1, LANES))
    sv = [jnp.dot(s[h].astype(BF16), v[h], preferred_element_type=F32) for h in heads]
    kv = [lax.dot_general(kw[h].astype(BF16), v[h], (((0,), (0,)), ((), ())), preferred_element_type=F32)
          for h in heads]
    outs = []
    for h in heads:
        num = sv[h] + scale[h] * qC[h]
        den = (jnp.sum(s[h], axis=-1, keepdims=True)
               + scale[h] * jnp.sum(qf[h] * n_prev[h], axis=-1, keepdims=True))
        hh = num / jnp.maximum(jnp.abs(den), jnp.exp(-m_t[h]))
        C_ref[h] = decay[h] * C_prev[h] + kv[h]
        n_ref[h:h + 1, :] = decay[h] * n_prev[h] + jnp.sum(kw[h], axis=0, keepdims=True)
        y = hh * lax.rsqrt(jnp.mean(hh * hh, axis=-1, keepdims=True) + EPS)
        outs.append(jax.nn.sigmoid(head(3 * D_M, h)) * y)
    o_ref[...] = (jnp.concatenate(outs, axis=1) * nw).astype(BF16)


def _mlstm(om, omg, gb, nw, S, L):
    return pl.pallas_call(
        _mlstm_kernel,
        grid=(S // L,),
        in_specs=[pl.BlockSpec((L, 4 * D_M), lambda i: (i, 0)), pl.BlockSpec((L, LANES), lambda i: (i, 0)),
                  pl.BlockSpec((1, LANES), lambda i: (0, 0)), pl.BlockSpec((1, D_M), lambda i: (0, 0))],
        out_specs=pl.BlockSpec((L, D_M), lambda i: (i, 0)),
        out_shape=jax.ShapeDtypeStruct((S, D_M), BF16),
        scratch_shapes=[pltpu.VMEM((M_HEADS, HEAD_DIM, HEAD_DIM), F32),
                        pltpu.VMEM((SUBLANES, HEAD_DIM), F32),
                        pltpu.VMEM((SUBLANES, LANES), F32)],
        compiler_params=_params(("arbitrary",)),
        name="mlstm",
    )(om, omg, gb, nw)


def _ssd_kernel(os_ref, odt_ref, cw_ref, cb_ref, dtb_ref, alog_ref, dsk_ref, nw_ref, o_ref,
                carry_ref, st_ref):
    L = os_ref.shape[0]

    @pl.when(pl.program_id(0) == 0)
    def _():
        carry_ref[...] = jnp.zeros_like(carry_ref)
        st_ref[...] = jnp.zeros_like(st_ref)

    causal, tril, triu = _tri(L)
    z = os_ref[:, 0:D_S]
    raw = os_ref[:, D_S:D_S + CONV_DIM]
    ext = jnp.concatenate([carry_ref[...], raw], axis=0)
    cw = cw_ref[...]
    xbc = cb_ref[...] + cw[S_CONV - 1:S_CONV, :] * raw
    for j in range(S_CONV - 1):
        off = SUBLANES - (S_CONV - 1) + j
        xbc = xbc + cw[j:j + 1, :] * ext[off:off + L, :]
    carry_ref[...] = raw[L - SUBLANES:L, :]
    xbc = xbc * jax.nn.sigmoid(xbc)

    dtr = odt_ref[...] + dtb_ref[...]
    DT = jnp.maximum(dtr, 0.0) + jnp.log1p(jnp.exp(-jnp.abs(dtr)))
    dA = DT * (-jnp.exp(alog_ref[...]))
    Ac = _cumsum_cols(tril, dA)
    ArT = _cumsum_rows(dA.T, triu)
    DTT = DT.T
    hpg = S_HEADS // S_GROUPS
    ys = []
    for g in range(S_GROUPS):
        Bg = xbc[:, D_S + S_STATE * g:D_S + S_STATE * (g + 1)]
        Cg = xbc[:, D_S + S_GROUPS * S_STATE + S_STATE * g:D_S + S_GROUPS * S_STATE + S_STATE * (g + 1)]
        Cb = Cg.astype(BF16)
        CB = lax.dot_general(Cb, Bg.astype(BF16), (((1,), (1,)), ((), ())), preferred_element_type=F32)
        BgT = Bg.T
        for hh in range(hpg):
            h = g * hpg + hh
            xpair = xbc[:, LANES * (h // 2):LANES * (h // 2 + 1)]
            xh = xpair[:, HEAD_DIM * (h % 2):HEAD_DIM * (h % 2 + 1)]
            ac_col = Ac[:, h:h + 1]
            ac_row = ArT[h:h + 1, :]
            dec = jnp.exp(jnp.where(causal, ac_col - ac_row, -jnp.inf))
            sc = (CB * dec).astype(BF16)
            xdt = (xh * DT[:, h:h + 1]).astype(BF16)
            st = st_ref[h]
            y = (jnp.dot(sc, xdt, preferred_element_type=F32)
                 + jnp.dot(Cb, st.astype(BF16), preferred_element_type=F32) * jnp.exp(ac_col))
            a_last = Ac[L - 1:L, h:h + 1]
            w_row = jnp.exp(a_last - ac_row) * DTT[h:h + 1, :]
            st_ref[h] = st * jnp.exp(a_last) + jnp.dot((BgT * w_row).astype(BF16), xh.astype(BF16),
                                                       preferred_element_type=F32)
            ys.append(y)
    Y = jnp.concatenate(ys, axis=1) + dsk_ref[...] * xbc[:, 0:D_S]
    gated = Y * (z * jax.nn.sigmoid(z))
    gw = D_S // S_GROUPS
    outs = []
    for g in range(S_GROUPS):
        gg = gated[:, gw * g:gw * (g + 1)]
        outs.append(gg * lax.rsqrt(jnp.mean(gg * gg, axis=-1, keepdims=True) + EPS))
    o_ref[...] = (jnp.concatenate(outs, axis=1) * nw_ref[...]).astype(BF16)


def _ssd(os_, odt, cw, cb, dtb, alog, dsk, nw, S, L):
    full = lambda r, c: pl.BlockSpec((r, c), lambda i: (0, 0))
    return pl.pallas_call(
        _ssd_kernel,
        grid=(S // L,),
        in_specs=[pl.BlockSpec((L, D_S + CONV_DIM), lambda i: (i, 0)), pl.BlockSpec((L, LANES), lambda i: (i, 0)),
                  full(S_CONV, CONV_DIM), full(1, CONV_DIM), full(1, LANES), full(1, LANES),
                  full(1, D_S), full(1, D_S)],
        out_specs=pl.BlockSpec((L, D_S), lambda i: (i, 0)),
        out_shape=jax.ShapeDtypeStruct((S, D_S), BF16),
        scratch_shapes=[pltpu.VMEM((SUBLANES, CONV_DIM), F32),
                        pltpu.VMEM((S_HEADS, S_STATE, HEAD_DIM), F32)],
        compiler_params=_params(("arbitrary",)),
        name="ssd",
    )(os_, odt, cw, cb, dtb, alog, dsk, nw)


def _max16(a, b):
    return jnp.where(a >= b, a, b)


def _count_ge16(ref, nblk, rb, thr_row, tq):
    thr = jnp.broadcast_to(thr_row.astype(jnp.int16), (PACK, tq))
    n_acc = 4
    one = jnp.ones((PACK, tq), jnp.int16)

    def body(b, accs):
        slab = ref[pl.ds(pl.multiple_of(b * rb, rb), rb), :]
        accs = list(accs)
        for j in range(rb // PACK):
            blk = slab[PACK * j:PACK * (j + 1)]
            a = accs[j % n_acc]
            accs[j % n_acc] = jnp.where(blk >= thr, a + one, a)
        return tuple(accs)

    accs = lax.fori_loop(0, nblk, body, tuple(jnp.zeros((PACK, tq), jnp.int16) for _ in range(n_acc)))
    tot = (accs[0].astype(I32) + accs[1].astype(I32)) + (accs[2].astype(I32) + accs[3].astype(I32))
    return jnp.sum(tot.astype(F32), axis=0, keepdims=True)


def _avg_floor(lo, hi):
    return (lo >> 1) + (hi >> 1) + (lo & hi & 1)


def _dsa_kernel(aqT_ref, iqT_ref, iwT_ref, ik_ref, ak_ref, avT_ref, o_ref, key_ref, k16_ref, gm_ref, acc_ref,
                *, top_k):
    TQ = aqT_ref.shape[1]
    q0 = pl.program_id(0) * TQ
    nb = (q0 + TQ + DSA_KB - 1) // DSA_KB
    qpos = q0 + lax.broadcasted_iota(I32, (1, TQ), 1)
    kf = float(top_k)
    w_scale = (IDX_HEADS ** -0.5) * (IDX_DIM ** -0.5)

    iqT = iqT_ref[...]
    zpad = jnp.zeros((LANES - IDX_DIM, TQ), BF16)
    iq_pad = [jnp.concatenate([iqT[IDX_DIM * h:IDX_DIM * (h + 1)], zpad], axis=0) for h in range(IDX_HEADS)]
    w_rows = [iwT_ref[h:h + 1, :] * w_scale for h in range(IDX_HEADS)]

    gm_ref[pl.ds(pl.multiple_of(nb * DSA_GB, DSA_GB), DSA_GCHUNK - DSA_GB), :] = jnp.full(
        (DSA_GCHUNK - DSA_GB, TQ), I16_MIN, jnp.int16)

    @pl.when(nb % 2 == 1)
    def _():
        key_ref[pl.ds(pl.multiple_of(nb * DSA_KB, DSA_KB), DSA_KB), :] = jnp.full((DSA_KB, TQ), INT_MIN, I32)

    half = DSA_KB // 2

    def p1(b, c, masked):
        gms = []
        for s2 in range(2):
            k0 = pl.multiple_of(b * DSA_KB + s2 * half, half)
            ikb = ik_ref[pl.ds(k0, half), :]
            sc = jnp.zeros((half, TQ), F32)
            for h in range(IDX_HEADS):
                d = jnp.dot(ikb, iq_pad[h], preferred_element_type=F32)
                sc = sc + w_rows[h] * jnp.maximum(d, 0.0)
            bits = lax.bitcast_convert_type(sc, I32)
            key = jnp.where(bits < 0, INT_MIN - bits, bits)
            if masked:
                kpos = k0 + lax.broadcasted_iota(I32, (half, 1), 0)
                key = jnp.where(kpos <= qpos, key, INT_MIN)
            key_ref[pl.ds(k0, half), :] = key
            k16 = (key >> 16).astype(jnp.int16)
            k16_ref[pl.ds(k0, half), :] = k16
            span = DSA_GROUP * PACK
            for g in range(half // span):
                m = k16[span * g:span * g + PACK]
                for j in range(1, DSA_GROUP):
                    m = _max16(m, k16[span * g + PACK * j:span * g + PACK * (j + 1)])
                gms.append(m)
        gm_ref[pl.ds(pl.multiple_of(b * DSA_GB, DSA_GB), DSA_GB), :] = jnp.concatenate(gms, axis=0)
        return c

    nfull = q0 // DSA_KB
    lax.fori_loop(0, nfull, functools.partial(p1, masked=False), 0)
    lax.fori_loop(nfull, nb, functools.partial(p1, masked=True), 0)

    ngc = (nb * DSA_GB + DSA_GCHUNK - 1) // DSA_GCHUNK

    def gm_bit(it, prefix):
        cand = prefix | jnp.left_shift(jnp.int32(1), 15 - it)
        cnt = _count_ge16(gm_ref, ngc, DSA_GCHUNK, cand + I16_MIN, TQ)
        return jnp.where(cnt >= kf, cand, prefix)

    lo_h = lax.fori_loop(0, 16, gm_bit, jnp.zeros((1, TQ), I32)) + I16_MIN

    def gm_max(c, m):
        r0 = pl.multiple_of(c * DSA_GCHUNK, DSA_GCHUNK)
        for j in range(DSA_GCHUNK // PACK):
            m = _max16(m, gm_ref[pl.ds(r0 + PACK * j, PACK), :])
        return m

    gmax = lax.fori_loop(0, ngc, gm_max, jnp.full((PACK, TQ), I16_MIN, jnp.int16))
    hi_h = jnp.max(gmax.astype(I32), axis=0, keepdims=True) + 1

    def halve(lo_0, hi_0, want, first_probe, n_free):
        def step(it, lo, hi, chi):
            probe = jnp.where(it < 1, first_probe, INT_MIN)
            mid = jnp.where((lo < probe) & (probe < hi), probe, _avg_floor(lo, hi))
            c = _count_ge16(k16_ref, nb, DSA_KB, mid, TQ)
            ge = c >= want
            ex = jnp.logical_and(c == want, mid != lo)
            lo2 = jnp.where(ge, mid, lo)
            hi2 = jnp.where(ex, mid + 1, jnp.where(ge, hi, mid))
            chi2 = jnp.where(ex, EXACT, jnp.where(ge, chi, c))
            return lo2, hi2, chi2

        def active(lo, hi):
            return jnp.max(jnp.where(_avg_floor(lo, hi) != lo, 1.0, 0.0))

        def w_cond(st):
            return jnp.logical_and(st[0] < DSA_MAX_STEPS, st[4] > 0.0)

        def w_body(st):
            lo2, hi2, chi2 = step(st[0], st[1], st[2], st[3])
            return st[0] + 1, lo2, hi2, chi2, active(lo2, hi2)

        lo, hi, chi = lax.fori_loop(0, n_free, lambda it, st: step(it, *st),
                                    (lo_0, hi_0, jnp.zeros((1, TQ), F32)))
        _, lo, _, chi, _ = lax.while_loop(w_cond, w_body, (jnp.int32(n_free), lo, hi, chi, active(lo, hi)))
        return lo, chi

    tau_h, chi_h = halve(lo_h, hi_h, kf, INT_MIN, DSA_FREE_STEPS)
    exact_h = chi_h == EXACT
    want_l = kf - chi_h

    def build(b, c):
        k0 = pl.multiple_of(b * DSA_KB, DSA_KB)
        key = key_ref[pl.ds(k0, DSA_KB), :]
        low = jnp.where((key >> 16) == tau_h, (key & 0xFFFF) + I16_MIN, I16_MIN)
        k16_ref[pl.ds(k0, DSA_KB), :] = low.astype(jnp.int16)
        return c

    lax.fori_loop(0, nb, build, 0)
    tau_l, chi_l = halve(jnp.where(exact_h, 0, I16_MIN), jnp.where(exact_h, 1, -I16_MIN), want_l, I16_MIN + 1,
                         DSA_FREE_STEPS)
    tau = jnp.where(exact_h, tau_h << 16, (tau_h << 16) + (tau_l - I16_MIN))
    r = jnp.where(tau == INT_MIN, 0.0,
                  jnp.where(jnp.logical_or(exact_h, chi_l == EXACT), TIE_ALL, want_l - chi_l))

    AB = DSA_AB
    rowi = lax.broadcasted_iota(I32, (AB, AB), 0)
    coli = lax.broadcasted_iota(I32, (AB, AB), 1)
    tril = jnp.where(coli <= rowi, 1.0, 0.0).astype(BF16)
    aqT = aqT_ref[...]
    hrow = lax.broadcasted_iota(I32, (LANES, TQ), 0) // HEAD_DIM
    q_pad = []
    for h in range(A_HEADS):
        pair = aqT[LANES * (h // 2):LANES * (h // 2 + 1)]
        q_pad.append(jnp.where(hrow == (h % 2), pair, jnp.zeros_like(pair)))
    acc_ref[...] = jnp.zeros_like(acc_ref)

    nsub = 2 * DSA_KB // AB

    def p3(b, carry, speculate):
        cnt, ms = carry
        k0 = pl.multiple_of(b * 2 * DSA_KB, 2 * DSA_KB)
        bias = []
        for s2 in range(nsub):
            key = key_ref[pl.ds(k0 + s2 * AB, AB), :]
            eq = key == tau
            pref = jnp.dot(tril, jnp.where(eq, 1.0, 0.0).astype(BF16), preferred_element_type=F32) + cnt
            sel = jnp.where(key > tau, 0.0, jnp.where(eq, pref, 1e9)) <= r
            bias.append(jnp.where(sel, 0.0, NEG_BIG))
            cnt = pref[AB - 1:AB, :]

        def logits(h):
            return [jnp.dot(ak_ref[pl.ds(k0 + s2 * AB, AB), pl.ds(LANES * (h // 2), LANES)], q_pad[h],
                            preferred_element_type=F32) + bias[s2] for s2 in range(nsub)]

        def block_max(lm):
            return functools.reduce(jnp.maximum, [jnp.max(t, axis=0, keepdims=True) for t in lm])

        def weights(lm, m):
            return jnp.concatenate([jnp.exp2(t - m).astype(BF16) for t in lm], axis=0)

        def values(h):
            return avT_ref[pl.ds(V_EXT * h, V_EXT), pl.ds(k0, 2 * DSA_KB)]

        def exact(ms2, lms=None):
            for h in range(A_HEADS):
                hs = pl.ds(V_EXT * h, V_EXT)
                lm = logits(h) if lms is None else lms[h]
                pv = jnp.dot(values(h), weights(lm, ms2[h]), preferred_element_type=F32)
                acc_ref[hs, :] = acc_ref[hs, :] * jnp.exp2(ms[h] - ms2[h]) + pv

        lms = [logits(h) for h in range(A_HEADS)]
        if not speculate:
            ms2 = [jnp.maximum(ms[h], block_max(lms[h])) for h in range(A_HEADS)]
            exact(ms2, lms)
            return cnt, tuple(ms2)

        ps, ms2 = [], []
        for h in range(A_HEADS):
            ps.append(weights(lms[h], ms[h]))
            ms2.append(jnp.maximum(ms[h], block_max(lms[h])))
        pvs = [jnp.dot(values(h), ps[h], preferred_element_type=F32) for h in range(A_HEADS)]
        rise = jnp.max(functools.reduce(jnp.maximum, [ms2[h] - ms[h] for h in range(A_HEADS)]))

        @pl.when(rise <= EXP2_HEADROOM)
        def _():
            for h in range(A_HEADS):
                hs = pl.ds(V_EXT * h, V_EXT)
                acc_ref[hs, :] = (acc_ref[hs, :] + pvs[h]) * jnp.exp2(ms[h] - ms2[h])

        @pl.when(rise > EXP2_HEADROOM)
        def _():
            exact(ms2)

        return cnt, tuple(ms2)

    init = (jnp.zeros((1, TQ), F32), tuple(jnp.full((1, TQ), M_FLOOR, F32) for _ in range(A_HEADS)))
    first = p3(0, init, speculate=False)
    lax.fori_loop(1, (nb + 1) // 2, functools.partial(p3, speculate=True), first)
    outT = jnp.concatenate([acc_ref[pl.ds(V_EXT * h, HEAD_DIM), :] / acc_ref[pl.ds(V_EXT * h + HEAD_DIM, 1), :]
                            for h in range(A_HEADS)], axis=0)
    o_ref[...] = outT.T.astype(BF16)


def _dsa(aqT, iqT, iwT, ik, ak, avT, S, tq):
    top_k = min(TOPK_MAX, S // 4)
    colT = lambda n: pl.BlockSpec((n, tq), lambda i: (0, i))
    return pl.pallas_call(
        functools.partial(_dsa_kernel, top_k=top_k),
        grid=(S // tq,),
        in_specs=[colT(D_A), colT(D_A), colT(SUBLANES),
                  _resident((S, LANES), lambda i: (0, 0)),
                  _resident((S, D_A), lambda i: (0, 0)),
                  _resident((A_HEADS * V_EXT, S), lambda i: (0, 0))],
        out_specs=pl.BlockSpec((tq, D_A), lambda i: (i, 0)),
        out_shape=jax.ShapeDtypeStruct((S, D_A), BF16),
        scratch_shapes=[pltpu.VMEM((S, tq), I32),
                        pltpu.VMEM((S, tq), jnp.int16),
                        pltpu.VMEM((S // DSA_GROUP + DSA_GCHUNK, tq), jnp.int16),
                        pltpu.VMEM((A_HEADS * V_EXT, tq), F32)],
        compiler_params=_params(("arbitrary",)),
        name="dsa",
    )(aqT, iqT, iwT, ik, ak, avT)


FF_CHUNKS = ((0, 768), (768, 768), (1536, 768), (2304, 512))


def _rms(v, w):
    return v * lax.rsqrt(jnp.mean(v * v, axis=-1, keepdims=True) + EPS) * w


def _outffn_kernel(x_ref, hm_ref, ha_ref, hs_ref, nw_ref, wo_ref, wg_ref, wu_ref, wd_ref, o_ref):
    mix = (jnp.dot(hm_ref[...], wo_ref[0:D_M, :], preferred_element_type=F32)
           + jnp.dot(ha_ref[...], wo_ref[D_M:D_M + D_A, :], preferred_element_type=F32)
           + jnp.dot(hs_ref[...], wo_ref[D_M + D_A:, :], preferred_element_type=F32))
    x1 = x_ref[...] + _rms(mix, nw_ref[1:2, :])
    h2 = _rms(x1, nw_ref[2:3, :]).astype(BF16)
    ff = jnp.zeros_like(x1)
    for c0, n in FF_CHUNKS:
        g = jnp.dot(h2, wg_ref[:, c0:c0 + n], preferred_element_type=F32)
        u = jnp.dot(h2, wu_ref[:, c0:c0 + n], preferred_element_type=F32)
        act = (g * jax.nn.sigmoid(g) * u).astype(BF16)
        ff = ff + jnp.dot(act, wd_ref[c0:c0 + n, :], preferred_element_type=F32)
    o_ref[...] = x1 + _rms(ff, nw_ref[3:4, :])


def _outffn(x2, hm, ha, hs, nw4, wo, wg, wu, wd, S, tm):
    row = lambda n: pl.BlockSpec((tm, n), lambda i: (i, 0))
    return pl.pallas_call(
        _outffn_kernel,
        grid=(S // tm,),
        in_specs=[row(D_MODEL), row(D_M), row(D_A), row(D_S),
                  pl.BlockSpec((4, D_MODEL), lambda i: (0, 0)),
                  _resident((D_MODEL, D_MODEL), lambda i: (0, 0)),
                  _resident((D_MODEL, D_FF), lambda i: (0, 0)),
                  _resident((D_MODEL, D_FF), lambda i: (0, 0)),
                  _resident((D_FF, D_MODEL), lambda i: (0, 0))],
        out_specs=row(D_MODEL),
        out_shape=jax.ShapeDtypeStruct((S, D_MODEL), F32),
        compiler_params=_params(("arbitrary",)),
        name="outffn",
    )(x2, hm, ha, hs, nw4, wo, wg, wu, wd)


def _plan(S):
    assert S % (2 * DSA_KB) == 0
    return dict(tm=min(S, 512), lm=min(S, 256), ls=min(S, 256), tq=min(S, 256))


def _pad_lanes(v):
    return jnp.pad(v, [(0, 0)] * (v.ndim - 1) + [(0, LANES - v.shape[-1])])


def kernel(x, positions, norm_w, w_in, mlstm_gate_bias, mlstm_norm_w, conv_w, conv_b, dt_bias, a_log,
           d_skip, ssd_norm_w, w_out, w_gate, w_up, w_down):
    B, S, D = x.shape
    assert B == 1 and D == D_MODEL
    depth = w_in.shape[0]
    plan = _plan(S)

    o = np.cumsum([0, D_M, D_M, D_M, D_M, M_HEADS, M_HEADS, D_A, D_A, D_A, IDX_HEADS * IDX_DIM, IDX_DIM,
                   IDX_HEADS, D_S, CONV_DIM, S_HEADS])
    wp = jnp.concatenate([
        w_in[:, :, o[0]:o[4]], w_in[:, :, o[6]:o[10]], w_in[:, :, o[12]:o[13]], w_in[:, :, o[13]:o[14]],
        _pad_lanes(w_in[:, :, o[4]:o[6]]), _pad_lanes(w_in[:, :, o[10]:o[12]]), _pad_lanes(w_in[:, :, o[14]:o[15]]),
    ], axis=-1).astype(BF16)
    wo = w_out.astype(BF16)
    wg = w_gate.astype(BF16)
    wu = w_up.astype(BF16)
    wd = w_down.astype(BF16)
    gb = _pad_lanes(mlstm_gate_bias[:, None, :])
    dtb = _pad_lanes(dt_bias[:, None, :])
    alog = _pad_lanes(a_log[:, None, :])
    dsk = jnp.repeat(d_skip, HEAD_DIM, axis=-1)[:, None, :]

    cf, sa, sb = _rope_tables(positions.astype(I32), S)
    x2 = x.reshape(S, D)
    for l in range(depth):
        om, omg, os_, odt, aqT, iqT, iwT, ik, ak, avT = _inproj(
            x2, norm_w[l, 0:1], wp[l], cf, sa, sb, S, plan["tm"])
        hm = _mlstm(om, omg, gb[l], mlstm_norm_w[l][None, :], S, plan["lm"])
        hs = _ssd(os_, odt, conv_w[l], conv_b[l][None, :], dtb[l], alog[l], dsk[l], ssd_norm_w[l][None, :],
                  S, plan["ls"])
        ha = _dsa(aqT, iqT, iwT, ik, ak, avT, S, plan["tq"])
        x2 = _outffn(x2, hm, ha, hs, norm_w[l], wo[l], wg[l], wu[l], wd[l], S, plan["tm"])
    return x2.reshape(B, S, D)
```

```python
import functools

import numpy as np
import jax
import jax.numpy as jnp
from jax import lax
from jax.experimental import pallas as pl
from jax.experimental.pallas import tpu as pltpu

F32 = jnp.float32
BF16 = jnp.bfloat16
I32 = jnp.int32

D_MODEL = 1024
HEAD_DIM = 64
D_M = 256
M_HEADS = 4
D_A = 256
A_HEADS = 4
IDX_HEADS = 4
IDX_DIM = 64
TOPK_MAX = 256
D_S = 512
S_HEADS = 8
S_GROUPS = 2
S_STATE = 128
S_CONV = 4
CONV_DIM = D_S + 2 * S_GROUPS * S_STATE
ROPE_THETA = 500000.0
ROPE_DIM = HEAD_DIM // 4
ROPE_HALF = ROPE_DIM // 2
D_FF = 2816
EPS = 1e-6

LANES = 128
SUBLANES = 8
PACK = 16
VMEM_LIMIT = 60 * 1024 * 1024

C_M = 0
C_A = C_M + 4 * D_M
C_Z = C_A + 4 * D_A
C_XBC = C_Z + D_S
C_MG = C_XBC + CONV_DIM
C_IK = C_MG + LANES
C_DT = C_IK + LANES
N_P = C_DT + LANES

INT_MIN = -2 ** 31
NEG_BIG = -1e30
M_FLOOR = -1e29
LOG2E = 1.4426950408889634
EXP2_HEADROOM = 64.0

DSA_KB = 512
DSA_AB = 256
DSA_GROUP = 16
DSA_GB = DSA_KB // DSA_GROUP
DSA_GCHUNK = 128
V_EXT = HEAD_DIM + PACK
DSA_MAX_STEPS = 24
DSA_FREE_STEPS = 7
TIE_ALL = 1e6
EXACT = -1e9
I16_MIN = -2 ** 15


def _params(sem):
    return pltpu.CompilerParams(dimension_semantics=sem, vmem_limit_bytes=VMEM_LIMIT)


def _resident(shape, index_map):
    return pl.BlockSpec(shape, index_map, pipeline_mode=pl.Buffered(1))


def _split3(x):
    h = x.astype(BF16)
    r = x - h.astype(F32)
    m = r.astype(BF16)
    lo = (r - m.astype(F32)).astype(BF16)
    return h, m, lo


def _cumsum_cols(tril, x):
    return sum(jnp.dot(tril, t, preferred_element_type=F32) for t in _split3(x))


def _cumsum_rows(x, triu):
    return sum(jnp.dot(t, triu, preferred_element_type=F32) for t in _split3(x))


def _tri(L):
    row = lax.broadcasted_iota(I32, (L, L), 0)
    col = lax.broadcasted_iota(I32, (L, L), 1)
    causal = col <= row
    tril = jnp.where(causal, 1.0, 0.0).astype(BF16)
    triu = jnp.where(row <= col, 1.0, 0.0).astype(BF16)
    return causal, tril, triu


def _rope_tables_kernel(pos_ref, inv_ref, cf_ref, sa_ref, sb_ref):
    ang = pos_ref[...].astype(F32) * inv_ref[...]
    c = jnp.cos(ang)
    s = jnp.sin(ang)
    j = lax.broadcasted_iota(I32, ang.shape, 1) & (HEAD_DIM - 1)
    cf_ref[...] = jnp.where(j < ROPE_DIM, c, 1.0)
    sa_ref[...] = jnp.where(j < ROPE_HALF, -s, 0.0)
    sb_ref[...] = jnp.where(j < ROPE_HALF, 0.0, jnp.where(j < ROPE_DIM, s, 0.0))


def _rope_tables(positions, S):
    tb = min(S, 1024)
    inv = np.power(np.float32(ROPE_THETA), -np.arange(ROPE_HALF, dtype=np.float32) / np.float32(ROPE_HALF))
    lane = np.arange(LANES) % HEAD_DIM
    inv_lanes = np.where(lane < ROPE_DIM, inv[lane % ROPE_HALF], np.float32(0)).astype(np.float32)[None, :]
    tab = jax.ShapeDtypeStruct((S, LANES), F32)
    row = pl.BlockSpec((tb, LANES), lambda i: (i, 0))
    return pl.pallas_call(
        _rope_tables_kernel,
        grid=(S // tb,),
        in_specs=[pl.BlockSpec((tb, 1), lambda i: (i, 0)), pl.BlockSpec((1, LANES), lambda i: (0, 0))],
        out_specs=[row, row, row],
        out_shape=[tab, tab, tab],
        compiler_params=_params(("arbitrary",)),
        name="rope_tables",
    )(positions.reshape(S, 1), jnp.asarray(inv_lanes))


def _inproj_kernel(x_ref, nw_ref, w_ref, cf_ref, sa_ref, sb_ref,
                   om_ref, omg_ref, os_ref, odt_ref,
                   aqT_ref, iqT_ref, iwT_ref, ik_ref, ak_ref, avT_ref):
    x = x_ref[...]
    ms = jnp.mean(x * x, axis=-1, keepdims=True)
    h = (x * lax.rsqrt(ms + EPS) * nw_ref[...]).astype(BF16)

    def proj(c0, n):
        return jnp.dot(h, w_ref[:, c0:c0 + n], preferred_element_type=F32)

    om_ref[...] = proj(C_M, 4 * D_M)
    os_ref[...] = proj(C_Z, D_S + CONV_DIM)
    omg_ref[...] = proj(C_MG, LANES)
    odt_ref[...] = proj(C_DT, LANES)

    cf = cf_ref[...]
    sa = sa_ref[...]
    sb = sb_ref[...]

    def rope(c, cf=cf, sa=sa, sb=sb):
        return c * cf + pltpu.roll(c, LANES - ROPE_HALF, 1) * sa + pltpu.roll(c, ROPE_HALF, 1) * sb

    def rope2(a2):
        return jnp.concatenate([rope(a2[:, :LANES]), rope(a2[:, LANES:])], axis=1)

    a = proj(C_A, 4 * D_A)
    aqT_ref[...] = (rope2(a[:, 0:D_A]) * (HEAD_DIM ** -0.5 * LOG2E)).T.astype(BF16)
    ak_ref[...] = rope2(a[:, D_A:2 * D_A]).astype(BF16)
    vT = a[:, 2 * D_A:3 * D_A].T.astype(BF16)
    ones = jnp.ones((V_EXT - HEAD_DIM, vT.shape[1]), BF16)
    avT_ref[...] = jnp.concatenate(
        [t for h in range(A_HEADS) for t in (vT[HEAD_DIM * h:HEAD_DIM * (h + 1)], ones)], axis=0)
    iqT_ref[...] = rope2(a[:, 3 * D_A:4 * D_A]).T.astype(BF16)

    ikw = proj(C_IK, LANES)
    is_ik = lax.broadcasted_iota(I32, ikw.shape, 1) < IDX_DIM
    ikr = rope(ikw, jnp.where(is_ik, cf, 1.0), jnp.where(is_ik, sa, 0.0), jnp.where(is_ik, sb, 0.0))
    ik_ref[...] = jnp.where(is_ik, ikr, 0.0).astype(BF16)
    iwT_ref[...] = ikw.T[IDX_DIM:IDX_DIM + SUBLANES, :]


def _inproj(x2, nw, wp, cf, sa, sb, S, tm):
    row = lambda n: pl.BlockSpec((tm, n), lambda i: (i, 0))
    colT = lambda n: pl.BlockSpec((n, tm), lambda i: (0, i))
    f = lambda n, dt=F32: jax.ShapeDtypeStruct((S, n), dt)
    fT = lambda n, dt=BF16: jax.ShapeDtypeStruct((n, S), dt)
    return pl.pallas_call(
        _inproj_kernel,
        grid=(S // tm,),
        in_specs=[row(D_MODEL), pl.BlockSpec((1, D_MODEL), lambda i: (0, 0)),
                  _resident((D_MODEL, N_P), lambda i: (0, 0)),
                  row(LANES), row(LANES), row(LANES)],
        out_specs=[row(4 * D_M), row(LANES), row(D_S + CONV_DIM), row(LANES),
                   colT(D_A), colT(D_A), colT(SUBLANES), row(LANES), row(D_A), colT(A_HEADS * V_EXT)],
        out_shape=[f(4 * D_M), f(LANES), f(D_S + CONV_DIM), f(LANES),
                   fT(D_A), fT(D_A), fT(SUBLANES, F32), f(LANES, BF16), f(D_A, BF16), fT(A_HEADS * V_EXT)],
        compiler_params=_params(("arbitrary",)),
        name="inproj",
    )(x2, nw, wp, cf, sa, sb)


def _mlstm_kernel(om_ref, omg_ref, gb_ref, nw_ref, o_ref, C_ref, n_ref, m_ref):
    L = om_ref.shape[0]

    @pl.when(pl.program_id(0) == 0)
    def _():
        C_ref[...] = jnp.zeros_like(C_ref)
        n_ref[...] = jnp.zeros_like(n_ref)
        m_ref[...] = jnp.zeros_like(m_ref)

    causal, tril, triu = _tri(L)
    G = omg_ref[...] + gb_ref[...]
    LF = jnp.minimum(G, 0.0) - jnp.log1p(jnp.exp(-jnp.abs(G)))
    Bc = _cumsum_cols(tril, LF)
    GT = G.T
    BrT = _cumsum_rows(LF.T, triu)
    nw = nw_ref[...]
    heads = range(M_HEADS)

    def head(c0, h):
        blk = om_ref[:, pl.ds(c0 + LANES * (h // 2), LANES)]
        return blk[:, HEAD_DIM * (h % 2):HEAD_DIM * (h % 2 + 1)]

    qf = [head(0, h) * (HEAD_DIM ** -0.5) for h in heads]
    kf = [head(D_M, h) for h in heads]
    q = [t.astype(BF16) for t in qf]
    k = [t.astype(BF16) for t in kf]
    v = [head(2 * D_M, h).astype(BF16) for h in heads]
    C_prev = [C_ref[h] for h in heads]
    n_prev = [n_ref[h:h + 1, :] for h in heads]
    m_prev = [m_ref[h:h + 1, 0:1] for h in heads]
    qk = [lax.dot_general(q[h], k[h], (((1,), (1,)), ((), ())), preferred_element_type=F32) for h in heads]
    qC = [jnp.dot(q[h], C_prev[h].astype(BF16), preferred_element_type=F32) for h in heads]
    s, scale, m_t, kw, decay = [], [], [], [], []
    for h in heads:
        f_l = M_HEADS + h
        b_col = Bc[:, f_l:f_l + 1]
        b_row = BrT[f_l:f_l + 1, :]
        i_col = G[:, h:h + 1]
        i_row = GT[h:h + 1, :]
        logd = jnp.where(causal, b_col + (i_row - b_row), -jnp.inf)
        m_inter = b_col + m_prev[h]
        mt = jnp.maximum(m_inter, jnp.max(logd, axis=-1, keepdims=True))
        s.append(qk[h] * jnp.exp(logd - mt))
        scale.append(jnp.exp(m_inter - mt))
        m_t.append(mt)
        b_last = Bc[L - 1:L, f_l:f_l + 1]
        m_new = jnp.maximum(b_last + m_prev[h], jnp.max(b_last - b_row + i_row, axis=-1, keepdims=True))
        kw.append(kf[h] * jnp.exp(b_last - b_col + i_col - m_new))
        decay.append(jnp.exp(b_last + m_prev[h] - m_new))
        m_ref[h:h + 1, :] = jnp.broadcast_to(m_new, (1, LANES))
    sv = [jnp.dot(s[h].astype(BF16), v[h], preferred_element_type=F32) for h in heads]
    kv = [lax.dot_general(kw[h].astype(BF16), v[h], (((0,), (0,)), ((), ())), preferred_element_type=F32)
          for h in heads]
    outs = []
    for h in heads:
        num = sv[h] + scale[h] * qC[h]
        den = (jnp.sum(s[h], axis=-1, keepdims=True)
               + scale[h] * jnp.sum(qf[h] * n_prev[h], axis=-1, keepdims=True))
        hh = num / jnp.maximum(jnp.abs(den), jnp.exp(-m_t[h]))
        C_ref[h] = decay[h] * C_prev[h] + kv[h]
        n_ref[h:h + 1, :] = decay[h] * n_prev[h] + jnp.sum(kw[h], axis=0, keepdims=True)
        y = hh * lax.rsqrt(jnp.mean(hh * hh, axis=-1, keepdims=True) + EPS)
        outs.append(jax.nn.sigmoid(head(3 * D_M, h)) * y)
    o_ref[...] = (jnp.concatenate(outs, axis=1) * nw).astype(BF16)


def _mlstm(om, omg, gb, nw, S, L):
    return pl.pallas_call(
        _mlstm_kernel,
        grid=(S // L,),
        in_specs=[pl.BlockSpec((L, 4 * D_M), lambda i: (i, 0)), pl.BlockSpec((L, LANES), lambda i: (i, 0)),
                  pl.BlockSpec((1, LANES), lambda i: (0, 0)), pl.BlockSpec((1, D_M), lambda i: (0, 0))],
        out_specs=pl.BlockSpec((L, D_M), lambda i: (i, 0)),
        out_shape=jax.ShapeDtypeStruct((S, D_M), BF16),
        scratch_shapes=[pltpu.VMEM((M_HEADS, HEAD_DIM, HEAD_DIM), F32),
                        pltpu.VMEM((SUBLANES, HEAD_DIM), F32),
                        pltpu.VMEM((SUBLANES, LANES), F32)],
        compiler_params=_params(("arbitrary",)),
        name="mlstm",
    )(om, omg, gb, nw)


def _ssd_kernel(os_ref, odt_ref, cw_ref, cb_ref, dtb_ref, alog_ref, dsk_ref, nw_ref, o_ref,
                carry_ref, st_ref):
    L = os_ref.shape[0]

    @pl.when(pl.program_id(0) == 0)
    def _():
        carry_ref[...] = jnp.zeros_like(carry_ref)
        st_ref[...] = jnp.zeros_like(st_ref)

    causal, tril, triu = _tri(L)
    z = os_ref[:, 0:D_S]
    raw = os_ref[:, D_S:D_S + CONV_DIM]
    ext = jnp.concatenate([carry_ref[...], raw], axis=0)
    cw = cw_ref[...]
    xbc = cb_ref[...] + cw[S_CONV - 1:S_CONV, :] * raw
    for j in range(S_CONV - 1):
        off = SUBLANES - (S_CONV - 1) + j
        xbc = xbc + cw[j:j + 1, :] * ext[off:off + L, :]
    carry_ref[...] = raw[L - SUBLANES:L, :]
    xbc = xbc * jax.nn.sigmoid(xbc)

    dtr = odt_ref[...] + dtb_ref[...]
    DT = jnp.maximum(dtr, 0.0) + jnp.log1p(jnp.exp(-jnp.abs(dtr)))
    dA = DT * (-jnp.exp(alog_ref[...]))
    Ac = _cumsum_cols(tril, dA)
    ArT = _cumsum_rows(dA.T, triu)
    DTT = DT.T
    hpg = S_HEADS // S_GROUPS
    ys = []
    for g in range(S_GROUPS):
        Bg = xbc[:, D_S + S_STATE * g:D_S + S_STATE * (g + 1)]
        Cg = xbc[:, D_S + S_GROUPS * S_STATE + S_STATE * g:D_S + S_GROUPS * S_STATE + S_STATE * (g + 1)]
        Cb = Cg.astype(BF16)
        CB = lax.dot_general(Cb, Bg.astype(BF16), (((1,), (1,)), ((), ())), preferred_element_type=F32)
        BgT = Bg.T
        for hh in range(hpg):
            h = g * hpg + hh
            xpair = xbc[:, LANES * (h // 2):LANES * (h // 2 + 1)]
            xh = xpair[:, HEAD_DIM * (h % 2):HEAD_DIM * (h % 2 + 1)]
            ac_col = Ac[:, h:h + 1]
            ac_row = ArT[h:h + 1, :]
            dec = jnp.exp(jnp.where(causal, ac_col - ac_row, -jnp.inf))
            sc = (CB * dec).astype(BF16)
            xdt = (xh * DT[:, h:h + 1]).astype(BF16)
            st = st_ref[h]
            y = (jnp.dot(sc, xdt, preferred_element_type=F32)
                 + jnp.dot(Cb, st.astype(BF16), preferred_element_type=F32) * jnp.exp(ac_col))
            a_last = Ac[L - 1:L, h:h + 1]
            w_row = jnp.exp(a_last - ac_row) * DTT[h:h + 1, :]
            st_ref[h] = st * jnp.exp(a_last) + jnp.dot((BgT * w_row).astype(BF16), xh.astype(BF16),
                                                       preferred_element_type=F32)
            ys.append(y)
    Y = jnp.concatenate(ys, axis=1) + dsk_ref[...] * xbc[:, 0:D_S]
    gated = Y * (z * jax.nn.sigmoid(z))
    gw = D_S // S_GROUPS
    outs = []
    for g in range(S_GROUPS):
        gg = gated[:, gw * g:gw * (g + 1)]
        outs.append(gg * lax.rsqrt(jnp.mean(gg * gg, axis=-1, keepdims=True) + EPS))
    o_ref[...] = (jnp.concatenate(outs, axis=1) * nw_ref[...]).astype(BF16)


def _ssd(os_, odt, cw, cb, dtb, alog, dsk, nw, S, L):
    full = lambda r, c: pl.BlockSpec((r, c), lambda i: (0, 0))
    return pl.pallas_call(
        _ssd_kernel,
        grid=(S // L,),
        in_specs=[pl.BlockSpec((L, D_S + CONV_DIM), lambda i: (i, 0)), pl.BlockSpec((L, LANES), lambda i: (i, 0)),
                  full(S_CONV, CONV_DIM), full(1, CONV_DIM), full(1, LANES), full(1, LANES),
                  full(1, D_S), full(1, D_S)],
        out_specs=pl.BlockSpec((L, D_S), lambda i: (i, 0)),
        out_shape=jax.ShapeDtypeStruct((S, D_S), BF16),
        scratch_shapes=[pltpu.VMEM((SUBLANES, CONV_DIM), F32),
                        pltpu.VMEM((S_HEADS, S_STATE, HEAD_DIM), F32)],
        compiler_params=_params(("arbitrary",)),
        name="ssd",
    )(os_, odt, cw, cb, dtb, alog, dsk, nw)


def _max16(a, b):
    return jnp.where(a >= b, a, b)


def _count_ge16(ref, nblk, rb, thr_row, tq):
    thr = jnp.broadcast_to(thr_row.astype(jnp.int16), (PACK, tq))
    n_acc = 4
    one = jnp.ones((PACK, tq), jnp.int16)

    def body(b, accs):
        slab = ref[pl.ds(pl.multiple_of(b * rb, rb), rb), :]
        accs = list(accs)
        for j in range(rb // PACK):
            blk = slab[PACK * j:PACK * (j + 1)]
            a = accs[j % n_acc]
            accs[j % n_acc] = jnp.where(blk >= thr, a + one, a)
        return tuple(accs)

    accs = lax.fori_loop(0, nblk, body, tuple(jnp.zeros((PACK, tq), jnp.int16) for _ in range(n_acc)))
    tot = (accs[0].astype(I32) + accs[1].astype(I32)) + (accs[2].astype(I32) + accs[3].astype(I32))
    return jnp.sum(tot.astype(F32), axis=0, keepdims=True)


def _avg_floor(lo, hi):
    return (lo >> 1) + (hi >> 1) + (lo & hi & 1)


def _dsa_kernel(aqT_ref, iqT_ref, iwT_ref, ik_ref, ak_ref, avT_ref, o_ref, key_ref, k16_ref, gm_ref, acc_ref,
                *, top_k):
    TQ = aqT_ref.shape[1]
    q0 = pl.program_id(0) * TQ
    nb = (q0 + TQ + DSA_KB - 1) // DSA_KB
    qpos = q0 + lax.broadcasted_iota(I32, (1, TQ), 1)
    kf = float(top_k)
    w_scale = (IDX_HEADS ** -0.5) * (IDX_DIM ** -0.5)

    iqT = iqT_ref[...]
    zpad = jnp.zeros((LANES - IDX_DIM, TQ), BF16)
    iq_pad = [jnp.concatenate([iqT[IDX_DIM * h:IDX_DIM * (h + 1)], zpad], axis=0) for h in range(IDX_HEADS)]
    w_rows = [iwT_ref[h:h + 1, :] * w_scale for h in range(IDX_HEADS)]

    gm_ref[pl.ds(pl.multiple_of(nb * DSA_GB, DSA_GB), DSA_GCHUNK - DSA_GB), :] = jnp.full(
        (DSA_GCHUNK - DSA_GB, TQ), I16_MIN, jnp.int16)

    @pl.when(nb % 2 == 1)
    def _():
        key_ref[pl.ds(pl.multiple_of(nb * DSA_KB, DSA_KB), DSA_KB), :] = jnp.full((DSA_KB, TQ), INT_MIN, I32)

    half = DSA_KB // 2

    def p1(step, c, masked, per_step):
        n_half = 2 * per_step
        base = step * (per_step * DSA_KB)
        k0s = [pl.multiple_of(base + s2 * half, half) for s2 in range(n_half)]
        dots = [[jnp.dot(ik_ref[pl.ds(k0s[s2], half), :], iq_pad[h], preferred_element_type=F32)
                 for h in range(IDX_HEADS)] for s2 in range(n_half)]
        gms = []
        for s2 in range(n_half):
            k0 = k0s[s2]
            sc = jnp.zeros((half, TQ), F32)
            for h in range(IDX_HEADS):
                sc = sc + w_rows[h] * jnp.maximum(dots[s2][h], 0.0)
            bits = lax.bitcast_convert_type(sc, I32)
            key = jnp.where(bits < 0, INT_MIN - bits, bits)
            if masked:
                kpos = k0 + lax.broadcasted_iota(I32, (half, 1), 0)
                key = jnp.where(kpos <= qpos, key, INT_MIN)
            key_ref[pl.ds(k0, half), :] = key
            k16 = (key >> 16).astype(jnp.int16)
            k16_ref[pl.ds(k0, half), :] = k16
            span = DSA_GROUP * PACK
            for g in range(half // span):
                m = k16[span * g:span * g + PACK]
                for j in range(1, DSA_GROUP):
                    m = _max16(m, k16[span * g + PACK * j:span * g + PACK * (j + 1)])
                gms.append(m)
        rows = per_step * DSA_GB
        gm_ref[pl.ds(pl.multiple_of(step * rows, rows), rows), :] = jnp.concatenate(gms, axis=0)
        return c

    nfull = q0 // DSA_KB
    lax.fori_loop(0, nfull // 2, functools.partial(p1, masked=False, per_step=2), 0)
    lax.fori_loop(2 * (nfull // 2), nfull, functools.partial(p1, masked=False, per_step=1), 0)
    lax.fori_loop(nfull, nb, functools.partial(p1, masked=True, per_step=1), 0)

    ngc = (nb * DSA_GB + DSA_GCHUNK - 1) // DSA_GCHUNK

    def gm_bit(it, prefix):
        cand = prefix | jnp.left_shift(jnp.int32(1), 15 - it)
        cnt = _count_ge16(gm_ref, ngc, DSA_GCHUNK, cand + I16_MIN, TQ)
        return jnp.where(cnt >= kf, cand, prefix)

    lo_h = lax.fori_loop(0, 16, gm_bit, jnp.zeros((1, TQ), I32)) + I16_MIN

    def gm_max(c, m):
        r0 = pl.multiple_of(c * DSA_GCHUNK, DSA_GCHUNK)
        for j in range(DSA_GCHUNK // PACK):
            m = _max16(m, gm_ref[pl.ds(r0 + PACK * j, PACK), :])
        return m

    gmax = lax.fori_loop(0, ngc, gm_max, jnp.full((PACK, TQ), I16_MIN, jnp.int16))
    hi_h = jnp.max(gmax.astype(I32), axis=0, keepdims=True) + 1

    def halve(lo_0, hi_0, want, first_probe, n_free):
        def step(it, lo, hi, chi):
            probe = jnp.where(it < 1, first_probe, INT_MIN)
            mid = jnp.where((lo < probe) & (probe < hi), probe, _avg_floor(lo, hi))
            c = _count_ge16(k16_ref, nb, DSA_KB, mid, TQ)
            ge = c >= want
            ex = jnp.logical_and(c == want, mid != lo)
            lo2 = jnp.where(ge, mid, lo)
            hi2 = jnp.where(ex, mid + 1, jnp.where(ge, hi, mid))
            chi2 = jnp.where(ex, EXACT, jnp.where(ge, chi, c))
            return lo2, hi2, chi2

        def active(lo, hi):
            return jnp.max(jnp.where(_avg_floor(lo, hi) != lo, 1.0, 0.0))

        def w_cond(st):
            return jnp.logical_and(st[0] < DSA_MAX_STEPS, st[4] > 0.0)

        def w_body(st):
            lo2, hi2, chi2 = step(st[0], st[1], st[2], st[3])
            return st[0] + 1, lo2, hi2, chi2, active(lo2, hi2)

        lo, hi, chi = lax.fori_loop(0, n_free, lambda it, st: step(it, *st),
                                    (lo_0, hi_0, jnp.zeros((1, TQ), F32)))
        _, lo, _, chi, _ = lax.while_loop(w_cond, w_body, (jnp.int32(n_free), lo, hi, chi, active(lo, hi)))
        return lo, chi

    tau_h, chi_h = halve(lo_h, hi_h, kf, INT_MIN, DSA_FREE_STEPS)
    exact_h = chi_h == EXACT
    want_l = kf - chi_h

    def build(b, c):
        k0 = pl.multiple_of(b * DSA_KB, DSA_KB)
        key = key_ref[pl.ds(k0, DSA_KB), :]
        low = jnp.where((key >> 16) == tau_h, (key & 0xFFFF) + I16_MIN, I16_MIN)
        k16_ref[pl.ds(k0, DSA_KB), :] = low.astype(jnp.int16)
        return c

    lax.fori_loop(0, nb, build, 0)
    tau_l, chi_l = halve(jnp.where(exact_h, 0, I16_MIN), jnp.where(exact_h, 1, -I16_MIN), want_l, I16_MIN + 1,
                         DSA_FREE_STEPS)
    tau = jnp.where(exact_h, tau_h << 16, (tau_h << 16) + (tau_l - I16_MIN))
    r = jnp.where(tau == INT_MIN, 0.0,
                  jnp.where(jnp.logical_or(exact_h, chi_l == EXACT), TIE_ALL, want_l - chi_l))

    AB = DSA_AB
    rowi = lax.broadcasted_iota(I32, (AB, AB), 0)
    coli = lax.broadcasted_iota(I32, (AB, AB), 1)
    tril = jnp.where(coli <= rowi, 1.0, 0.0).astype(BF16)
    aqT = aqT_ref[...]
    hrow = lax.broadcasted_iota(I32, (LANES, TQ), 0) // HEAD_DIM
    q_pad = []
    for h in range(A_HEADS):
        pair = aqT[LANES * (h // 2):LANES * (h // 2 + 1)]
        q_pad.append(jnp.where(hrow == (h % 2), pair, jnp.zeros_like(pair)))
    acc_ref[...] = jnp.zeros_like(acc_ref)

    nsub = 2 * DSA_KB // AB

    def p3(b, carry, speculate):
        cnt, ms = carry
        k0 = pl.multiple_of(b * 2 * DSA_KB, 2 * DSA_KB)
        bias = []
        for s2 in range(nsub):
            key = key_ref[pl.ds(k0 + s2 * AB, AB), :]
            eq = key == tau
            pref = jnp.dot(tril, jnp.where(eq, 1.0, 0.0).astype(BF16), preferred_element_type=F32) + cnt
            sel = jnp.where(key > tau, 0.0, jnp.where(eq, pref, 1e9)) <= r
            bias.append(jnp.where(sel, 0.0, NEG_BIG))
            cnt = pref[AB - 1:AB, :]

        def logits(h):
            return [jnp.dot(ak_ref[pl.ds(k0 + s2 * AB, AB), pl.ds(LANES * (h // 2), LANES)], q_pad[h],
                            preferred_element_type=F32) + bias[s2] for s2 in range(nsub)]

        def block_max(lm):
            return functools.reduce(jnp.maximum, [jnp.max(t, axis=0, keepdims=True) for t in lm])

        def weights(lm, m):
            return jnp.concatenate([jnp.exp2(t - m).astype(BF16) for t in lm], axis=0)

        def values(h):
            return avT_ref[pl.ds(V_EXT * h, V_EXT), pl.ds(k0, 2 * DSA_KB)]

        def exact(ms2, lms=None):
            for h in range(A_HEADS):
                hs = pl.ds(V_EXT * h, V_EXT)
                lm = logits(h) if lms is None else lms[h]
                pv = jnp.dot(values(h), weights(lm, ms2[h]), preferred_element_type=F32)
                acc_ref[hs, :] = acc_ref[hs, :] * jnp.exp2(ms[h] - ms2[h]) + pv

        lms = [logits(h) for h in range(A_HEADS)]
        if not speculate:
            ms2 = [jnp.maximum(ms[h], block_max(lms[h])) for h in range(A_HEADS)]
            exact(ms2, lms)
            return cnt, tuple(ms2)

        ps, ms2 = [], []
        for h in range(A_HEADS):
            ps.append(weights(lms[h], ms[h]))
            ms2.append(jnp.maximum(ms[h], block_max(lms[h])))
        pvs = [jnp.dot(values(h), ps[h], preferred_element_type=F32) for h in range(A_HEADS)]
        rise = jnp.max(functools.reduce(jnp.maximum, [ms2[h] - ms[h] for h in range(A_HEADS)]))

        @pl.when(rise <= EXP2_HEADROOM)
        def _():
            for h in range(A_HEADS):
                hs = pl.ds(V_EXT * h, V_EXT)
                acc_ref[hs, :] = (acc_ref[hs, :] + pvs[h]) * jnp.exp2(ms[h] - ms2[h])

        @pl.when(rise > EXP2_HEADROOM)
        def _():
            exact(ms2)

        return cnt, tuple(ms2)

    init = (jnp.zeros((1, TQ), F32), tuple(jnp.full((1, TQ), M_FLOOR, F32) for _ in range(A_HEADS)))
    first = p3(0, init, speculate=False)
    lax.fori_loop(1, (nb + 1) // 2, functools.partial(p3, speculate=True), first)
    outT = jnp.concatenate([acc_ref[pl.ds(V_EXT * h, HEAD_DIM), :] / acc_ref[pl.ds(V_EXT * h + HEAD_DIM, 1), :]
                            for h in range(A_HEADS)], axis=0)
    o_ref[...] = outT.T.astype(BF16)


def _dsa(aqT, iqT, iwT, ik, ak, avT, S, tq):
    top_k = min(TOPK_MAX, S // 4)
    colT = lambda n: pl.BlockSpec((n, tq), lambda i: (0, i))
    return pl.pallas_call(
        functools.partial(_dsa_kernel, top_k=top_k),
        grid=(S // tq,),
        in_specs=[colT(D_A), colT(D_A), colT(SUBLANES),
                  _resident((S, LANES), lambda i: (0, 0)),
                  _resident((S, D_A), lambda i: (0, 0)),
                  _resident((A_HEADS * V_EXT, S), lambda i: (0, 0))],
        out_specs=pl.BlockSpec((tq, D_A), lambda i: (i, 0)),
        out_shape=jax.ShapeDtypeStruct((S, D_A), BF16),
        scratch_shapes=[pltpu.VMEM((S, tq), I32),
                        pltpu.VMEM((S, tq), jnp.int16),
                        pltpu.VMEM((S // DSA_GROUP + DSA_GCHUNK, tq), jnp.int16),
                        pltpu.VMEM((A_HEADS * V_EXT, tq), F32)],
        compiler_params=_params(("arbitrary",)),
        name="dsa",
    )(aqT, iqT, iwT, ik, ak, avT)


FF_CHUNKS = ((0, 768), (768, 768), (1536, 768), (2304, 512))


def _rms(v, w):
    return v * lax.rsqrt(jnp.mean(v * v, axis=-1, keepdims=True) + EPS) * w


def _outffn_kernel(x_ref, hm_ref, ha_ref, hs_ref, nw_ref, wo_ref, wg_ref, wu_ref, wd_ref, o_ref):
    mix = (jnp.dot(hm_ref[...], wo_ref[0:D_M, :], preferred_element_type=F32)
           + jnp.dot(ha_ref[...], wo_ref[D_M:D_M + D_A, :], preferred_element_type=F32)
           + jnp.dot(hs_ref[...], wo_ref[D_M + D_A:, :], preferred_element_type=F32))
    x1 = x_ref[...] + _rms(mix, nw_ref[1:2, :])
    h2 = _rms(x1, nw_ref[2:3, :]).astype(BF16)
    ff = jnp.zeros_like(x1)
    for c0, n in FF_CHUNKS:
        g = jnp.dot(h2, wg_ref[:, c0:c0 + n], preferred_element_type=F32)
        u = jnp.dot(h2, wu_ref[:, c0:c0 + n], preferred_element_type=F32)
        act = (g * jax.nn.sigmoid(g) * u).astype(BF16)
        ff = ff + jnp.dot(act, wd_ref[c0:c0 + n, :], preferred_element_type=F32)
    o_ref[...] = x1 + _rms(ff, nw_ref[3:4, :])


def _outffn(x2, hm, ha, hs, nw4, wo, wg, wu, wd, S, tm):
    row = lambda n: pl.BlockSpec((tm, n), lambda i: (i, 0))
    return pl.pallas_call(
        _outffn_kernel,
        grid=(S // tm,),
        in_specs=[row(D_MODEL), row(D_M), row(D_A), row(D_S),
                  pl.BlockSpec((4, D_MODEL), lambda i: (0, 0)),
                  _resident((D_MODEL, D_MODEL), lambda i: (0, 0)),
                  _resident((D_MODEL, D_FF), lambda i: (0, 0)),
                  _resident((D_MODEL, D_FF), lambda i: (0, 0)),
                  _resident((D_FF, D_MODEL), lambda i: (0, 0))],
        out_specs=row(D_MODEL),
        out_shape=jax.ShapeDtypeStruct((S, D_MODEL), F32),
        compiler_params=_params(("arbitrary",)),
        name="outffn",
    )(x2, hm, ha, hs, nw4, wo, wg, wu, wd)


def _plan(S):
    assert S % (2 * DSA_KB) == 0
    return dict(tm=min(S, 512), lm=min(S, 256), ls=min(S, 256), tq=min(S, 256))


def _pad_lanes(v):
    return jnp.pad(v, [(0, 0)] * (v.ndim - 1) + [(0, LANES - v.shape[-1])])


def kernel(x, positions, norm_w, w_in, mlstm_gate_bias, mlstm_norm_w, conv_w, conv_b, dt_bias, a_log,
           d_skip, ssd_norm_w, w_out, w_gate, w_up, w_down):
    B, S, D = x.shape
    assert B == 1 and D == D_MODEL
    depth = w_in.shape[0]
    plan = _plan(S)

    o = np.cumsum([0, D_M, D_M, D_M, D_M, M_HEADS, M_HEADS, D_A, D_A, D_A, IDX_HEADS * IDX_DIM, IDX_DIM,
                   IDX_HEADS, D_S, CONV_DIM, S_HEADS])
    wp = jnp.concatenate([
        w_in[:, :, o[0]:o[4]], w_in[:, :, o[6]:o[10]], w_in[:, :, o[12]:o[13]], w_in[:, :, o[13]:o[14]],
        _pad_lanes(w_in[:, :, o[4]:o[6]]), _pad_lanes(w_in[:, :, o[10]:o[12]]), _pad_lanes(w_in[:, :, o[14]:o[15]]),
    ], axis=-1).astype(BF16)
    wo = w_out.astype(BF16)
    wg = w_gate.astype(BF16)
    wu = w_up.astype(BF16)
    wd = w_down.astype(BF16)
    gb = _pad_lanes(mlstm_gate_bias[:, None, :])
    dtb = _pad_lanes(dt_bias[:, None, :])
    alog = _pad_lanes(a_log[:, None, :])
    dsk = jnp.repeat(d_skip, HEAD_DIM, axis=-1)[:, None, :]

    cf, sa, sb = _rope_tables(positions.astype(I32), S)
    x2 = x.reshape(S, D)
    for l in range(depth):
        om, omg, os_, odt, aqT, iqT, iwT, ik, ak, avT = _inproj(
            x2, norm_w[l, 0:1], wp[l], cf, sa, sb, S, plan["tm"])
        hm = _mlstm(om, omg, gb[l], mlstm_norm_w[l][None, :], S, plan["lm"])
        hs = _ssd(os_, odt, conv_w[l], conv_b[l][None, :], dtb[l], alog[l], dsk[l], ssd_norm_w[l][None, :],
                  S, plan["ls"])
        ha = _dsa(aqT, iqT, iwT, ik, ak, avT, S, plan["tq"])
        x2 = _outffn(x2, hm, ha, hs, norm_w[l], wo[l], wg[l], wu[l], wd[l], S, plan["tm"])
    return x2.reshape(B, S, D)
```

```python
import functools

import numpy as np
import jax
import jax.numpy as jnp
from jax import lax
from jax.experimental import pallas as pl
from jax.experimental.pallas import tpu as pltpu

F32 = jnp.float32
BF16 = jnp.bfloat16
I32 = jnp.int32

D_MODEL = 1024
HEAD_DIM = 64
D_M = 256
M_HEADS = 4
D_A = 256
A_HEADS = 4
IDX_HEADS = 4
IDX_DIM = 64
TOPK_MAX = 256
D_S = 512
S_HEADS = 8
S_GROUPS = 2
S_STATE = 128
S_CONV = 4
CONV_DIM = D_S + 2 * S_GROUPS * S_STATE
ROPE_THETA = 500000.0
ROPE_DIM = HEAD_DIM // 4
ROPE_HALF = ROPE_DIM // 2
D_FF = 2816
EPS = 1e-6

LANES = 128
SUBLANES = 8
PACK = 16
VMEM_LIMIT = 60 * 1024 * 1024

C_M = 0
C_A = C_M + 4 * D_M
C_Z = C_A + 4 * D_A
C_XBC = C_Z + D_S
C_MG = C_XBC + CONV_DIM
C_IK = C_MG + LANES
C_DT = C_IK + LANES
N_P = C_DT + LANES

INT_MIN = -2 ** 31
NEG_BIG = -1e30
M_FLOOR = -1e29
LOG2E = 1.4426950408889634
EXP2_HEADROOM = 64.0

DSA_KB = 512
DSA_AB = 256
DSA_GROUP = 16
DSA_GB = DSA_KB // DSA_GROUP
DSA_GCHUNK = 128
V_EXT = HEAD_DIM + PACK
DSA_MAX_STEPS = 24
DSA_FREE_STEPS = 7
TIE_ALL = 1e6
EXACT = -1e9
I16_MIN = -2 ** 15


def _params(sem):
    return pltpu.CompilerParams(dimension_semantics=sem, vmem_limit_bytes=VMEM_LIMIT)


def _resident(shape, index_map):
    return pl.BlockSpec(shape, index_map, pipeline_mode=pl.Buffered(1))


def _split3(x):
    h = x.astype(BF16)
    r = x - h.astype(F32)
    m = r.astype(BF16)
    lo = (r - m.astype(F32)).astype(BF16)
    return h, m, lo


def _cumsum_cols(tril, x):
    return sum(jnp.dot(tril, t, preferred_element_type=F32) for t in _split3(x))


def _cumsum_rows(x, triu):
    return sum(jnp.dot(t, triu, preferred_element_type=F32) for t in _split3(x))


def _tri(L):
    row = lax.broadcasted_iota(I32, (L, L), 0)
    col = lax.broadcasted_iota(I32, (L, L), 1)
    causal = col <= row
    tril = jnp.where(causal, 1.0, 0.0).astype(BF16)
    triu = jnp.where(row <= col, 1.0, 0.0).astype(BF16)
    return causal, tril, triu


def _rope_tables_kernel(pos_ref, inv_ref, cf_ref, sa_ref, sb_ref):
    ang = pos_ref[...].astype(F32) * inv_ref[...]
    c = jnp.cos(ang)
    s = jnp.sin(ang)
    j = lax.broadcasted_iota(I32, ang.shape, 1) & (HEAD_DIM - 1)
    cf_ref[...] = jnp.where(j < ROPE_DIM, c, 1.0)
    sa_ref[...] = jnp.where(j < ROPE_HALF, -s, 0.0)
    sb_ref[...] = jnp.where(j < ROPE_HALF, 0.0, jnp.where(j < ROPE_DIM, s, 0.0))


def _rope_tables(positions, S):
    tb = min(S, 1024)
    inv = np.power(np.float32(ROPE_THETA), -np.arange(ROPE_HALF, dtype=np.float32) / np.float32(ROPE_HALF))
    lane = np.arange(LANES) % HEAD_DIM
    inv_lanes = np.where(lane < ROPE_DIM, inv[lane % ROPE_HALF], np.float32(0)).astype(np.float32)[None, :]
    tab = jax.ShapeDtypeStruct((S, LANES), F32)
    row = pl.BlockSpec((tb, LANES), lambda i: (i, 0))
    return pl.pallas_call(
        _rope_tables_kernel,
        grid=(S // tb,),
        in_specs=[pl.BlockSpec((tb, 1), lambda i: (i, 0)), pl.BlockSpec((1, LANES), lambda i: (0, 0))],
        out_specs=[row, row, row],
        out_shape=[tab, tab, tab],
        compiler_params=_params(("arbitrary",)),
        name="rope_tables",
    )(positions.reshape(S, 1), jnp.asarray(inv_lanes))


def _inproj_kernel(x_ref, nw_ref, w_ref, cf_ref, sa_ref, sb_ref,
                   om_ref, omg_ref, os_ref, odt_ref,
                   aqT_ref, iqT_ref, iwT_ref, ik_ref, ak_ref, avT_ref):
    x = x_ref[...]
    ms = jnp.mean(x * x, axis=-1, keepdims=True)
    h = (x * lax.rsqrt(ms + EPS) * nw_ref[...]).astype(BF16)

    def proj(c0, n):
        return jnp.dot(h, w_ref[:, c0:c0 + n], preferred_element_type=F32)

    om_ref[...] = proj(C_M, 4 * D_M)
    os_ref[...] = proj(C_Z, D_S + CONV_DIM)
    omg_ref[...] = proj(C_MG, LANES)
    odt_ref[...] = proj(C_DT, LANES)

    cf = cf_ref[...]
    sa = sa_ref[...]
    sb = sb_ref[...]

    def rope(c, cf=cf, sa=sa, sb=sb):
        return c * cf + pltpu.roll(c, LANES - ROPE_HALF, 1) * sa + pltpu.roll(c, ROPE_HALF, 1) * sb

    def rope2(a2):
        return jnp.concatenate([rope(a2[:, :LANES]), rope(a2[:, LANES:])], axis=1)

    a = proj(C_A, 4 * D_A)
    aqT_ref[...] = (rope2(a[:, 0:D_A]) * (HEAD_DIM ** -0.5 * LOG2E)).T.astype(BF16)
    ak_ref[...] = rope2(a[:, D_A:2 * D_A]).astype(BF16)
    vT = a[:, 2 * D_A:3 * D_A].T.astype(BF16)
    ones = jnp.ones((V_EXT - HEAD_DIM, vT.shape[1]), BF16)
    avT_ref[...] = jnp.concatenate(
        [t for h in range(A_HEADS) for t in (vT[HEAD_DIM * h:HEAD_DIM * (h + 1)], ones)], axis=0)
    iqT_ref[...] = rope2(a[:, 3 * D_A:4 * D_A]).T.astype(BF16)

    ikw = proj(C_IK, LANES)
    is_ik = lax.broadcasted_iota(I32, ikw.shape, 1) < IDX_DIM
    ikr = rope(ikw, jnp.where(is_ik, cf, 1.0), jnp.where(is_ik, sa, 0.0), jnp.where(is_ik, sb, 0.0))
    ik_ref[...] = jnp.where(is_ik, ikr, 0.0).astype(BF16)
    iwT_ref[...] = ikw.T[IDX_DIM:IDX_DIM + SUBLANES, :]


def _inproj(x2, nw, wp, cf, sa, sb, S, tm):
    row = lambda n: pl.BlockSpec((tm, n), lambda i: (i, 0))
    colT = lambda n: pl.BlockSpec((n, tm), lambda i: (0, i))
    f = lambda n, dt=F32: jax.ShapeDtypeStruct((S, n), dt)
    fT = lambda n, dt=BF16: jax.ShapeDtypeStruct((n, S), dt)
    return pl.pallas_call(
        _inproj_kernel,
        grid=(S // tm,),
        in_specs=[row(D_MODEL), pl.BlockSpec((1, D_MODEL), lambda i: (0, 0)),
                  _resident((D_MODEL, N_P), lambda i: (0, 0)),
                  row(LANES), row(LANES), row(LANES)],
        out_specs=[row(4 * D_M), row(LANES), row(D_S + CONV_DIM), row(LANES),
                   colT(D_A), colT(D_A), colT(SUBLANES), row(LANES), row(D_A), colT(A_HEADS * V_EXT)],
        out_shape=[f(4 * D_M), f(LANES), f(D_S + CONV_DIM), f(LANES),
                   fT(D_A), fT(D_A), fT(SUBLANES, F32), f(LANES, BF16), f(D_A, BF16), fT(A_HEADS * V_EXT)],
        compiler_params=_params(("arbitrary",)),
        name="inproj",
    )(x2, nw, wp, cf, sa, sb)


def _mlstm_kernel(om_ref, omg_ref, gb_ref, nw_ref, o_ref, C_ref, n_ref, m_ref):
    L = om_ref.shape[0]

    @pl.when(pl.program_id(0) == 0)
    def _():
        C_ref[...] = jnp.zeros_like(C_ref)
        n_ref[...] = jnp.zeros_like(n_ref)
        m_ref[...] = jnp.zeros_like(m_ref)

    causal, tril, triu = _tri(L)
    G = omg_ref[...] + gb_ref[...]
    LF = jnp.minimum(G, 0.0) - jnp.log1p(jnp.exp(-jnp.abs(G)))
    Bc = _cumsum_cols(tril, LF)
    GT = G.T
    BrT = _cumsum_rows(LF.T, triu)
    nw = nw_ref[...]
    heads = range(M_HEADS)

    def head(c0, h):
        blk = om_ref[:, pl.ds(c0 + LANES * (h // 2), LANES)]
        return blk[:, HEAD_DIM * (h % 2):HEAD_DIM * (h % 2 + 1)]

    qf = [head(0, h) * (HEAD_DIM ** -0.5) for h in heads]
    kf = [head(D_M, h) for h in heads]
    q = [t.astype(BF16) for t in qf]
    k = [t.astype(BF16) for t in kf]
    v = [head(2 * D_M, h).astype(BF16) for h in heads]
    C_prev = [C_ref[h] for h in heads]
    n_prev = [n_ref[h:h + 1, :] for h in heads]
    m_prev = [m_ref[h:h + 1, 0:1] for h in heads]
    qk = [lax.dot_general(q[h], k[h], (((1,), (1,)), ((), ())), preferred_element_type=F32) for h in heads]
    qC = [jnp.dot(q[h], C_prev[h].astype(BF16), preferred_element_type=F32) for h in heads]
    s, scale, m_t, kw, decay = [], [], [], [], []
    for h in heads:
        f_l = M_HEADS + h
        b_col = Bc[:, f_l:f_l + 1]
        b_row = BrT[f_l:f_l + 1, :]
        i_col = G[:, h:h + 1]
        i_row = GT[h:h + 1, :]
        logd = jnp.where(causal, b_col + (i_row - b_row), -jnp.inf)
        m_inter = b_col + m_prev[h]
        mt = jnp.maximum(m_inter, jnp.max(logd, axis=-1, keepdims=True))
        s.append(qk[h] * jnp.exp(logd - mt))
        scale.append(jnp.exp(m_inter - mt))
        m_t.append(mt)
        b_last = Bc[L - 1:L, f_l:f_l + 1]
        m_new = jnp.maximum(b_last + m_prev[h], jnp.max(b_last - b_row + i_row, axis=-1, keepdims=True))
        kw.append(kf[h] * jnp.exp(b_last - b_col + i_col - m_new))
        decay.append(jnp.exp(b_last + m_prev[h] - m_new))
        m_ref[h:h + 1, :] = jnp.broadcast_to(m_new, (1, LANES))
    sv = [jnp.dot(s[h].astype(BF16), v[h], preferred_element_type=F32) for h in heads]
    kv = [lax.dot_general(kw[h].astype(BF16), v[h], (((0,), (0,)), ((), ())), preferred_element_type=F32)
          for h in heads]
    outs = []
    for h in heads:
        num = sv[h] + scale[h] * qC[h]
        den = (jnp.sum(s[h], axis=-1, keepdims=True)
               + scale[h] * jnp.sum(qf[h] * n_prev[h], axis=-1, keepdims=True))
        hh = num / jnp.maximum(jnp.abs(den), jnp.exp(-m_t[h]))
        C_ref[h] = decay[h] * C_prev[h] + kv[h]
        n_ref[h:h + 1, :] = decay[h] * n_prev[h] + jnp.sum(kw[h], axis=0, keepdims=True)
        y = hh * lax.rsqrt(jnp.mean(hh * hh, axis=-1, keepdims=True) + EPS)
        outs.append(jax.nn.sigmoid(head(3 * D_M, h)) * y)
    o_ref[...] = (jnp.concatenate(outs, axis=1) * nw).astype(BF16)


def _mlstm(om, omg, gb, nw, S, L):
    return pl.pallas_call(
        _mlstm_kernel,
        grid=(S // L,),
        in_specs=[pl.BlockSpec((L, 4 * D_M), lambda i: (i, 0)), pl.BlockSpec((L, LANES), lambda i: (i, 0)),
                  pl.BlockSpec((1, LANES), lambda i: (0, 0)), pl.BlockSpec((1, D_M), lambda i: (0, 0))],
        out_specs=pl.BlockSpec((L, D_M), lambda i: (i, 0)),
        out_shape=jax.ShapeDtypeStruct((S, D_M), BF16),
        scratch_shapes=[pltpu.VMEM((M_HEADS, HEAD_DIM, HEAD_DIM), F32),
                        pltpu.VMEM((SUBLANES, HEAD_DIM), F32),
                        pltpu.VMEM((SUBLANES, LANES), F32)],
        compiler_params=_params(("arbitrary",)),
        name="mlstm",
    )(om, omg, gb, nw)


def _ssd_kernel(os_ref, odt_ref, cw_ref, cb_ref, dtb_ref, alog_ref, dsk_ref, nw_ref, o_ref,
                carry_ref, st_ref):
    L = os_ref.shape[0]

    @pl.when(pl.program_id(0) == 0)
    def _():
        carry_ref[...] = jnp.zeros_like(carry_ref)
        st_ref[...] = jnp.zeros_like(st_ref)

    causal, tril, triu = _tri(L)
    z = os_ref[:, 0:D_S]
    raw = os_ref[:, D_S:D_S + CONV_DIM]
    ext = jnp.concatenate([carry_ref[...], raw], axis=0)
    cw = cw_ref[...]
    xbc = cb_ref[...] + cw[S_CONV - 1:S_CONV, :] * raw
    for j in range(S_CONV - 1):
        off = SUBLANES - (S_CONV - 1) + j
        xbc = xbc + cw[j:j + 1, :] * ext[off:off + L, :]
    carry_ref[...] = raw[L - SUBLANES:L, :]
    xbc = xbc * jax.nn.sigmoid(xbc)

    dtr = odt_ref[...] + dtb_ref[...]
    DT = jnp.maximum(dtr, 0.0) + jnp.log1p(jnp.exp(-jnp.abs(dtr)))
    dA = DT * (-jnp.exp(alog_ref[...]))
    Ac = _cumsum_cols(tril, dA)
    ArT = _cumsum_rows(dA.T, triu)
    DTT = DT.T
    hpg = S_HEADS // S_GROUPS
    ys = []
    for g in range(S_GROUPS):
        Bg = xbc[:, D_S + S_STATE * g:D_S + S_STATE * (g + 1)]
        Cg = xbc[:, D_S + S_GROUPS * S_STATE + S_STATE * g:D_S + S_GROUPS * S_STATE + S_STATE * (g + 1)]
        Cb = Cg.astype(BF16)
        CB = lax.dot_general(Cb, Bg.astype(BF16), (((1,), (1,)), ((), ())), preferred_element_type=F32)
        BgT = Bg.T
        for hh in range(hpg):
            h = g * hpg + hh
            xpair = xbc[:, LANES * (h // 2):LANES * (h // 2 + 1)]
            xh = xpair[:, HEAD_DIM * (h % 2):HEAD_DIM * (h % 2 + 1)]
            ac_col = Ac[:, h:h + 1]
            ac_row = ArT[h:h + 1, :]
            dec = jnp.exp(jnp.where(causal, ac_col - ac_row, -jnp.inf))
            sc = (CB * dec).astype(BF16)
            xdt = (xh * DT[:, h:h + 1]).astype(BF16)
            st = st_ref[h]
            y = (jnp.dot(sc, xdt, preferred_element_type=F32)
                 + jnp.dot(Cb, st.astype(BF16), preferred_element_type=F32) * jnp.exp(ac_col))
            a_last = Ac[L - 1:L, h:h + 1]
            w_row = jnp.exp(a_last - ac_row) * DTT[h:h + 1, :]
            st_ref[h] = st * jnp.exp(a_last) + jnp.dot((BgT * w_row).astype(BF16), xh.astype(BF16),
                                                       preferred_element_type=F32)
            ys.append(y)
    Y = jnp.concatenate(ys, axis=1) + dsk_ref[...] * xbc[:, 0:D_S]
    gated = Y * (z * jax.nn.sigmoid(z))
    gw = D_S // S_GROUPS
    outs = []
    for g in range(S_GROUPS):
        gg = gated[:, gw * g:gw * (g + 1)]
        outs.append(gg * lax.rsqrt(jnp.mean(gg * gg, axis=-1, keepdims=True) + EPS))
    o_ref[...] = (jnp.concatenate(outs, axis=1) * nw_ref[...]).astype(BF16)


def _ssd(os_, odt, cw, cb, dtb, alog, dsk, nw, S, L):
    full = lambda r, c: pl.BlockSpec((r, c), lambda i: (0, 0))
    return pl.pallas_call(
        _ssd_kernel,
        grid=(S // L,),
        in_specs=[pl.BlockSpec((L, D_S + CONV_DIM), lambda i: (i, 0)), pl.BlockSpec((L, LANES), lambda i: (i, 0)),
                  full(S_CONV, CONV_DIM), full(1, CONV_DIM), full(1, LANES), full(1, LANES),
                  full(1, D_S), full(1, D_S)],
        out_specs=pl.BlockSpec((L, D_S), lambda i: (i, 0)),
        out_shape=jax.ShapeDtypeStruct((S, D_S), BF16),
        scratch_shapes=[pltpu.VMEM((SUBLANES, CONV_DIM), F32),
                        pltpu.VMEM((S_HEADS, S_STATE, HEAD_DIM), F32)],
        compiler_params=_params(("arbitrary",)),
        name="ssd",
    )(os_, odt, cw, cb, dtb, alog, dsk, nw)


def _max16(a, b):
    return jnp.where(a >= b, a, b)


def _count_ge16(ref, nblk, rb, thr_row, tq):
    thr = jnp.broadcast_to(thr_row.astype(jnp.int16), (PACK, tq))
    n_acc = 4
    one = jnp.ones((PACK, tq), jnp.int16)

    def body(b, accs):
        slab = ref[pl.ds(pl.multiple_of(b * rb, rb), rb), :]
        accs = list(accs)
        for j in range(rb // PACK):
            blk = slab[PACK * j:PACK * (j + 1)]
            a = accs[j % n_acc]
            accs[j % n_acc] = jnp.where(blk >= thr, a + one, a)
        return tuple(accs)

    accs = lax.fori_loop(0, nblk, body, tuple(jnp.zeros((PACK, tq), jnp.int16) for _ in range(n_acc)))
    tot = (accs[0].astype(I32) + accs[1].astype(I32)) + (accs[2].astype(I32) + accs[3].astype(I32))
    return jnp.sum(tot.astype(F32), axis=0, keepdims=True)


def _avg_floor(lo, hi):
    return (lo >> 1) + (hi >> 1) + (lo & hi & 1)


def _dsa_kernel(aqT_ref, iqT_ref, iwT_ref, ik_ref, ak_ref, avT_ref, o_ref, key_ref, k16_ref, gm_ref, acc_ref,
                *, top_k):
    TQ = aqT_ref.shape[1]
    q0 = pl.program_id(0) * TQ
    nb = (q0 + TQ + DSA_KB - 1) // DSA_KB
    qpos = q0 + lax.broadcasted_iota(I32, (1, TQ), 1)
    kf = float(top_k)
    w_scale = (IDX_HEADS ** -0.5) * (IDX_DIM ** -0.5)

    iqT = iqT_ref[...]
    zpad = jnp.zeros((LANES - IDX_DIM, TQ), BF16)
    iq_pad = [jnp.concatenate([iqT[IDX_DIM * h:IDX_DIM * (h + 1)], zpad], axis=0) for h in range(IDX_HEADS)]
    w_rows = [iwT_ref[h:h + 1, :] * w_scale for h in range(IDX_HEADS)]

    gm_ref[pl.ds(pl.multiple_of(nb * DSA_GB, DSA_GB), DSA_GCHUNK - DSA_GB), :] = jnp.full(
        (DSA_GCHUNK - DSA_GB, TQ), I16_MIN, jnp.int16)

    @pl.when(nb % 2 == 1)
    def _():
        key_ref[pl.ds(pl.multiple_of(nb * DSA_KB, DSA_KB), DSA_KB), :] = jnp.full((DSA_KB, TQ), INT_MIN, I32)

    half = DSA_KB // 2

    def p1(step, c, masked, per_step):
        n_half = 2 * per_step
        base = step * (per_step * DSA_KB)
        k0s = [pl.multiple_of(base + s2 * half, half) for s2 in range(n_half)]
        dots = [[jnp.dot(ik_ref[pl.ds(k0s[s2], half), :], iq_pad[h], preferred_element_type=F32)
                 for h in range(IDX_HEADS)] for s2 in range(n_half)]
        gms = []
        for s2 in range(n_half):
            k0 = k0s[s2]
            sc = jnp.zeros((half, TQ), F32)
            for h in range(IDX_HEADS):
                sc = sc + w_rows[h] * jnp.maximum(dots[s2][h], 0.0)
            bits = lax.bitcast_convert_type(sc, I32)
            key = jnp.where(bits < 0, INT_MIN - bits, bits)
            if masked:
                kpos = k0 + lax.broadcasted_iota(I32, (half, 1), 0)
                key = jnp.where(kpos <= qpos, key, INT_MIN)
            key_ref[pl.ds(k0, half), :] = key
            k16 = (key >> 16).astype(jnp.int16)
            k16_ref[pl.ds(k0, half), :] = k16
            span = DSA_GROUP * PACK
            for g in range(half // span):
                m = k16[span * g:span * g + PACK]
                for j in range(1, DSA_GROUP):
                    m = _max16(m, k16[span * g + PACK * j:span * g + PACK * (j + 1)])
                gms.append(m)
        rows = per_step * DSA_GB
        gm_ref[pl.ds(pl.multiple_of(step * rows, rows), rows), :] = jnp.concatenate(gms, axis=0)
        return c

    nfull = q0 // DSA_KB
    lax.fori_loop(0, nfull // 4, functools.partial(p1, masked=False, per_step=4), 0)
    lax.fori_loop(2 * (nfull // 4), nfull // 2, functools.partial(p1, masked=False, per_step=2), 0)
    lax.fori_loop(2 * (nfull // 2), nfull, functools.partial(p1, masked=False, per_step=1), 0)
    lax.fori_loop(nfull, nb, functools.partial(p1, masked=True, per_step=1), 0)

    ngc = (nb * DSA_GB + DSA_GCHUNK - 1) // DSA_GCHUNK

    def gm_bit(it, prefix):
        cand = prefix | jnp.left_shift(jnp.int32(1), 15 - it)
        cnt = _count_ge16(gm_ref, ngc, DSA_GCHUNK, cand + I16_MIN, TQ)
        return jnp.where(cnt >= kf, cand, prefix)

    lo_h = lax.fori_loop(0, 16, gm_bit, jnp.zeros((1, TQ), I32)) + I16_MIN

    def gm_max(c, m):
        r0 = pl.multiple_of(c * DSA_GCHUNK, DSA_GCHUNK)
        for j in range(DSA_GCHUNK // PACK):
            m = _max16(m, gm_ref[pl.ds(r0 + PACK * j, PACK), :])
        return m

    gmax = lax.fori_loop(0, ngc, gm_max, jnp.full((PACK, TQ), I16_MIN, jnp.int16))
    hi_h = jnp.max(gmax.astype(I32), axis=0, keepdims=True) + 1

    def halve(lo_0, hi_0, want, first_probe, n_free):
        def step(it, lo, hi, chi):
            probe = jnp.where(it < 1, first_probe, INT_MIN)
            mid = jnp.where((lo < probe) & (probe < hi), probe, _avg_floor(lo, hi))
            c = _count_ge16(k16_ref, nb, DSA_KB, mid, TQ)
            ge = c >= want
            ex = jnp.logical_and(c == want, mid != lo)
            lo2 = jnp.where(ge, mid, lo)
            hi2 = jnp.where(ex, mid + 1, jnp.where(ge, hi, mid))
            chi2 = jnp.where(ex, EXACT, jnp.where(ge, chi, c))
            return lo2, hi2, chi2

        def active(lo, hi):
            return jnp.max(jnp.where(_avg_floor(lo, hi) != lo, 1.0, 0.0))

        def w_cond(st):
            return jnp.logical_and(st[0] < DSA_MAX_STEPS, st[4] > 0.0)

        def w_body(st):
            lo2, hi2, chi2 = step(st[0], st[1], st[2], st[3])
            return st[0] + 1, lo2, hi2, chi2, active(lo2, hi2)

        lo, hi, chi = lax.fori_loop(0, n_free, lambda it, st: step(it, *st),
                                    (lo_0, hi_0, jnp.zeros((1, TQ), F32)))
        _, lo, _, chi, _ = lax.while_loop(w_cond, w_body, (jnp.int32(n_free), lo, hi, chi, active(lo, hi)))
        return lo, chi

    tau_h, chi_h = halve(lo_h, hi_h, kf, INT_MIN, DSA_FREE_STEPS)
    exact_h = chi_h == EXACT
    want_l = kf - chi_h

    def build(b, c):
        k0 = pl.multiple_of(b * DSA_KB, DSA_KB)
        key = key_ref[pl.ds(k0, DSA_KB), :]
        low = jnp.where((key >> 16) == tau_h, (key & 0xFFFF) + I16_MIN, I16_MIN)
        k16_ref[pl.ds(k0, DSA_KB), :] = low.astype(jnp.int16)
        return c

    lax.fori_loop(0, nb, build, 0)
    tau_l, chi_l = halve(jnp.where(exact_h, 0, I16_MIN), jnp.where(exact_h, 1, -I16_MIN), want_l, I16_MIN + 1,
                         DSA_FREE_STEPS)
    tau = jnp.where(exact_h, tau_h << 16, (tau_h << 16) + (tau_l - I16_MIN))
    r = jnp.where(tau == INT_MIN, 0.0,
                  jnp.where(jnp.logical_or(exact_h, chi_l == EXACT), TIE_ALL, want_l - chi_l))

    AB = DSA_AB
    rowi = lax.broadcasted_iota(I32, (AB, AB), 0)
    coli = lax.broadcasted_iota(I32, (AB, AB), 1)
    tril = jnp.where(coli <= rowi, 1.0, 0.0).astype(BF16)
    aqT = aqT_ref[...]
    hrow = lax.broadcasted_iota(I32, (LANES, TQ), 0) // HEAD_DIM
    q_pad = []
    for h in range(A_HEADS):
        pair = aqT[LANES * (h // 2):LANES * (h // 2 + 1)]
        q_pad.append(jnp.where(hrow == (h % 2), pair, jnp.zeros_like(pair)))
    acc_ref[...] = jnp.zeros_like(acc_ref)

    nsub = 2 * DSA_KB // AB

    def p3(b, carry, speculate):
        cnt, ms = carry
        k0 = pl.multiple_of(b * 2 * DSA_KB, 2 * DSA_KB)
        bias = []
        for s2 in range(nsub):
            key = key_ref[pl.ds(k0 + s2 * AB, AB), :]
            eq = key == tau
            pref = jnp.dot(tril, jnp.where(eq, 1.0, 0.0).astype(BF16), preferred_element_type=F32) + cnt
            sel = jnp.where(key > tau, 0.0, jnp.where(eq, pref, 1e9)) <= r
            bias.append(jnp.where(sel, 0.0, NEG_BIG))
            cnt = pref[AB - 1:AB, :]

        def logits(h):
            return [jnp.dot(ak_ref[pl.ds(k0 + s2 * AB, AB), pl.ds(LANES * (h // 2), LANES)], q_pad[h],
                            preferred_element_type=F32) + bias[s2] for s2 in range(nsub)]

        def block_max(lm):
            return functools.reduce(jnp.maximum, [jnp.max(t, axis=0, keepdims=True) for t in lm])

        def weights(lm, m):
            return jnp.concatenate([jnp.exp2(t - m).astype(BF16) for t in lm], axis=0)

        def values(h):
            return avT_ref[pl.ds(V_EXT * h, V_EXT), pl.ds(k0, 2 * DSA_KB)]

        def exact(ms2, lms=None):
            for h in range(A_HEADS):
                hs = pl.ds(V_EXT * h, V_EXT)
                lm = logits(h) if lms is None else lms[h]
                pv = jnp.dot(values(h), weights(lm, ms2[h]), preferred_element_type=F32)
                acc_ref[hs, :] = acc_ref[hs, :] * jnp.exp2(ms[h] - ms2[h]) + pv

        lms = [logits(h) for h in range(A_HEADS)]
        if not speculate:
            ms2 = [jnp.maximum(ms[h], block_max(lms[h])) for h in range(A_HEADS)]
            exact(ms2, lms)
            return cnt, tuple(ms2)

        ps, ms2 = [], []
        for h in range(A_HEADS):
            ps.append(weights(lms[h], ms[h]))
            ms2.append(jnp.maximum(ms[h], block_max(lms[h])))
        pvs = [jnp.dot(values(h), ps[h], preferred_element_type=F32) for h in range(A_HEADS)]
        rise = jnp.max(functools.reduce(jnp.maximum, [ms2[h] - ms[h] for h in range(A_HEADS)]))

        @pl.when(rise <= EXP2_HEADROOM)
        def _():
            for h in range(A_HEADS):
                hs = pl.ds(V_EXT * h, V_EXT)
                acc_ref[hs, :] = (acc_ref[hs, :] + pvs[h]) * jnp.exp2(ms[h] - ms2[h])

        @pl.when(rise > EXP2_HEADROOM)
        def _():
            exact(ms2)

        return cnt, tuple(ms2)

    init = (jnp.zeros((1, TQ), F32), tuple(jnp.full((1, TQ), M_FLOOR, F32) for _ in range(A_HEADS)))
    first = p3(0, init, speculate=False)
    lax.fori_loop(1, (nb + 1) // 2, functools.partial(p3, speculate=True), first)
    outT = jnp.concatenate([acc_ref[pl.ds(V_EXT * h, HEAD_DIM), :] / acc_ref[pl.ds(V_EXT * h + HEAD_DIM, 1), :]
                            for h in range(A_HEADS)], axis=0)
    o_ref[...] = outT.T.astype(BF16)


def _dsa(aqT, iqT, iwT, ik, ak, avT, S, tq):
    top_k = min(TOPK_MAX, S // 4)
    colT = lambda n: pl.BlockSpec((n, tq), lambda i: (0, i))
    return pl.pallas_call(
        functools.partial(_dsa_kernel, top_k=top_k),
        grid=(S // tq,),
        in_specs=[colT(D_A), colT(D_A), colT(SUBLANES),
                  _resident((S, LANES), lambda i: (0, 0)),
                  _resident((S, D_A), lambda i: (0, 0)),
                  _resident((A_HEADS * V_EXT, S), lambda i: (0, 0))],
        out_specs=pl.BlockSpec((tq, D_A), lambda i: (i, 0)),
        out_shape=jax.ShapeDtypeStruct((S, D_A), BF16),
        scratch_shapes=[pltpu.VMEM((S, tq), I32),
                        pltpu.VMEM((S, tq), jnp.int16),
                        pltpu.VMEM((S // DSA_GROUP + DSA_GCHUNK, tq), jnp.int16),
                        pltpu.VMEM((A_HEADS * V_EXT, tq), F32)],
        compiler_params=_params(("arbitrary",)),
        name="dsa",
    )(aqT, iqT, iwT, ik, ak, avT)


FF_CHUNKS = ((0, 768), (768, 768), (1536, 768), (2304, 512))


def _rms(v, w):
    return v * lax.rsqrt(jnp.mean(v * v, axis=-1, keepdims=True) + EPS) * w


def _outffn_kernel(x_ref, hm_ref, ha_ref, hs_ref, nw_ref, wo_ref, wg_ref, wu_ref, wd_ref, o_ref):
    mix = (jnp.dot(hm_ref[...], wo_ref[0:D_M, :], preferred_element_type=F32)
           + jnp.dot(ha_ref[...], wo_ref[D_M:D_M + D_A, :], preferred_element_type=F32)
           + jnp.dot(hs_ref[...], wo_ref[D_M + D_A:, :], preferred_element_type=F32))
    x1 = x_ref[...] + _rms(mix, nw_ref[1:2, :])
    h2 = _rms(x1, nw_ref[2:3, :]).astype(BF16)
    ff = jnp.zeros_like(x1)
    for c0, n in FF_CHUNKS:
        g = jnp.dot(h2, wg_ref[:, c0:c0 + n], preferred_element_type=F32)
        u = jnp.dot(h2, wu_ref[:, c0:c0 + n], preferred_element_type=F32)
        act = (g * jax.nn.sigmoid(g) * u).astype(BF16)
        ff = ff + jnp.dot(act, wd_ref[c0:c0 + n, :], preferred_element_type=F32)
    o_ref[...] = x1 + _rms(ff, nw_ref[3:4, :])


def _outffn(x2, hm, ha, hs, nw4, wo, wg, wu, wd, S, tm):
    row = lambda n: pl.BlockSpec((tm, n), lambda i: (i, 0))
    return pl.pallas_call(
        _outffn_kernel,
        grid=(S // tm,),
        in_specs=[row(D_MODEL), row(D_M), row(D_A), row(D_S),
                  pl.BlockSpec((4, D_MODEL), lambda i: (0, 0)),
                  _resident((D_MODEL, D_MODEL), lambda i: (0, 0)),
                  _resident((D_MODEL, D_FF), lambda i: (0, 0)),
                  _resident((D_MODEL, D_FF), lambda i: (0, 0)),
                  _resident((D_FF, D_MODEL), lambda i: (0, 0))],
        out_specs=row(D_MODEL),
        out_shape=jax.ShapeDtypeStruct((S, D_MODEL), F32),
        compiler_params=_params(("arbitrary",)),
        name="outffn",
    )(x2, hm, ha, hs, nw4, wo, wg, wu, wd)


def _plan(S):
    assert S % (2 * DSA_KB) == 0
    return dict(tm=min(S, 512), lm=min(S, 256), ls=min(S, 256), tq=min(S, 256))


def _pad_lanes(v):
    return jnp.pad(v, [(0, 0)] * (v.ndim - 1) + [(0, LANES - v.shape[-1])])


def kernel(x, positions, norm_w, w_in, mlstm_gate_bias, mlstm_norm_w, conv_w, conv_b, dt_bias, a_log,
           d_skip, ssd_norm_w, w_out, w_gate, w_up, w_down):
    B, S, D = x.shape
    assert B == 1 and D == D_MODEL
    depth = w_in.shape[0]
    plan = _plan(S)

    o = np.cumsum([0, D_M, D_M, D_M, D_M, M_HEADS, M_HEADS, D_A, D_A, D_A, IDX_HEADS * IDX_DIM, IDX_DIM,
                   IDX_HEADS, D_S, CONV_DIM, S_HEADS])
    wp = jnp.concatenate([
        w_in[:, :, o[0]:o[4]], w_in[:, :, o[6]:o[10]], w_in[:, :, o[12]:o[13]], w_in[:, :, o[13]:o[14]],
        _pad_lanes(w_in[:, :, o[4]:o[6]]), _pad_lanes(w_in[:, :, o[10]:o[12]]), _pad_lanes(w_in[:, :, o[14]:o[15]]),
    ], axis=-1).astype(BF16)
    wo = w_out.astype(BF16)
    wg = w_gate.astype(BF16)
    wu = w_up.astype(BF16)
    wd = w_down.astype(BF16)
    gb = _pad_lanes(mlstm_gate_bias[:, None, :])
    dtb = _pad_lanes(dt_bias[:, None, :])
    alog = _pad_lanes(a_log[:, None, :])
    dsk = jnp.repeat(d_skip, HEAD_DIM, axis=-1)[:, None, :]

    cf, sa, sb = _rope_tables(positions.astype(I32), S)
    x2 = x.reshape(S, D)
    for l in range(depth):
        om, omg, os_, odt, aqT, iqT, iwT, ik, ak, avT = _inproj(
            x2, norm_w[l, 0:1], wp[l], cf, sa, sb, S, plan["tm"])
        hm = _mlstm(om, omg, gb[l], mlstm_norm_w[l][None, :], S, plan["lm"])
        hs = _ssd(os_, odt, conv_w[l], conv_b[l][None, :], dtb[l], alog[l], dsk[l], ssd_norm_w[l][None, :],
                  S, plan["ls"])
        ha = _dsa(aqT, iqT, iwT, ik, ak, avT, S, plan["tq"])
        x2 = _outffn(x2, hm, ha, hs, norm_w[l], wo[l], wg[l], wu[l], wd[l], S, plan["tm"])
    return x2.reshape(B, S, D)
```

```python
import functools

import numpy as np
import jax
import jax.numpy as jnp
from jax import lax
from jax.experimental import pallas as pl
from jax.experimental.pallas import tpu as pltpu

F32 = jnp.float32
BF16 = jnp.bfloat16
I32 = jnp.int32

D_MODEL = 1024
HEAD_DIM = 64
D_M = 256
M_HEADS = 4
D_A = 256
A_HEADS = 4
IDX_HEADS = 4
IDX_DIM = 64
TOPK_MAX = 256
D_S = 512
S_HEADS = 8
S_GROUPS = 2
S_STATE = 128
S_CONV = 4
CONV_DIM = D_S + 2 * S_GROUPS * S_STATE
ROPE_THETA = 500000.0
ROPE_DIM = HEAD_DIM // 4
ROPE_HALF = ROPE_DIM // 2
D_FF = 2816
EPS = 1e-6

LANES = 128
SUBLANES = 8
PACK = 16
VMEM_LIMIT = 60 * 1024 * 1024

C_M = 0
C_A = C_M + 4 * D_M
C_Z = C_A + 4 * D_A
C_XBC = C_Z + D_S
C_MG = C_XBC + CONV_DIM
C_IK = C_MG + LANES
C_DT = C_IK + LANES
N_P = C_DT + LANES

INT_MIN = -2 ** 31
NEG_BIG = -1e30
M_FLOOR = -1e29
LOG2E = 1.4426950408889634
EXP2_HEADROOM = 64.0

DSA_KB = 512
DSA_AB = 256
DSA_GROUP = 16
DSA_GB = DSA_KB // DSA_GROUP
DSA_GCHUNK = 128
V_EXT = HEAD_DIM + PACK
DSA_MAX_STEPS = 24
DSA_FREE_STEPS = 7
TIE_ALL = 1e6
EXACT = -1e9
I16_MIN = -2 ** 15


def _params(sem):
    return pltpu.CompilerParams(dimension_semantics=sem, vmem_limit_bytes=VMEM_LIMIT)


def _resident(shape, index_map):
    return pl.BlockSpec(shape, index_map, pipeline_mode=pl.Buffered(1))


def _split3(x):
    h = x.astype(BF16)
    r = x - h.astype(F32)
    m = r.astype(BF16)
    lo = (r - m.astype(F32)).astype(BF16)
    return h, m, lo


def _cumsum_cols(tril, x):
    return sum(jnp.dot(tril, t, preferred_element_type=F32) for t in _split3(x))


def _cumsum_rows(x, triu):
    return sum(jnp.dot(t, triu, preferred_element_type=F32) for t in _split3(x))


def _tri(L):
    row = lax.broadcasted_iota(I32, (L, L), 0)
    col = lax.broadcasted_iota(I32, (L, L), 1)
    causal = col <= row
    tril = jnp.where(causal, 1.0, 0.0).astype(BF16)
    triu = jnp.where(row <= col, 1.0, 0.0).astype(BF16)
    return causal, tril, triu


def _rope_tables_kernel(pos_ref, inv_ref, cf_ref, sa_ref, sb_ref):
    ang = pos_ref[...].astype(F32) * inv_ref[...]
    c = jnp.cos(ang)
    s = jnp.sin(ang)
    j = lax.broadcasted_iota(I32, ang.shape, 1) & (HEAD_DIM - 1)
    cf_ref[...] = jnp.where(j < ROPE_DIM, c, 1.0)
    sa_ref[...] = jnp.where(j < ROPE_HALF, -s, 0.0)
    sb_ref[...] = jnp.where(j < ROPE_HALF, 0.0, jnp.where(j < ROPE_DIM, s, 0.0))


def _rope_tables(positions, S):
    tb = min(S, 1024)
    inv = np.power(np.float32(ROPE_THETA), -np.arange(ROPE_HALF, dtype=np.float32) / np.float32(ROPE_HALF))
    lane = np.arange(LANES) % HEAD_DIM
    inv_lanes = np.where(lane < ROPE_DIM, inv[lane % ROPE_HALF], np.float32(0)).astype(np.float32)[None, :]
    tab = jax.ShapeDtypeStruct((S, LANES), F32)
    row = pl.BlockSpec((tb, LANES), lambda i: (i, 0))
    return pl.pallas_call(
        _rope_tables_kernel,
        grid=(S // tb,),
        in_specs=[pl.BlockSpec((tb, 1), lambda i: (i, 0)), pl.BlockSpec((1, LANES), lambda i: (0, 0))],
        out_specs=[row, row, row],
        out_shape=[tab, tab, tab],
        compiler_params=_params(("arbitrary",)),
        name="rope_tables",
    )(positions.reshape(S, 1), jnp.asarray(inv_lanes))


def _inproj_kernel(x_ref, nw_ref, w_ref, cf_ref, sa_ref, sb_ref,
                   om_ref, omg_ref, os_ref, odt_ref,
                   aqT_ref, iqT_ref, iwT_ref, ik_ref, ak_ref, avT_ref):
    x = x_ref[...]
    ms = jnp.mean(x * x, axis=-1, keepdims=True)
    h = (x * lax.rsqrt(ms + EPS) * nw_ref[...]).astype(BF16)

    def proj(c0, n):
        return jnp.dot(h, w_ref[:, c0:c0 + n], preferred_element_type=F32)

    om_ref[...] = proj(C_M, 4 * D_M)
    os_ref[...] = proj(C_Z, D_S + CONV_DIM)
    omg_ref[...] = proj(C_MG, LANES)
    odt_ref[...] = proj(C_DT, LANES)

    cf = cf_ref[...]
    sa = sa_ref[...]
    sb = sb_ref[...]

    def rope(c, cf=cf, sa=sa, sb=sb):
        return c * cf + pltpu.roll(c, LANES - ROPE_HALF, 1) * sa + pltpu.roll(c, ROPE_HALF, 1) * sb

    def rope2(a2):
        return jnp.concatenate([rope(a2[:, :LANES]), rope(a2[:, LANES:])], axis=1)

    a = proj(C_A, 4 * D_A)
    aqT_ref[...] = (rope2(a[:, 0:D_A]) * (HEAD_DIM ** -0.5 * LOG2E)).T.astype(BF16)
    ak_ref[...] = rope2(a[:, D_A:2 * D_A]).astype(BF16)
    vT = a[:, 2 * D_A:3 * D_A].T.astype(BF16)
    ones = jnp.ones((V_EXT - HEAD_DIM, vT.shape[1]), BF16)
    avT_ref[...] = jnp.concatenate(
        [t for h in range(A_HEADS) for t in (vT[HEAD_DIM * h:HEAD_DIM * (h + 1)], ones)], axis=0)
    iqT_ref[...] = rope2(a[:, 3 * D_A:4 * D_A]).T.astype(BF16)

    ikw = proj(C_IK, LANES)
    is_ik = lax.broadcasted_iota(I32, ikw.shape, 1) < IDX_DIM
    ikr = rope(ikw, jnp.where(is_ik, cf, 1.0), jnp.where(is_ik, sa, 0.0), jnp.where(is_ik, sb, 0.0))
    ik_ref[...] = jnp.where(is_ik, ikr, 0.0).astype(BF16)
    iwT_ref[...] = ikw.T[IDX_DIM:IDX_DIM + SUBLANES, :]


def _inproj(x2, nw, wp, cf, sa, sb, S, tm):
    row = lambda n: pl.BlockSpec((tm, n), lambda i: (i, 0))
    colT = lambda n: pl.BlockSpec((n, tm), lambda i: (0, i))
    f = lambda n, dt=F32: jax.ShapeDtypeStruct((S, n), dt)
    fT = lambda n, dt=BF16: jax.ShapeDtypeStruct((n, S), dt)
    return pl.pallas_call(
        _inproj_kernel,
        grid=(S // tm,),
        in_specs=[row(D_MODEL), pl.BlockSpec((1, D_MODEL), lambda i: (0, 0)),
                  _resident((D_MODEL, N_P), lambda i: (0, 0)),
                  row(LANES), row(LANES), row(LANES)],
        out_specs=[row(4 * D_M), row(LANES), row(D_S + CONV_DIM), row(LANES),
                   colT(D_A), colT(D_A), colT(SUBLANES), row(LANES), row(D_A), colT(A_HEADS * V_EXT)],
        out_shape=[f(4 * D_M), f(LANES), f(D_S + CONV_DIM), f(LANES),
                   fT(D_A), fT(D_A), fT(SUBLANES, F32), f(LANES, BF16), f(D_A, BF16), fT(A_HEADS * V_EXT)],
        compiler_params=_params(("arbitrary",)),
        name="inproj",
    )(x2, nw, wp, cf, sa, sb)


def _mlstm_kernel(om_ref, omg_ref, gb_ref, nw_ref, o_ref, C_ref, n_ref, m_ref):
    L = om_ref.shape[0]

    @pl.when(pl.program_id(0) == 0)
    def _():
        C_ref[...] = jnp.zeros_like(C_ref)
        n_ref[...] = jnp.zeros_like(n_ref)
        m_ref[...] = jnp.zeros_like(m_ref)

    causal, tril, triu = _tri(L)
    G = omg_ref[...] + gb_ref[...]
    LF = jnp.minimum(G, 0.0) - jnp.log1p(jnp.exp(-jnp.abs(G)))
    Bc = _cumsum_cols(tril, LF)
    GT = G.T
    BrT = _cumsum_rows(LF.T, triu)
    nw = nw_ref[...]
    heads = range(M_HEADS)

    def head(c0, h):
        blk = om_ref[:, pl.ds(c0 + LANES * (h // 2), LANES)]
        return blk[:, HEAD_DIM * (h % 2):HEAD_DIM * (h % 2 + 1)]

    qf = [head(0, h) * (HEAD_DIM ** -0.5) for h in heads]
    kf = [head(D_M, h) for h in heads]
    q = [t.astype(BF16) for t in qf]
    k = [t.astype(BF16) for t in kf]
    v = [head(2 * D_M, h).astype(BF16) for h in heads]
    C_prev = [C_ref[h] for h in heads]
    n_prev = [n_ref[h:h + 1, :] for h in heads]
    m_prev = [m_ref[h:h + 1, 0:1] for h in heads]
    qk = [lax.dot_general(q[h], k[h], (((1,), (1,)), ((), ())), preferred_element_type=F32) for h in heads]
    qC = [jnp.dot(q[h], C_prev[h].astype(BF16), preferred_element_type=F32) for h in heads]
    s, scale, m_t, kw, decay = [], [], [], [], []
    for h in heads:
        f_l = M_HEADS + h
        b_col = Bc[:, f_l:f_l + 1]
        b_row = BrT[f_l:f_l + 1, :]
        i_col = G[:, h:h + 1]
        i_row = GT[h:h + 1, :]
        logd = jnp.where(causal, b_col + (i_row - b_row), -jnp.inf)
        m_inter = b_col + m_prev[h]
        mt = jnp.maximum(m_inter, jnp.max(logd, axis=-1, keepdims=True))
        s.append(qk[h] * jnp.exp(logd - mt))
        scale.append(jnp.exp(m_inter - mt))
        m_t.append(mt)
        b_last = Bc[L - 1:L, f_l:f_l + 1]
        m_new = jnp.maximum(b_last + m_prev[h], jnp.max(b_last - b_row + i_row, axis=-1, keepdims=True))
        kw.append(kf[h] * jnp.exp(b_last - b_col + i_col - m_new))
        decay.append(jnp.exp(b_last + m_prev[h] - m_new))
        m_ref[h:h + 1, :] = jnp.broadcast_to(m_new, (1, LANES))
    sv = [jnp.dot(s[h].astype(BF16), v[h], preferred_element_type=F32) for h in heads]
    kv = [lax.dot_general(kw[h].astype(BF16), v[h], (((0,), (0,)), ((), ())), preferred_element_type=F32)
          for h in heads]
    outs = []
    for h in heads:
        num = sv[h] + scale[h] * qC[h]
        den = (jnp.sum(s[h], axis=-1, keepdims=True)
               + scale[h] * jnp.sum(qf[h] * n_prev[h], axis=-1, keepdims=True))
        hh = num / jnp.maximum(jnp.abs(den), jnp.exp(-m_t[h]))
        C_ref[h] = decay[h] * C_prev[h] + kv[h]
        n_ref[h:h + 1, :] = decay[h] * n_prev[h] + jnp.sum(kw[h], axis=0, keepdims=True)
        y = hh * lax.rsqrt(jnp.mean(hh * hh, axis=-1, keepdims=True) + EPS)
        outs.append(jax.nn.sigmoid(head(3 * D_M, h)) * y)
    o_ref[...] = (jnp.concatenate(outs, axis=1) * nw).astype(BF16)


def _mlstm(om, omg, gb, nw, S, L):
    return pl.pallas_call(
        _mlstm_kernel,
        grid=(S // L,),
        in_specs=[pl.BlockSpec((L, 4 * D_M), lambda i: (i, 0)), pl.BlockSpec((L, LANES), lambda i: (i, 0)),
                  pl.BlockSpec((1, LANES), lambda i: (0, 0)), pl.BlockSpec((1, D_M), lambda i: (0, 0))],
        out_specs=pl.BlockSpec((L, D_M), lambda i: (i, 0)),
        out_shape=jax.ShapeDtypeStruct((S, D_M), BF16),
        scratch_shapes=[pltpu.VMEM((M_HEADS, HEAD_DIM, HEAD_DIM), F32),
                        pltpu.VMEM((SUBLANES, HEAD_DIM), F32),
                        pltpu.VMEM((SUBLANES, LANES), F32)],
        compiler_params=_params(("arbitrary",)),
        name="mlstm",
    )(om, omg, gb, nw)


def _ssd_kernel(os_ref, odt_ref, cw_ref, cb_ref, dtb_ref, alog_ref, dsk_ref, nw_ref, o_ref,
                carry_ref, st_ref):
    L = os_ref.shape[0]

    @pl.when(pl.program_id(0) == 0)
    def _():
        carry_ref[...] = jnp.zeros_like(carry_ref)
        st_ref[...] = jnp.zeros_like(st_ref)

    causal, tril, triu = _tri(L)
    z = os_ref[:, 0:D_S]
    raw = os_ref[:, D_S:D_S + CONV_DIM]
    ext = jnp.concatenate([carry_ref[...], raw], axis=0)
    cw = cw_ref[...]
    xbc = cb_ref[...] + cw[S_CONV - 1:S_CONV, :] * raw
    for j in range(S_CONV - 1):
        off = SUBLANES - (S_CONV - 1) + j
        xbc = xbc + cw[j:j + 1, :] * ext[off:off + L, :]
    carry_ref[...] = raw[L - SUBLANES:L, :]
    xbc = xbc * jax.nn.sigmoid(xbc)

    dtr = odt_ref[...] + dtb_ref[...]
    DT = jnp.maximum(dtr, 0.0) + jnp.log1p(jnp.exp(-jnp.abs(dtr)))
    dA = DT * (-jnp.exp(alog_ref[...]))
    Ac = _cumsum_cols(tril, dA)
    ArT = _cumsum_rows(dA.T, triu)
    DTT = DT.T
    hpg = S_HEADS // S_GROUPS
    ys = []
    for g in range(S_GROUPS):
        Bg = xbc[:, D_S + S_STATE * g:D_S + S_STATE * (g + 1)]
        Cg = xbc[:, D_S + S_GROUPS * S_STATE + S_STATE * g:D_S + S_GROUPS * S_STATE + S_STATE * (g + 1)]
        Cb = Cg.astype(BF16)
        CB = lax.dot_general(Cb, Bg.astype(BF16), (((1,), (1,)), ((), ())), preferred_element_type=F32)
        BgT = Bg.T
        for hh in range(hpg):
            h = g * hpg + hh
            xpair = xbc[:, LANES * (h // 2):LANES * (h // 2 + 1)]
            xh = xpair[:, HEAD_DIM * (h % 2):HEAD_DIM * (h % 2 + 1)]
            ac_col = Ac[:, h:h + 1]
            ac_row = ArT[h:h + 1, :]
            dec = jnp.exp(jnp.where(causal, ac_col - ac_row, -jnp.inf))
            sc = (CB * dec).astype(BF16)
            xdt = (xh * DT[:, h:h + 1]).astype(BF16)
            st = st_ref[h]
            y = (jnp.dot(sc, xdt, preferred_element_type=F32)
                 + jnp.dot(Cb, st.astype(BF16), preferred_element_type=F32) * jnp.exp(ac_col))
            a_last = Ac[L - 1:L, h:h + 1]
            w_row = jnp.exp(a_last - ac_row) * DTT[h:h + 1, :]
            st_ref[h] = st * jnp.exp(a_last) + jnp.dot((BgT * w_row).astype(BF16), xh.astype(BF16),
                                                       preferred_element_type=F32)
            ys.append(y)
    Y = jnp.concatenate(ys, axis=1) + dsk_ref[...] * xbc[:, 0:D_S]
    gated = Y * (z * jax.nn.sigmoid(z))
    gw = D_S // S_GROUPS
    outs = []
    for g in range(S_GROUPS):
        gg = gated[:, gw * g:gw * (g + 1)]
        outs.append(gg * lax.rsqrt(jnp.mean(gg * gg, axis=-1, keepdims=True) + EPS))
    o_ref[...] = (jnp.concatenate(outs, axis=1) * nw_ref[...]).astype(BF16)


def _ssd(os_, odt, cw, cb, dtb, alog, dsk, nw, S, L):
    full = lambda r, c: pl.BlockSpec((r, c), lambda i: (0, 0))
    return pl.pallas_call(
        _ssd_kernel,
        grid=(S // L,),
        in_specs=[pl.BlockSpec((L, D_S + CONV_DIM), lambda i: (i, 0)), pl.BlockSpec((L, LANES), lambda i: (i, 0)),
                  full(S_CONV, CONV_DIM), full(1, CONV_DIM), full(1, LANES), full(1, LANES),
                  full(1, D_S), full(1, D_S)],
        out_specs=pl.BlockSpec((L, D_S), lambda i: (i, 0)),
        out_shape=jax.ShapeDtypeStruct((S, D_S), BF16),
        scratch_shapes=[pltpu.VMEM((SUBLANES, CONV_DIM), F32),
                        pltpu.VMEM((S_HEADS, S_STATE, HEAD_DIM), F32)],
        compiler_params=_params(("arbitrary",)),
        name="ssd",
    )(os_, odt, cw, cb, dtb, alog, dsk, nw)


def _max16(a, b):
    return jnp.where(a >= b, a, b)


def _count_ge16(ref, nblk, rb, thr_row, tq):
    thr = jnp.broadcast_to(thr_row.astype(jnp.int16), (PACK, tq))
    n_acc = 4
    one = jnp.ones((PACK, tq), jnp.int16)

    def body(b, accs):
        slab = ref[pl.ds(pl.multiple_of(b * rb, rb), rb), :]
        accs = list(accs)
        for j in range(rb // PACK):
            blk = slab[PACK * j:PACK * (j + 1)]
            a = accs[j % n_acc]
            accs[j % n_acc] = jnp.where(blk >= thr, a + one, a)
        return tuple(accs)

    accs = lax.fori_loop(0, nblk, body, tuple(jnp.zeros((PACK, tq), jnp.int16) for _ in range(n_acc)))
    tot = (accs[0].astype(I32) + accs[1].astype(I32)) + (accs[2].astype(I32) + accs[3].astype(I32))
    return jnp.sum(tot.astype(F32), axis=0, keepdims=True)


def _avg_floor(lo, hi):
    return (lo >> 1) + (hi >> 1) + (lo & hi & 1)


def _dsa_kernel(aqT_ref, iqT_ref, iwT_ref, ik_ref, ak_ref, avT_ref, o_ref, key_ref, k16_ref, gm_ref, acc_ref,
                *, top_k):
    TQ = aqT_ref.shape[1]
    q0 = pl.program_id(0) * TQ
    nb = (q0 + TQ + DSA_KB - 1) // DSA_KB
    qpos = q0 + lax.broadcasted_iota(I32, (1, TQ), 1)
    kf = float(top_k)
    w_scale = (IDX_HEADS ** -0.5) * (IDX_DIM ** -0.5)

    iqT = iqT_ref[...]
    zpad = jnp.zeros((LANES - IDX_DIM, TQ), BF16)
    iq_pad = [jnp.concatenate([iqT[IDX_DIM * h:IDX_DIM * (h + 1)], zpad], axis=0) for h in range(IDX_HEADS)]
    w_rows = [iwT_ref[h:h + 1, :] * w_scale for h in range(IDX_HEADS)]

    gm_ref[pl.ds(pl.multiple_of(nb * DSA_GB, DSA_GB), DSA_GCHUNK - DSA_GB), :] = jnp.full(
        (DSA_GCHUNK - DSA_GB, TQ), I16_MIN, jnp.int16)

    @pl.when(nb % 2 == 1)
    def _():
        key_ref[pl.ds(pl.multiple_of(nb * DSA_KB, DSA_KB), DSA_KB), :] = jnp.full((DSA_KB, TQ), INT_MIN, I32)

    half = DSA_KB // 2

    def p1(step, c, masked, per_step):
        n_half = 2 * per_step
        base = step * (per_step * DSA_KB)
        k0s = [pl.multiple_of(base + s2 * half, half) for s2 in range(n_half)]
        dots = [[jnp.dot(ik_ref[pl.ds(k0s[s2], half), :], iq_pad[h], preferred_element_type=F32)
                 for h in range(IDX_HEADS)] for s2 in range(n_half)]
        gms = []
        for s2 in range(n_half):
            k0 = k0s[s2]
            sc = jnp.zeros((half, TQ), F32)
            for h in range(IDX_HEADS):
                sc = sc + w_rows[h] * jnp.maximum(dots[s2][h], 0.0)
            bits = lax.bitcast_convert_type(sc, I32)
            key = jnp.where(bits < 0, INT_MIN - bits, bits)
            if masked:
                kpos = k0 + lax.broadcasted_iota(I32, (half, 1), 0)
                key = jnp.where(kpos <= qpos, key, INT_MIN)
            key_ref[pl.ds(k0, half), :] = key
            k16 = (key >> 16).astype(jnp.int16)
            k16_ref[pl.ds(k0, half), :] = k16
            span = DSA_GROUP * PACK
            for g in range(half // span):
                m = k16[span * g:span * g + PACK]
                for j in range(1, DSA_GROUP):
                    m = _max16(m, k16[span * g + PACK * j:span * g + PACK * (j + 1)])
                gms.append(m)
        rows = per_step * DSA_GB
        gm_ref[pl.ds(pl.multiple_of(step * rows, rows), rows), :] = jnp.concatenate(gms, axis=0)
        return c

    nfull = q0 // DSA_KB
    lax.fori_loop(0, nfull // 4, functools.partial(p1, masked=False, per_step=4), 0)
    lax.fori_loop(2 * (nfull // 4), nfull // 2, functools.partial(p1, masked=False, per_step=2), 0)
    lax.fori_loop(2 * (nfull // 2), nfull, functools.partial(p1, masked=False, per_step=1), 0)
    lax.fori_loop(nfull, nb, functools.partial(p1, masked=True, per_step=1), 0)

    ngc = (nb * DSA_GB + DSA_GCHUNK - 1) // DSA_GCHUNK

    def gm_bit(it, prefix):
        cand = prefix | jnp.left_shift(jnp.int32(1), 15 - it)
        cnt = _count_ge16(gm_ref, ngc, DSA_GCHUNK, cand + I16_MIN, TQ)
        return jnp.where(cnt >= kf, cand, prefix)

    lo_h = lax.fori_loop(0, 16, gm_bit, jnp.zeros((1, TQ), I32)) + I16_MIN

    def gm_max(c, m):
        r0 = pl.multiple_of(c * DSA_GCHUNK, DSA_GCHUNK)
        for j in range(DSA_GCHUNK // PACK):
            m = _max16(m, gm_ref[pl.ds(r0 + PACK * j, PACK), :])
        return m

    gmax = lax.fori_loop(0, ngc, gm_max, jnp.full((PACK, TQ), I16_MIN, jnp.int16))
    hi_h = jnp.max(gmax.astype(I32), axis=0, keepdims=True) + 1

    def halve(lo_0, hi_0, want, first_probe, n_free):
        def step(it, lo, hi, chi):
            probe = jnp.where(it < 1, first_probe, INT_MIN)
            mid = jnp.where((lo < probe) & (probe < hi), probe, _avg_floor(lo, hi))
            c = _count_ge16(k16_ref, nb, DSA_KB, mid, TQ)
            ge = c >= want
            ex = jnp.logical_and(c == want, mid != lo)
            lo2 = jnp.where(ge, mid, lo)
            hi2 = jnp.where(ex, mid + 1, jnp.where(ge, hi, mid))
            chi2 = jnp.where(ex, EXACT, jnp.where(ge, chi, c))
            return lo2, hi2, chi2

        def active(lo, hi):
            return jnp.max(jnp.where(_avg_floor(lo, hi) != lo, 1.0, 0.0))

        def w_cond(st):
            return jnp.logical_and(st[0] < DSA_MAX_STEPS, st[4] > 0.0)

        def w_body(st):
            lo2, hi2, chi2 = step(st[0], st[1], st[2], st[3])
            return st[0] + 1, lo2, hi2, chi2, active(lo2, hi2)

        lo, hi, chi = lax.fori_loop(0, n_free, lambda it, st: step(it, *st),
                                    (lo_0, hi_0, jnp.zeros((1, TQ), F32)))
        _, lo, _, chi, _ = lax.while_loop(w_cond, w_body, (jnp.int32(n_free), lo, hi, chi, active(lo, hi)))
        return lo, chi

    tau_h, chi_h = halve(lo_h, hi_h, kf, INT_MIN, DSA_FREE_STEPS)
    exact_h = chi_h == EXACT
    want_l = kf - chi_h

    tau_h16 = tau_h.astype(jnp.int16)
    floor16 = jnp.full((DSA_KB, TQ), I16_MIN, jnp.int16)

    def build(b, c):
        k0 = pl.multiple_of(b * DSA_KB, DSA_KB)
        low = ((key_ref[pl.ds(k0, DSA_KB), :] & 0xFFFF) + I16_MIN).astype(jnp.int16)
        k16_ref[pl.ds(k0, DSA_KB), :] = jnp.where(k16_ref[pl.ds(k0, DSA_KB), :] == tau_h16, low, floor16)
        return c

    lax.fori_loop(0, nb, build, 0)
    tau_l, chi_l = halve(jnp.where(exact_h, 0, I16_MIN), jnp.where(exact_h, 1, -I16_MIN), want_l, I16_MIN + 1,
                         DSA_FREE_STEPS)
    tau = jnp.where(exact_h, tau_h << 16, (tau_h << 16) + (tau_l - I16_MIN))
    r = jnp.where(tau == INT_MIN, 0.0,
                  jnp.where(jnp.logical_or(exact_h, chi_l == EXACT), TIE_ALL, want_l - chi_l))

    AB = DSA_AB
    rowi = lax.broadcasted_iota(I32, (AB, AB), 0)
    coli = lax.broadcasted_iota(I32, (AB, AB), 1)
    tril = jnp.where(coli <= rowi, 1.0, 0.0).astype(BF16)
    aqT = aqT_ref[...]
    hrow = lax.broadcasted_iota(I32, (LANES, TQ), 0) // HEAD_DIM
    q_pad = []
    for h in range(A_HEADS):
        pair = aqT[LANES * (h // 2):LANES * (h // 2 + 1)]
        q_pad.append(jnp.where(hrow == (h % 2), pair, jnp.zeros_like(pair)))
    acc_ref[...] = jnp.zeros_like(acc_ref)

    nsub = 2 * DSA_KB // AB

    def p3(b, carry, speculate):
        cnt, ms = carry
        k0 = pl.multiple_of(b * 2 * DSA_KB, 2 * DSA_KB)
        bias = []
        for s2 in range(nsub):
            key = key_ref[pl.ds(k0 + s2 * AB, AB), :]
            eq = key == tau
            pref = jnp.dot(tril, jnp.where(eq, 1.0, 0.0).astype(BF16), preferred_element_type=F32) + cnt
            sel = jnp.where(key > tau, 0.0, jnp.where(eq, pref, 1e9)) <= r
            bias.append(jnp.where(sel, 0.0, NEG_BIG))
            cnt = pref[AB - 1:AB, :]

        def logits(h):
            return [jnp.dot(ak_ref[pl.ds(k0 + s2 * AB, AB), pl.ds(LANES * (h // 2), LANES)], q_pad[h],
                            preferred_element_type=F32) + bias[s2] for s2 in range(nsub)]

        def block_max(lm):
            return functools.reduce(jnp.maximum, [jnp.max(t, axis=0, keepdims=True) for t in lm])

        def weights(lm, m):
            return jnp.concatenate([jnp.exp2(t - m).astype(BF16) for t in lm], axis=0)

        def values(h):
            return avT_ref[pl.ds(V_EXT * h, V_EXT), pl.ds(k0, 2 * DSA_KB)]

        def exact(ms2, lms=None):
            for h in range(A_HEADS):
                hs = pl.ds(V_EXT * h, V_EXT)
                lm = logits(h) if lms is None else lms[h]
                pv = jnp.dot(values(h), weights(lm, ms2[h]), preferred_element_type=F32)
                acc_ref[hs, :] = acc_ref[hs, :] * jnp.exp2(ms[h] - ms2[h]) + pv

        lms = [logits(h) for h in range(A_HEADS)]
        if not speculate:
            ms2 = [jnp.maximum(ms[h], block_max(lms[h])) for h in range(A_HEADS)]
            exact(ms2, lms)
            return cnt, tuple(ms2)

        ps, ms2 = [], []
        for h in range(A_HEADS):
            ps.append(weights(lms[h], ms[h]))
            ms2.append(jnp.maximum(ms[h], block_max(lms[h])))
        pvs = [jnp.dot(values(h), ps[h], preferred_element_type=F32) for h in range(A_HEADS)]
        rise = jnp.max(functools.reduce(jnp.maximum, [ms2[h] - ms[h] for h in range(A_HEADS)]))

        @pl.when(rise <= EXP2_HEADROOM)
        def _():
            for h in range(A_HEADS):
                hs = pl.ds(V_EXT * h, V_EXT)
                acc_ref[hs, :] = (acc_ref[hs, :] + pvs[h]) * jnp.exp2(ms[h] - ms2[h])

        @pl.when(rise > EXP2_HEADROOM)
        def _():
            exact(ms2)

        return cnt, tuple(ms2)

    init = (jnp.zeros((1, TQ), F32), tuple(jnp.full((1, TQ), M_FLOOR, F32) for _ in range(A_HEADS)))
    first = p3(0, init, speculate=False)
    lax.fori_loop(1, (nb + 1) // 2, functools.partial(p3, speculate=True), first)
    outT = jnp.concatenate([acc_ref[pl.ds(V_EXT * h, HEAD_DIM), :] / acc_ref[pl.ds(V_EXT * h + HEAD_DIM, 1), :]
                            for h in range(A_HEADS)], axis=0)
    o_ref[...] = outT.T.astype(BF16)


def _dsa(aqT, iqT, iwT, ik, ak, avT, S, tq):
    top_k = min(TOPK_MAX, S // 4)
    colT = lambda n: pl.BlockSpec((n, tq), lambda i: (0, i))
    return pl.pallas_call(
        functools.partial(_dsa_kernel, top_k=top_k),
        grid=(S // tq,),
        in_specs=[colT(D_A), colT(D_A), colT(SUBLANES),
                  _resident((S, LANES), lambda i: (0, 0)),
                  _resident((S, D_A), lambda i: (0, 0)),
                  _resident((A_HEADS * V_EXT, S), lambda i: (0, 0))],
        out_specs=pl.BlockSpec((tq, D_A), lambda i: (i, 0)),
        out_shape=jax.ShapeDtypeStruct((S, D_A), BF16),
        scratch_shapes=[pltpu.VMEM((S, tq), I32),
                        pltpu.VMEM((S, tq), jnp.int16),
                        pltpu.VMEM((S // DSA_GROUP + DSA_GCHUNK, tq), jnp.int16),
                        pltpu.VMEM((A_HEADS * V_EXT, tq), F32)],
        compiler_params=_params(("arbitrary",)),
        name="dsa",
    )(aqT, iqT, iwT, ik, ak, avT)


FF_CHUNKS = ((0, 768), (768, 768), (1536, 768), (2304, 512))


def _rms(v, w):
    return v * lax.rsqrt(jnp.mean(v * v, axis=-1, keepdims=True) + EPS) * w


def _outffn_kernel(x_ref, hm_ref, ha_ref, hs_ref, nw_ref, wo_ref, wg_ref, wu_ref, wd_ref, o_ref):
    mix = (jnp.dot(hm_ref[...], wo_ref[0:D_M, :], preferred_element_type=F32)
           + jnp.dot(ha_ref[...], wo_ref[D_M:D_M + D_A, :], preferred_element_type=F32)
           + jnp.dot(hs_ref[...], wo_ref[D_M + D_A:, :], preferred_element_type=F32))
    x1 = x_ref[...] + _rms(mix, nw_ref[1:2, :])
    h2 = _rms(x1, nw_ref[2:3, :]).astype(BF16)
    ff = jnp.zeros_like(x1)
    for c0, n in FF_CHUNKS:
        g = jnp.dot(h2, wg_ref[:, c0:c0 + n], preferred_element_type=F32)
        u = jnp.dot(h2, wu_ref[:, c0:c0 + n], preferred_element_type=F32)
        act = (g * jax.nn.sigmoid(g) * u).astype(BF16)
        ff = ff + jnp.dot(act, wd_ref[c0:c0 + n, :], preferred_element_type=F32)
    o_ref[...] = x1 + _rms(ff, nw_ref[3:4, :])


def _outffn(x2, hm, ha, hs, nw4, wo, wg, wu, wd, S, tm):
    row = lambda n: pl.BlockSpec((tm, n), lambda i: (i, 0))
    return pl.pallas_call(
        _outffn_kernel,
        grid=(S // tm,),
        in_specs=[row(D_MODEL), row(D_M), row(D_A), row(D_S),
                  pl.BlockSpec((4, D_MODEL), lambda i: (0, 0)),
                  _resident((D_MODEL, D_MODEL), lambda i: (0, 0)),
                  _resident((D_MODEL, D_FF), lambda i: (0, 0)),
                  _resident((D_MODEL, D_FF), lambda i: (0, 0)),
                  _resident((D_FF, D_MODEL), lambda i: (0, 0))],
        out_specs=row(D_MODEL),
        out_shape=jax.ShapeDtypeStruct((S, D_MODEL), F32),
        compiler_params=_params(("arbitrary",)),
        name="outffn",
    )(x2, hm, ha, hs, nw4, wo, wg, wu, wd)


def _plan(S):
    assert S % (2 * DSA_KB) == 0
    return dict(tm=min(S, 512), lm=min(S, 256), ls=min(S, 256), tq=min(S, 256))


def _pad_lanes(v):
    return jnp.pad(v, [(0, 0)] * (v.ndim - 1) + [(0, LANES - v.shape[-1])])


def kernel(x, positions, norm_w, w_in, mlstm_gate_bias, mlstm_norm_w, conv_w, conv_b, dt_bias, a_log,
           d_skip, ssd_norm_w, w_out, w_gate, w_up, w_down):
    B, S, D = x.shape
    assert B == 1 and D == D_MODEL
    depth = w_in.shape[0]
    plan = _plan(S)

    o = np.cumsum([0, D_M, D_M, D_M, D_M, M_HEADS, M_HEADS, D_A, D_A, D_A, IDX_HEADS * IDX_DIM, IDX_DIM,
                   IDX_HEADS, D_S, CONV_DIM, S_HEADS])
    wp = jnp.concatenate([
        w_in[:, :, o[0]:o[4]], w_in[:, :, o[6]:o[10]], w_in[:, :, o[12]:o[13]], w_in[:, :, o[13]:o[14]],
        _pad_lanes(w_in[:, :, o[4]:o[6]]), _pad_lanes(w_in[:, :, o[10]:o[12]]), _pad_lanes(w_in[:, :, o[14]:o[15]]),
    ], axis=-1).astype(BF16)
    wo = w_out.astype(BF16)
    wg = w_gate.astype(BF16)
    wu = w_up.astype(BF16)
    wd = w_down.astype(BF16)
    gb = _pad_lanes(mlstm_gate_bias[:, None, :])
    dtb = _pad_lanes(dt_bias[:, None, :])
    alog = _pad_lanes(a_log[:, None, :])
    dsk = jnp.repeat(d_skip, HEAD_DIM, axis=-1)[:, None, :]

    cf, sa, sb = _rope_tables(positions.astype(I32), S)
    x2 = x.reshape(S, D)
    for l in range(depth):
        om, omg, os_, odt, aqT, iqT, iwT, ik, ak, avT = _inproj(
            x2, norm_w[l, 0:1], wp[l], cf, sa, sb, S, plan["tm"])
        hm = _mlstm(om, omg, gb[l], mlstm_norm_w[l][None, :], S, plan["lm"])
        hs = _ssd(os_, odt, conv_w[l], conv_b[l][None, :], dtb[l], alog[l], dsk[l], ssd_norm_w[l][None, :],
                  S, plan["ls"])
        ha = _dsa(aqT, iqT, iwT, ik, ak, avT, S, plan["tq"])
        x2 = _outffn(x2, hm, ha, hs, norm_w[l], wo[l], wg[l], wu[l], wd[l], S, plan["tm"])
    return x2.reshape(B, S, D)
```

```python
import functools

import numpy as np
import jax
import jax.numpy as jnp
from jax import lax
from jax.experimental import pallas as pl
from jax.experimental.pallas import tpu as pltpu

F32 = jnp.float32
BF16 = jnp.bfloat16
I32 = jnp.int32

D_MODEL = 1024
HEAD_DIM = 64
D_M = 256
M_HEADS = 4
D_A = 256
A_HEADS = 4
IDX_HEADS = 4
IDX_DIM = 64
TOPK_MAX = 256
D_S = 512
S_HEADS = 8
S_GROUPS = 2
S_STATE = 128
S_CONV = 4
CONV_DIM = D_S + 2 * S_GROUPS * S_STATE
ROPE_THETA = 500000.0
ROPE_DIM = HEAD_DIM // 4
ROPE_HALF = ROPE_DIM // 2
D_FF = 2816
EPS = 1e-6

LANES = 128
SUBLANES = 8
PACK = 16
VMEM_LIMIT = 60 * 1024 * 1024

C_M = 0
C_A = C_M + 4 * D_M
C_Z = C_A + 4 * D_A
C_XBC = C_Z + D_S
C_MG = C_XBC + CONV_DIM
C_IK = C_MG + LANES
C_DT = C_IK + LANES
N_P = C_DT + LANES

INT_MIN = -2 ** 31
NEG_BIG = -1e30
M_FLOOR = -1e29
LOG2E = 1.4426950408889634
EXP2_HEADROOM = 64.0

DSA_KB = 512
DSA_AB = 256
DSA_GROUP = 16
DSA_GB = DSA_KB // DSA_GROUP
DSA_GCHUNK = 128
V_EXT = HEAD_DIM + PACK
DIGIT_BITS = 16
LOW_MASK = 2 ** DIGIT_BITS - 1
I16_MIN = -2 ** (DIGIT_BITS - 1)
DSA_MAX_STEPS = DIGIT_BITS + 8
DSA_FREE_STEPS = 8
TIE_ALL = 1e6
NOT_TIED = 1e9
EXACT = -1e9


def _params(sem):
    return pltpu.CompilerParams(dimension_semantics=sem, vmem_limit_bytes=VMEM_LIMIT)


def _resident(shape, index_map):
    return pl.BlockSpec(shape, index_map, pipeline_mode=pl.Buffered(1))


def _split3(x):
    h = x.astype(BF16)
    r = x - h.astype(F32)
    m = r.astype(BF16)
    lo = (r - m.astype(F32)).astype(BF16)
    return h, m, lo


def _cumsum_cols(tril, x):
    return sum(jnp.dot(tril, t, preferred_element_type=F32) for t in _split3(x))


def _cumsum_rows(x, triu):
    return sum(jnp.dot(t, triu, preferred_element_type=F32) for t in _split3(x))


def _tri(L):
    row = lax.broadcasted_iota(I32, (L, L), 0)
    col = lax.broadcasted_iota(I32, (L, L), 1)
    causal = col <= row
    tril = jnp.where(causal, 1.0, 0.0).astype(BF16)
    triu = jnp.where(row <= col, 1.0, 0.0).astype(BF16)
    return causal, tril, triu


def _rope_tables_kernel(pos_ref, inv_ref, cf_ref, sa_ref, sb_ref):
    ang = pos_ref[...].astype(F32) * inv_ref[...]
    c = jnp.cos(ang)
    s = jnp.sin(ang)
    j = lax.broadcasted_iota(I32, ang.shape, 1) & (HEAD_DIM - 1)
    cf_ref[...] = jnp.where(j < ROPE_DIM, c, 1.0)
    sa_ref[...] = jnp.where(j < ROPE_HALF, -s, 0.0)
    sb_ref[...] = jnp.where(j < ROPE_HALF, 0.0, jnp.where(j < ROPE_DIM, s, 0.0))


def _rope_tables(positions, S):
    tb = min(S, 1024)
    inv = np.power(np.float32(ROPE_THETA), -np.arange(ROPE_HALF, dtype=np.float32) / np.float32(ROPE_HALF))
    lane = np.arange(LANES) % HEAD_DIM
    inv_lanes = np.where(lane < ROPE_DIM, inv[lane % ROPE_HALF], np.float32(0)).astype(np.float32)[None, :]
    tab = jax.ShapeDtypeStruct((S, LANES), F32)
    row = pl.BlockSpec((tb, LANES), lambda i: (i, 0))
    return pl.pallas_call(
        _rope_tables_kernel,
        grid=(S // tb,),
        in_specs=[pl.BlockSpec((tb, 1), lambda i: (i, 0)), pl.BlockSpec((1, LANES), lambda i: (0, 0))],
        out_specs=[row, row, row],
        out_shape=[tab, tab, tab],
        compiler_params=_params(("arbitrary",)),
        name="rope_tables",
    )(positions.reshape(S, 1), jnp.asarray(inv_lanes))


def _inproj_kernel(x_ref, nw_ref, w_ref, cf_ref, sa_ref, sb_ref,
                   om_ref, omg_ref, os_ref, odt_ref,
                   aqT_ref, iqT_ref, iwT_ref, ik_ref, ak_ref, avT_ref):
    x = x_ref[...]
    ms = jnp.mean(x * x, axis=-1, keepdims=True)
    h = (x * lax.rsqrt(ms + EPS) * nw_ref[...]).astype(BF16)

    def proj(c0, n):
        return jnp.dot(h, w_ref[:, c0:c0 + n], preferred_element_type=F32)

    om_ref[...] = proj(C_M, 4 * D_M)
    os_ref[...] = proj(C_Z, D_S + CONV_DIM)
    omg_ref[...] = proj(C_MG, LANES)
    odt_ref[...] = proj(C_DT, LANES)

    cf = cf_ref[...]
    sa = sa_ref[...]
    sb = sb_ref[...]

    def rope(c, cf=cf, sa=sa, sb=sb):
        return c * cf + pltpu.roll(c, LANES - ROPE_HALF, 1) * sa + pltpu.roll(c, ROPE_HALF, 1) * sb

    def rope2(a2):
        return jnp.concatenate([rope(a2[:, :LANES]), rope(a2[:, LANES:])], axis=1)

    a = proj(C_A, 4 * D_A)
    aqT_ref[...] = (rope2(a[:, 0:D_A]) * (HEAD_DIM ** -0.5 * LOG2E)).T.astype(BF16)
    ak_ref[...] = rope2(a[:, D_A:2 * D_A]).astype(BF16)
    vT = a[:, 2 * D_A:3 * D_A].T.astype(BF16)
    ones = jnp.ones((V_EXT - HEAD_DIM, vT.shape[1]), BF16)
    avT_ref[...] = jnp.concatenate(
        [t for h in range(A_HEADS) for t in (vT[HEAD_DIM * h:HEAD_DIM * (h + 1)], ones)], axis=0)
    iqT_ref[...] = rope2(a[:, 3 * D_A:4 * D_A]).T.astype(BF16)

    ikw = proj(C_IK, LANES)
    is_ik = lax.broadcasted_iota(I32, ikw.shape, 1) < IDX_DIM
    ikr = rope(ikw, jnp.where(is_ik, cf, 1.0), jnp.where(is_ik, sa, 0.0), jnp.where(is_ik, sb, 0.0))
    ik_ref[...] = jnp.where(is_ik, ikr, 0.0).astype(BF16)
    iwT_ref[...] = ikw.T[IDX_DIM:IDX_DIM + SUBLANES, :]


def _inproj(x2, nw, wp, cf, sa, sb, S, tm):
    row = lambda n: pl.BlockSpec((tm, n), lambda i: (i, 0))
    colT = lambda n: pl.BlockSpec((n, tm), lambda i: (0, i))
    f = lambda n, dt=F32: jax.ShapeDtypeStruct((S, n), dt)
    fT = lambda n, dt=BF16: jax.ShapeDtypeStruct((n, S), dt)
    return pl.pallas_call(
        _inproj_kernel,
        grid=(S // tm,),
        in_specs=[row(D_MODEL), pl.BlockSpec((1, D_MODEL), lambda i: (0, 0)),
                  _resident((D_MODEL, N_P), lambda i: (0, 0)),
                  row(LANES), row(LANES), row(LANES)],
        out_specs=[row(4 * D_M), row(LANES), row(D_S + CONV_DIM), row(LANES),
                   colT(D_A), colT(D_A), colT(SUBLANES), row(LANES), row(D_A), colT(A_HEADS * V_EXT)],
        out_shape=[f(4 * D_M), f(LANES), f(D_S + CONV_DIM), f(LANES),
                   fT(D_A), fT(D_A), fT(SUBLANES, F32), f(LANES, BF16), f(D_A, BF16), fT(A_HEADS * V_EXT)],
        compiler_params=_params(("arbitrary",)),
        name="inproj",
    )(x2, nw, wp, cf, sa, sb)


def _mlstm_kernel(om_ref, omg_ref, gb_ref, nw_ref, o_ref, C_ref, n_ref, m_ref):
    L = om_ref.shape[0]

    @pl.when(pl.program_id(0) == 0)
    def _():
        C_ref[...] = jnp.zeros_like(C_ref)
        n_ref[...] = jnp.zeros_like(n_ref)
        m_ref[...] = jnp.zeros_like(m_ref)

    causal, tril, triu = _tri(L)
    G = omg_ref[...] + gb_ref[...]
    LF = jnp.minimum(G, 0.0) - jnp.log1p(jnp.exp(-jnp.abs(G)))
    Bc = _cumsum_cols(tril, LF)
    GT = G.T
    BrT = _cumsum_rows(LF.T, triu)
    nw = nw_ref[...]
    heads = range(M_HEADS)

    def head(c0, h):
        blk = om_ref[:, pl.ds(c0 + LANES * (h // 2), LANES)]
        return blk[:, HEAD_DIM * (h % 2):HEAD_DIM * (h % 2 + 1)]

    qf = [head(0, h) * (HEAD_DIM ** -0.5) for h in heads]
    kf = [head(D_M, h) for h in heads]
    q = [t.astype(BF16) for t in qf]
    k = [t.astype(BF16) for t in kf]
    v = [head(2 * D_M, h).astype(BF16) for h in heads]
    C_prev = [C_ref[h] for h in heads]
    n_prev = [n_ref[h:h + 1, :] for h in heads]
    m_prev = [m_ref[h:h + 1, 0:1] for h in heads]
    qk = [lax.dot_general(q[h], k[h], (((1,), (1,)), ((), ())), preferred_element_type=F32) for h in heads]
    qC = [jnp.dot(q[h], C_prev[h].astype(BF16), preferred_element_type=F32) for h in heads]
    s, scale, m_t, kw, decay = [], [], [], [], []
    for h in heads:
        f_l = M_HEADS + h
        b_col = Bc[:, f_l:f_l + 1]
        b_row = BrT[f_l:f_l + 1, :]
        i_col = G[:, h:h + 1]
        i_row = GT[h:h + 1, :]
        logd = jnp.where(causal, b_col + (i_row - b_row), -jnp.inf)
        m_inter = b_col + m_prev[h]
        mt = jnp.maximum(m_inter, jnp.max(logd, axis=-1, keepdims=True))
        s.append(qk[h] * jnp.exp(logd - mt))
        scale.append(jnp.exp(m_inter - mt))
        m_t.append(mt)
        b_last = Bc[L - 1:L, f_l:f_l + 1]
        m_new = jnp.maximum(b_last + m_prev[h], jnp.max(b_last - b_row + i_row, axis=-1, keepdims=True))
        kw.append(kf[h] * jnp.exp(b_last - b_col + i_col - m_new))
        decay.append(jnp.exp(b_last + m_prev[h] - m_new))
        m_ref[h:h + 1, :] = jnp.broadcast_to(m_new, (1, LANES))
    sv = [jnp.dot(s[h].astype(BF16), v[h], preferred_element_type=F32) for h in heads]
    kv = [lax.dot_general(kw[h].astype(BF16), v[h], (((0,), (0,)), ((), ())), preferred_element_type=F32)
          for h in heads]
    outs = []
    for h in heads:
        num = sv[h] + scale[h] * qC[h]
        den = (jnp.sum(s[h], axis=-1, keepdims=True)
               + scale[h] * jnp.sum(qf[h] * n_prev[h], axis=-1, keepdims=True))
        hh = num / jnp.maximum(jnp.abs(den), jnp.exp(-m_t[h]))
        C_ref[h] = decay[h] * C_prev[h] + kv[h]
        n_ref[h:h + 1, :] = decay[h] * n_prev[h] + jnp.sum(kw[h], axis=0, keepdims=True)
        y = hh * lax.rsqrt(jnp.mean(hh * hh, axis=-1, keepdims=True) + EPS)
        outs.append(jax.nn.sigmoid(head(3 * D_M, h)) * y)
    o_ref[...] = (jnp.concatenate(outs, axis=1) * nw).astype(BF16)


def _mlstm(om, omg, gb, nw, S, L):
    return pl.pallas_call(
        _mlstm_kernel,
        grid=(S // L,),
        in_specs=[pl.BlockSpec((L, 4 * D_M), lambda i: (i, 0)), pl.BlockSpec((L, LANES), lambda i: (i, 0)),
                  pl.BlockSpec((1, LANES), lambda i: (0, 0)), pl.BlockSpec((1, D_M), lambda i: (0, 0))],
        out_specs=pl.BlockSpec((L, D_M), lambda i: (i, 0)),
        out_shape=jax.ShapeDtypeStruct((S, D_M), BF16),
        scratch_shapes=[pltpu.VMEM((M_HEADS, HEAD_DIM, HEAD_DIM), F32),
                        pltpu.VMEM((SUBLANES, HEAD_DIM), F32),
                        pltpu.VMEM((SUBLANES, LANES), F32)],
        compiler_params=_params(("arbitrary",)),
        name="mlstm",
    )(om, omg, gb, nw)


def _ssd_kernel(os_ref, odt_ref, cw_ref, cb_ref, dtb_ref, alog_ref, dsk_ref, nw_ref, o_ref,
                carry_ref, st_ref):
    L = os_ref.shape[0]

    @pl.when(pl.program_id(0) == 0)
    def _():
        carry_ref[...] = jnp.zeros_like(carry_ref)
        st_ref[...] = jnp.zeros_like(st_ref)

    causal, tril, triu = _tri(L)
    z = os_ref[:, 0:D_S]
    raw = os_ref[:, D_S:D_S + CONV_DIM]
    ext = jnp.concatenate([carry_ref[...], raw], axis=0)
    cw = cw_ref[...]
    xbc = cb_ref[...] + cw[S_CONV - 1:S_CONV, :] * raw
    for j in range(S_CONV - 1):
        off = SUBLANES - (S_CONV - 1) + j
        xbc = xbc + cw[j:j + 1, :] * ext[off:off + L, :]
    carry_ref[...] = raw[L - SUBLANES:L, :]
    xbc = xbc * jax.nn.sigmoid(xbc)

    dtr = odt_ref[...] + dtb_ref[...]
    DT = jnp.maximum(dtr, 0.0) + jnp.log1p(jnp.exp(-jnp.abs(dtr)))
    dA = DT * (-jnp.exp(alog_ref[...]))
    Ac = _cumsum_cols(tril, dA)
    ArT = _cumsum_rows(dA.T, triu)
    DTT = DT.T
    hpg = S_HEADS // S_GROUPS
    ys = []
    for g in range(S_GROUPS):
        Bg = xbc[:, D_S + S_STATE * g:D_S + S_STATE * (g + 1)]
        Cg = xbc[:, D_S + S_GROUPS * S_STATE + S_STATE * g:D_S + S_GROUPS * S_STATE + S_STATE * (g + 1)]
        Cb = Cg.astype(BF16)
        CB = lax.dot_general(Cb, Bg.astype(BF16), (((1,), (1,)), ((), ())), preferred_element_type=F32)
        BgT = Bg.T
        for hh in range(hpg):
            h = g * hpg + hh
            xpair = xbc[:, LANES * (h // 2):LANES * (h // 2 + 1)]
            xh = xpair[:, HEAD_DIM * (h % 2):HEAD_DIM * (h % 2 + 1)]
            ac_col = Ac[:, h:h + 1]
            ac_row = ArT[h:h + 1, :]
            dec = jnp.exp(jnp.where(causal, ac_col - ac_row, -jnp.inf))
            sc = (CB * dec).astype(BF16)
            xdt = (xh * DT[:, h:h + 1]).astype(BF16)
            st = st_ref[h]
            y = (jnp.dot(sc, xdt, preferred_element_type=F32)
                 + jnp.dot(Cb, st.astype(BF16), preferred_element_type=F32) * jnp.exp(ac_col))
            a_last = Ac[L - 1:L, h:h + 1]
            w_row = jnp.exp(a_last - ac_row) * DTT[h:h + 1, :]
            st_ref[h] = st * jnp.exp(a_last) + jnp.dot((BgT * w_row).astype(BF16), xh.astype(BF16),
                                                       preferred_element_type=F32)
            ys.append(y)
    Y = jnp.concatenate(ys, axis=1) + dsk_ref[...] * xbc[:, 0:D_S]
    gated = Y * (z * jax.nn.sigmoid(z))
    gw = D_S // S_GROUPS
    outs = []
    for g in range(S_GROUPS):
        gg = gated[:, gw * g:gw * (g + 1)]
        outs.append(gg * lax.rsqrt(jnp.mean(gg * gg, axis=-1, keepdims=True) + EPS))
    o_ref[...] = (jnp.concatenate(outs, axis=1) * nw_ref[...]).astype(BF16)


def _ssd(os_, odt, cw, cb, dtb, alog, dsk, nw, S, L):
    full = lambda r, c: pl.BlockSpec((r, c), lambda i: (0, 0))
    return pl.pallas_call(
        _ssd_kernel,
        grid=(S // L,),
        in_specs=[pl.BlockSpec((L, D_S + CONV_DIM), lambda i: (i, 0)), pl.BlockSpec((L, LANES), lambda i: (i, 0)),
                  full(S_CONV, CONV_DIM), full(1, CONV_DIM), full(1, LANES), full(1, LANES),
                  full(1, D_S), full(1, D_S)],
        out_specs=pl.BlockSpec((L, D_S), lambda i: (i, 0)),
        out_shape=jax.ShapeDtypeStruct((S, D_S), BF16),
        scratch_shapes=[pltpu.VMEM((SUBLANES, CONV_DIM), F32),
                        pltpu.VMEM((S_HEADS, S_STATE, HEAD_DIM), F32)],
        compiler_params=_params(("arbitrary",)),
        name="ssd",
    )(os_, odt, cw, cb, dtb, alog, dsk, nw)


def _max16(a, b):
    return jnp.where(a >= b, a, b)


def _count_ge16(ref, nblk, rb, thr_row, tq):
    thr = jnp.broadcast_to(thr_row.astype(jnp.int16), (PACK, tq))
    n_acc = 4
    one = jnp.ones((PACK, tq), jnp.int16)

    def body(b, accs):
        slab = ref[pl.ds(pl.multiple_of(b * rb, rb), rb), :]
        accs = list(accs)
        for j in range(rb // PACK):
            blk = slab[PACK * j:PACK * (j + 1)]
            a = accs[j % n_acc]
            accs[j % n_acc] = jnp.where(blk >= thr, a + one, a)
        return tuple(accs)

    accs = lax.fori_loop(0, nblk, body, tuple(jnp.zeros((PACK, tq), jnp.int16) for _ in range(n_acc)))
    tot = (accs[0].astype(I32) + accs[1].astype(I32)) + (accs[2].astype(I32) + accs[3].astype(I32))
    return jnp.sum(tot.astype(F32), axis=0, keepdims=True)


def _avg_floor(lo, hi):
    return (lo >> 1) + (hi >> 1) + (lo & hi & 1)


def _dsa_kernel(aqT_ref, iqT_ref, iwT_ref, ik_ref, ak_ref, avT_ref, o_ref, key_ref, k16_ref, gm_ref, acc_ref,
                *, top_k):
    TQ = aqT_ref.shape[1]
    q0 = pl.program_id(0) * TQ
    nb = (q0 + TQ + DSA_KB - 1) // DSA_KB
    qpos = q0 + lax.broadcasted_iota(I32, (1, TQ), 1)
    kf = float(top_k)
    w_scale = (IDX_HEADS ** -0.5) * (IDX_DIM ** -0.5)

    iqT = iqT_ref[...]
    zpad = jnp.zeros((LANES - IDX_DIM, TQ), BF16)
    iq_pad = [jnp.concatenate([iqT[IDX_DIM * h:IDX_DIM * (h + 1)], zpad], axis=0) for h in range(IDX_HEADS)]
    w_rows = [iwT_ref[h:h + 1, :] * w_scale for h in range(IDX_HEADS)]

    gm_ref[pl.ds(pl.multiple_of(nb * DSA_GB, DSA_GB), DSA_GCHUNK - DSA_GB), :] = jnp.full(
        (DSA_GCHUNK - DSA_GB, TQ), I16_MIN, jnp.int16)

    @pl.when(nb % 2 == 1)
    def _():
        key_ref[pl.ds(pl.multiple_of(nb * DSA_KB, DSA_KB), DSA_KB), :] = jnp.full((DSA_KB, TQ), INT_MIN, I32)

    half = DSA_KB // 2

    def p1(step, c, masked, per_step):
        n_half = 2 * per_step
        base = step * (per_step * DSA_KB)
        k0s = [pl.multiple_of(base + s2 * half, half) for s2 in range(n_half)]
        dots = [[jnp.dot(ik_ref[pl.ds(k0s[s2], half), :], iq_pad[h], preferred_element_type=F32)
                 for h in range(IDX_HEADS)] for s2 in range(n_half)]
        gms = []
        for s2 in range(n_half):
            k0 = k0s[s2]
            sc = jnp.zeros((half, TQ), F32)
            for h in range(IDX_HEADS):
                sc = sc + w_rows[h] * jnp.maximum(dots[s2][h], 0.0)
            bits = lax.bitcast_convert_type(sc, I32)
            key = jnp.where(bits < 0, INT_MIN - bits, bits)
            if masked:
                kpos = k0 + lax.broadcasted_iota(I32, (half, 1), 0)
                key = jnp.where(kpos <= qpos, key, INT_MIN)
            key_ref[pl.ds(k0, half), :] = key
            k16 = (key >> DIGIT_BITS).astype(jnp.int16)
            k16_ref[pl.ds(k0, half), :] = k16
            span = DSA_GROUP * PACK
            for g in range(half // span):
                m = k16[span * g:span * g + PACK]
                for j in range(1, DSA_GROUP):
                    m = _max16(m, k16[span * g + PACK * j:span * g + PACK * (j + 1)])
                gms.append(m)
        rows = per_step * DSA_GB
        gm_ref[pl.ds(pl.multiple_of(step * rows, rows), rows), :] = jnp.concatenate(gms, axis=0)
        return c

    nfull = q0 // DSA_KB
    lax.fori_loop(0, nfull // 4, functools.partial(p1, masked=False, per_step=4), 0)
    lax.fori_loop(2 * (nfull // 4), nfull // 2, functools.partial(p1, masked=False, per_step=2), 0)
    lax.fori_loop(2 * (nfull // 2), nfull, functools.partial(p1, masked=False, per_step=1), 0)
    lax.fori_loop(nfull, nb, functools.partial(p1, masked=True, per_step=1), 0)

    ngc = (nb * DSA_GB + DSA_GCHUNK - 1) // DSA_GCHUNK

    def gm_bit(it, prefix):
        cand = prefix | jnp.left_shift(jnp.int32(1), DIGIT_BITS - 1 - it)
        cnt = _count_ge16(gm_ref, ngc, DSA_GCHUNK, cand + I16_MIN, TQ)
        return jnp.where(cnt >= kf, cand, prefix)

    lo_h = lax.fori_loop(0, DIGIT_BITS, gm_bit, jnp.zeros((1, TQ), I32)) + I16_MIN

    def gm_max(c, m):
        r0 = pl.multiple_of(c * DSA_GCHUNK, DSA_GCHUNK)
        for j in range(DSA_GCHUNK // PACK):
            m = _max16(m, gm_ref[pl.ds(r0 + PACK * j, PACK), :])
        return m

    gmax = lax.fori_loop(0, ngc, gm_max, jnp.full((PACK, TQ), I16_MIN, jnp.int16))
    hi_h = jnp.max(gmax.astype(I32), axis=0, keepdims=True) + 1

    def halve(lo_0, hi_0, want, first_probe, n_free):
        def step(it, lo, hi, chi):
            probe = jnp.where(it < 1, first_probe, INT_MIN)
            mid = jnp.where((lo < probe) & (probe < hi), probe, _avg_floor(lo, hi))
            c = _count_ge16(k16_ref, nb, DSA_KB, mid, TQ)
            ge = c >= want
            ex = jnp.logical_and(c == want, mid != lo)
            lo2 = jnp.where(ge, mid, lo)
            hi2 = jnp.where(ex, mid + 1, jnp.where(ge, hi, mid))
            chi2 = jnp.where(ex, EXACT, jnp.where(ge, chi, c))
            return lo2, hi2, chi2

        def active(lo, hi):
            return jnp.max(jnp.where(_avg_floor(lo, hi) != lo, 1.0, 0.0))

        def w_cond(st):
            return jnp.logical_and(st[0] < DSA_MAX_STEPS, st[4] > 0.0)

        def w_body(st):
            lo2, hi2, chi2 = step(st[0], st[1], st[2], st[3])
            return st[0] + 1, lo2, hi2, chi2, active(lo2, hi2)

        lo, hi, chi = lax.fori_loop(0, n_free, lambda it, st: step(it, *st),
                                    (lo_0, hi_0, jnp.zeros((1, TQ), F32)))
        _, lo, _, chi, _ = lax.while_loop(w_cond, w_body, (jnp.int32(n_free), lo, hi, chi, active(lo, hi)))
        return lo, chi

    tau_h, chi_h = halve(lo_h, hi_h, kf, INT_MIN, DSA_FREE_STEPS)
    exact_h = chi_h == EXACT
    want_l = kf - chi_h

    tau_h16 = tau_h.astype(jnp.int16)
    floor16 = jnp.full((DSA_KB, TQ), I16_MIN, jnp.int16)

    def build(b, c):
        k0 = pl.multiple_of(b * DSA_KB, DSA_KB)
        low = ((key_ref[pl.ds(k0, DSA_KB), :] & LOW_MASK) + I16_MIN).astype(jnp.int16)
        k16_ref[pl.ds(k0, DSA_KB), :] = jnp.where(k16_ref[pl.ds(k0, DSA_KB), :] == tau_h16, low, floor16)
        return c

    lax.fori_loop(0, nb, build, 0)
    tau_l, chi_l = halve(jnp.where(exact_h, 0, I16_MIN), jnp.where(exact_h, 1, -I16_MIN), want_l, I16_MIN + 1,
                         DSA_FREE_STEPS)
    tau = jnp.where(exact_h, tau_h << DIGIT_BITS, (tau_h << DIGIT_BITS) + (tau_l - I16_MIN))
    r = jnp.where(tau == INT_MIN, 0.0,
                  jnp.where(jnp.logical_or(exact_h, chi_l == EXACT), TIE_ALL, want_l - chi_l))

    AB = DSA_AB
    rowi = lax.broadcasted_iota(I32, (AB, AB), 0)
    coli = lax.broadcasted_iota(I32, (AB, AB), 1)
    tril = jnp.where(coli <= rowi, 1.0, 0.0).astype(BF16)
    aqT = aqT_ref[...]
    hrow = lax.broadcasted_iota(I32, (LANES, TQ), 0) // HEAD_DIM
    q_pad = []
    for h in range(A_HEADS):
        pair = aqT[LANES * (h // 2):LANES * (h // 2 + 1)]
        q_pad.append(jnp.where(hrow == (h % 2), pair, jnp.zeros_like(pair)))
    acc_ref[...] = jnp.zeros_like(acc_ref)

    nsub = 2 * DSA_KB // AB

    def p3(b, carry, speculate):
        cnt, ms = carry
        k0 = pl.multiple_of(b * 2 * DSA_KB, 2 * DSA_KB)
        bias = []
        for s2 in range(nsub):
            key = key_ref[pl.ds(k0 + s2 * AB, AB), :]
            eq = key == tau
            pref = jnp.dot(tril, jnp.where(eq, 1.0, 0.0).astype(BF16), preferred_element_type=F32) + cnt
            sel = jnp.where(key > tau, 0.0, jnp.where(eq, pref, NOT_TIED)) <= r
            bias.append(jnp.where(sel, 0.0, NEG_BIG))
            cnt = pref[AB - 1:AB, :]

        def logits(h):
            return [jnp.dot(ak_ref[pl.ds(k0 + s2 * AB, AB), pl.ds(LANES * (h // 2), LANES)], q_pad[h],
                            preferred_element_type=F32) + bias[s2] for s2 in range(nsub)]

        def block_max(lm):
            return functools.reduce(jnp.maximum, [jnp.max(t, axis=0, keepdims=True) for t in lm])

        def weights(lm, m):
            return jnp.concatenate([jnp.exp2(t - m).astype(BF16) for t in lm], axis=0)

        def values(h):
            return avT_ref[pl.ds(V_EXT * h, V_EXT), pl.ds(k0, 2 * DSA_KB)]

        def exact(ms2, lms=None):
            for h in range(A_HEADS):
                hs = pl.ds(V_EXT * h, V_EXT)
                lm = logits(h) if lms is None else lms[h]
                pv = jnp.dot(values(h), weights(lm, ms2[h]), preferred_element_type=F32)
                acc_ref[hs, :] = acc_ref[hs, :] * jnp.exp2(ms[h] - ms2[h]) + pv

        lms = [logits(h) for h in range(A_HEADS)]
        if not speculate:
            ms2 = [jnp.maximum(ms[h], block_max(lms[h])) for h in range(A_HEADS)]
            exact(ms2, lms)
            return cnt, tuple(ms2)

        ps, ms2 = [], []
        for h in range(A_HEADS):
            ps.append(weights(lms[h], ms[h]))
            ms2.append(jnp.maximum(ms[h], block_max(lms[h])))
        pvs = [jnp.dot(values(h), ps[h], preferred_element_type=F32) for h in range(A_HEADS)]
        rise = jnp.max(functools.reduce(jnp.maximum, [ms2[h] - ms[h] for h in range(A_HEADS)]))

        @pl.when(rise <= EXP2_HEADROOM)
        def _():
            for h in range(A_HEADS):
                hs = pl.ds(V_EXT * h, V_EXT)
                acc_ref[hs, :] = (acc_ref[hs, :] + pvs[h]) * jnp.exp2(ms[h] - ms2[h])

        @pl.when(rise > EXP2_HEADROOM)
        def _():
            exact(ms2)

        return cnt, tuple(ms2)

    init = (jnp.zeros((1, TQ), F32), tuple(jnp.full((1, TQ), M_FLOOR, F32) for _ in range(A_HEADS)))
    first = p3(0, init, speculate=False)
    lax.fori_loop(1, (nb + 1) // 2, functools.partial(p3, speculate=True), first)
    outT = jnp.concatenate([acc_ref[pl.ds(V_EXT * h, HEAD_DIM), :] / acc_ref[pl.ds(V_EXT * h + HEAD_DIM, 1), :]
                            for h in range(A_HEADS)], axis=0)
    o_ref[...] = outT.T.astype(BF16)


def _dsa(aqT, iqT, iwT, ik, ak, avT, S, tq):
    top_k = min(TOPK_MAX, S // 4)
    colT = lambda n: pl.BlockSpec((n, tq), lambda i: (0, i))
    return pl.pallas_call(
        functools.partial(_dsa_kernel, top_k=top_k),
        grid=(S // tq,),
        in_specs=[colT(D_A), colT(D_A), colT(SUBLANES),
                  _resident((S, LANES), lambda i: (0, 0)),
                  _resident((S, D_A), lambda i: (0, 0)),
                  _resident((A_HEADS * V_EXT, S), lambda i: (0, 0))],
        out_specs=pl.BlockSpec((tq, D_A), lambda i: (i, 0)),
        out_shape=jax.ShapeDtypeStruct((S, D_A), BF16),
        scratch_shapes=[pltpu.VMEM((S, tq), I32),
                        pltpu.VMEM((S, tq), jnp.int16),
                        pltpu.VMEM((S // DSA_GROUP + DSA_GCHUNK, tq), jnp.int16),
                        pltpu.VMEM((A_HEADS * V_EXT, tq), F32)],
        compiler_params=_params(("arbitrary",)),
        name="dsa",
    )(aqT, iqT, iwT, ik, ak, avT)


FF_CHUNKS = ((0, 768), (768, 768), (1536, 768), (2304, 512))


def _rms(v, w):
    return v * lax.rsqrt(jnp.mean(v * v, axis=-1, keepdims=True) + EPS) * w


def _outffn_kernel(x_ref, hm_ref, ha_ref, hs_ref, nw_ref, wo_ref, wg_ref, wu_ref, wd_ref, o_ref):
    mix = (jnp.dot(hm_ref[...], wo_ref[0:D_M, :], preferred_element_type=F32)
           + jnp.dot(ha_ref[...], wo_ref[D_M:D_M + D_A, :], preferred_element_type=F32)
           + jnp.dot(hs_ref[...], wo_ref[D_M + D_A:, :], preferred_element_type=F32))
    x1 = x_ref[...] + _rms(mix, nw_ref[1:2, :])
    h2 = _rms(x1, nw_ref[2:3, :]).astype(BF16)
    ff = jnp.zeros_like(x1)
    for c0, n in FF_CHUNKS:
        g = jnp.dot(h2, wg_ref[:, c0:c0 + n], preferred_element_type=F32)
        u = jnp.dot(h2, wu_ref[:, c0:c0 + n], preferred_element_type=F32)
        act = (g * jax.nn.sigmoid(g) * u).astype(BF16)
        ff = ff + jnp.dot(act, wd_ref[c0:c0 + n, :], preferred_element_type=F32)
    o_ref[...] = x1 + _rms(ff, nw_ref[3:4, :])


def _outffn(x2, hm, ha, hs, nw4, wo, wg, wu, wd, S, tm):
    row = lambda n: pl.BlockSpec((tm, n), lambda i: (i, 0))
    return pl.pallas_call(
        _outffn_kernel,
        grid=(S // tm,),
        in_specs=[row(D_MODEL), row(D_M), row(D_A), row(D_S),
                  pl.BlockSpec((4, D_MODEL), lambda i: (0, 0)),
                  _resident((D_MODEL, D_MODEL), lambda i: (0, 0)),
                  _resident((D_MODEL, D_FF), lambda i: (0, 0)),
                  _resident((D_MODEL, D_FF), lambda i: (0, 0)),
                  _resident((D_FF, D_MODEL), lambda i: (0, 0))],
        out_specs=row(D_MODEL),
        out_shape=jax.ShapeDtypeStruct((S, D_MODEL), F32),
        compiler_params=_params(("arbitrary",)),
        name="outffn",
    )(x2, hm, ha, hs, nw4, wo, wg, wu, wd)


def _plan(S):
    assert S % (2 * DSA_KB) == 0
    return dict(tm=min(S, 512), lm=min(S, 256), ls=min(S, 256), tq=min(S, 256))


def _pad_lanes(v):
    return jnp.pad(v, [(0, 0)] * (v.ndim - 1) + [(0, LANES - v.shape[-1])])


def kernel(x, positions, norm_w, w_in, mlstm_gate_bias, mlstm_norm_w, conv_w, conv_b, dt_bias, a_log,
           d_skip, ssd_norm_w, w_out, w_gate, w_up, w_down):
    B, S, D = x.shape
    assert B == 1 and D == D_MODEL
    depth = w_in.shape[0]
    plan = _plan(S)

    o = np.cumsum([0, D_M, D_M, D_M, D_M, M_HEADS, M_HEADS, D_A, D_A, D_A, IDX_HEADS * IDX_DIM, IDX_DIM,
                   IDX_HEADS, D_S, CONV_DIM, S_HEADS])
    wp = jnp.concatenate([
        w_in[:, :, o[0]:o[4]], w_in[:, :, o[6]:o[10]], w_in[:, :, o[12]:o[13]], w_in[:, :, o[13]:o[14]],
        _pad_lanes(w_in[:, :, o[4]:o[6]]), _pad_lanes(w_in[:, :, o[10]:o[12]]), _pad_lanes(w_in[:, :, o[14]:o[15]]),
    ], axis=-1).astype(BF16)
    wo = w_out.astype(BF16)
    wg = w_gate.astype(BF16)
    wu = w_up.astype(BF16)
    wd = w_down.astype(BF16)
    gb = _pad_lanes(mlstm_gate_bias[:, None, :])
    dtb = _pad_lanes(dt_bias[:, None, :])
    alog = _pad_lanes(a_log[:, None, :])
    dsk = jnp.repeat(d_skip, HEAD_DIM, axis=-1)[:, None, :]

    cf, sa, sb = _rope_tables(positions.astype(I32), S)
    x2 = x.reshape(S, D)
    for l in range(depth):
        om, omg, os_, odt, aqT, iqT, iwT, ik, ak, avT = _inproj(
            x2, norm_w[l, 0:1], wp[l], cf, sa, sb, S, plan["tm"])
        hm = _mlstm(om, omg, gb[l], mlstm_norm_w[l][None, :], S, plan["lm"])
        hs = _ssd(os_, odt, conv_w[l], conv_b[l][None, :], dtb[l], alog[l], dsk[l], ssd_norm_w[l][None, :],
                  S, plan["ls"])
        ha = _dsa(aqT, iqT, iwT, ik, ak, avT, S, plan["tq"])
        x2 = _outffn(x2, hm, ha, hs, norm_w[l], wo[l], wg[l], wu[l], wd[l], S, plan["tm"])
    return x2.reshape(B, S, D)
```

```python
import functools

import numpy as np
import jax
import jax.numpy as jnp
from jax import lax
from jax.experimental import pallas as pl
from jax.experimental.pallas import tpu as pltpu

F32 = jnp.float32
BF16 = jnp.bfloat16
I32 = jnp.int32

D_MODEL = 1024
HEAD_DIM = 64
D_M = 256
M_HEADS = 4
D_A = 256
A_HEADS = 4
IDX_HEADS = 4
IDX_DIM = 64
TOPK_MAX = 256
D_S = 512
S_HEADS = 8
S_GROUPS = 2
S_STATE = 128
S_CONV = 4
CONV_DIM = D_S + 2 * S_GROUPS * S_STATE
ROPE_THETA = 500000.0
ROPE_DIM = HEAD_DIM // 4
ROPE_HALF = ROPE_DIM // 2
D_FF = 2816
EPS = 1e-6

LANES = 128
SUBLANES = 8
PACK = 16
VMEM_LIMIT = 60 * 1024 * 1024

C_M = 0
C_A = C_M + 4 * D_M
C_Z = C_A + 4 * D_A
C_XBC = C_Z + D_S
C_MG = C_XBC + CONV_DIM
C_IK = C_MG + LANES
C_DT = C_IK + LANES
N_P = C_DT + LANES

INT_MIN = -2 ** 31
NEG_BIG = -1e30
M_FLOOR = -1e29
LOG2E = 1.4426950408889634
EXP2_HEADROOM = 64.0

DSA_KB = 512
DSA_AB = 256
DSA_GROUP = 16
DSA_GB = DSA_KB // DSA_GROUP
DSA_GCHUNK = 128
V_EXT = HEAD_DIM + PACK
DIGIT_BITS = 16
LOW_MASK = 2 ** DIGIT_BITS - 1
I16_MIN = -2 ** (DIGIT_BITS - 1)
DSA_MAX_STEPS = DIGIT_BITS + 8
DSA_FREE_STEPS = 8
TIE_ALL = 1e6
NOT_TIED = 1e9
EXACT = -1e9


def _params(sem):
    return pltpu.CompilerParams(dimension_semantics=sem, vmem_limit_bytes=VMEM_LIMIT)


def _resident(shape, index_map):
    return pl.BlockSpec(shape, index_map, pipeline_mode=pl.Buffered(1))


def _split3(x):
    h = x.astype(BF16)
    r = x - h.astype(F32)
    m = r.astype(BF16)
    lo = (r - m.astype(F32)).astype(BF16)
    return h, m, lo


def _cumsum_cols(tril, x):
    return sum(jnp.dot(tril, t, preferred_element_type=F32) for t in _split3(x))


def _cumsum_rows(x, triu):
    return sum(jnp.dot(t, triu, preferred_element_type=F32) for t in _split3(x))


def _tri(L):
    row = lax.broadcasted_iota(I32, (L, L), 0)
    col = lax.broadcasted_iota(I32, (L, L), 1)
    causal = col <= row
    tril = jnp.where(causal, 1.0, 0.0).astype(BF16)
    triu = jnp.where(row <= col, 1.0, 0.0).astype(BF16)
    return causal, tril, triu


def _rope_tables_kernel(pos_ref, inv_ref, cf_ref, sa_ref, sb_ref):
    ang = pos_ref[...].astype(F32) * inv_ref[...]
    c = jnp.cos(ang)
    s = jnp.sin(ang)
    j = lax.broadcasted_iota(I32, ang.shape, 1) & (HEAD_DIM - 1)
    cf_ref[...] = jnp.where(j < ROPE_DIM, c, 1.0)
    sa_ref[...] = jnp.where(j < ROPE_HALF, -s, 0.0)
    sb_ref[...] = jnp.where(j < ROPE_HALF, 0.0, jnp.where(j < ROPE_DIM, s, 0.0))


def _rope_tables(positions, S):
    tb = min(S, 1024)
    inv = np.power(np.float32(ROPE_THETA), -np.arange(ROPE_HALF, dtype=np.float32) / np.float32(ROPE_HALF))
    lane = np.arange(LANES) % HEAD_DIM
    inv_lanes = np.where(lane < ROPE_DIM, inv[lane % ROPE_HALF], np.float32(0)).astype(np.float32)[None, :]
    tab = jax.ShapeDtypeStruct((S, LANES), F32)
    row = pl.BlockSpec((tb, LANES), lambda i: (i, 0))
    return pl.pallas_call(
        _rope_tables_kernel,
        grid=(S // tb,),
        in_specs=[pl.BlockSpec((tb, 1), lambda i: (i, 0)), pl.BlockSpec((1, LANES), lambda i: (0, 0))],
        out_specs=[row, row, row],
        out_shape=[tab, tab, tab],
        compiler_params=_params(("arbitrary",)),
        name="rope_tables",
    )(positions.reshape(S, 1), jnp.asarray(inv_lanes))


def _inproj_kernel(x_ref, nw_ref, w_ref, cf_ref, sa_ref, sb_ref,
                   om_ref, omg_ref, os_ref, odt_ref,
                   aqT_ref, iqT_ref, iwT_ref, ik_ref, ak_ref, avT_ref):
    x = x_ref[...]
    ms = jnp.mean(x * x, axis=-1, keepdims=True)
    h = (x * lax.rsqrt(ms + EPS) * nw_ref[...]).astype(BF16)

    def proj(c0, n):
        return jnp.dot(h, w_ref[:, c0:c0 + n], preferred_element_type=F32)

    om_ref[...] = proj(C_M, 4 * D_M)
    os_ref[...] = proj(C_Z, D_S + CONV_DIM)
    omg_ref[...] = proj(C_MG, LANES)
    odt_ref[...] = proj(C_DT, LANES)

    cf = cf_ref[...]
    sa = sa_ref[...]
    sb = sb_ref[...]

    def rope(c, cf=cf, sa=sa, sb=sb):
        return c * cf + pltpu.roll(c, LANES - ROPE_HALF, 1) * sa + pltpu.roll(c, ROPE_HALF, 1) * sb

    def rope2(a2):
        return jnp.concatenate([rope(a2[:, :LANES]), rope(a2[:, LANES:])], axis=1)

    a = proj(C_A, 4 * D_A)
    aqT_ref[...] = (rope2(a[:, 0:D_A]) * (HEAD_DIM ** -0.5 * LOG2E)).T.astype(BF16)
    ak_ref[...] = rope2(a[:, D_A:2 * D_A]).astype(BF16)
    vT = a[:, 2 * D_A:3 * D_A].T.astype(BF16)
    ones = jnp.ones((V_EXT - HEAD_DIM, vT.shape[1]), BF16)
    avT_ref[...] = jnp.concatenate(
        [t for h in range(A_HEADS) for t in (vT[HEAD_DIM * h:HEAD_DIM * (h + 1)], ones)], axis=0)
    iqT_ref[...] = rope2(a[:, 3 * D_A:4 * D_A]).T.astype(BF16)

    ikw = proj(C_IK, LANES)
    is_ik = lax.broadcasted_iota(I32, ikw.shape, 1) < IDX_DIM
    ikr = rope(ikw, jnp.where(is_ik, cf, 1.0), jnp.where(is_ik, sa, 0.0), jnp.where(is_ik, sb, 0.0))
    ik_ref[...] = jnp.where(is_ik, ikr, 0.0).astype(BF16)
    iwT_ref[...] = ikw.T[IDX_DIM:IDX_DIM + SUBLANES, :]


def _inproj(x2, nw, wp, cf, sa, sb, S, tm):
    row = lambda n: pl.BlockSpec((tm, n), lambda i: (i, 0))
    colT = lambda n: pl.BlockSpec((n, tm), lambda i: (0, i))
    f = lambda n, dt=F32: jax.ShapeDtypeStruct((S, n), dt)
    fT = lambda n, dt=BF16: jax.ShapeDtypeStruct((n, S), dt)
    return pl.pallas_call(
        _inproj_kernel,
        grid=(S // tm,),
        in_specs=[row(D_MODEL), pl.BlockSpec((1, D_MODEL), lambda i: (0, 0)),
                  _resident((D_MODEL, N_P), lambda i: (0, 0)),
                  row(LANES), row(LANES), row(LANES)],
        out_specs=[row(4 * D_M), row(LANES), row(D_S + CONV_DIM), row(LANES),
                   colT(D_A), colT(D_A), colT(SUBLANES), row(LANES), row(D_A), colT(A_HEADS * V_EXT)],
        out_shape=[f(4 * D_M), f(LANES), f(D_S + CONV_DIM), f(LANES),
                   fT(D_A), fT(D_A), fT(SUBLANES, F32), f(LANES, BF16), f(D_A, BF16), fT(A_HEADS * V_EXT)],
        compiler_params=_params(("arbitrary",)),
        name="inproj",
    )(x2, nw, wp, cf, sa, sb)


def _mlstm_kernel(om_ref, omg_ref, gb_ref, nw_ref, o_ref, C_ref, n_ref, m_ref):
    L = om_ref.shape[0]

    @pl.when(pl.program_id(0) == 0)
    def _():
        C_ref[...] = jnp.zeros_like(C_ref)
        n_ref[...] = jnp.zeros_like(n_ref)
        m_ref[...] = jnp.zeros_like(m_ref)

    causal, tril, triu = _tri(L)
    G = omg_ref[...] + gb_ref[...]
    LF = jnp.minimum(G, 0.0) - jnp.log1p(jnp.exp(-jnp.abs(G)))
    Bc = _cumsum_cols(tril, LF)
    GT = G.T
    BrT = _cumsum_rows(LF.T, triu)
    nw = nw_ref[...]
    heads = range(M_HEADS)

    def head(c0, h):
        blk = om_ref[:, pl.ds(c0 + LANES * (h // 2), LANES)]
        return blk[:, HEAD_DIM * (h % 2):HEAD_DIM * (h % 2 + 1)]

    qf = [head(0, h) * (HEAD_DIM ** -0.5) for h in heads]
    kf = [head(D_M, h) for h in heads]
    q = [t.astype(BF16) for t in qf]
    k = [t.astype(BF16) for t in kf]
    v = [head(2 * D_M, h).astype(BF16) for h in heads]
    C_prev = [C_ref[h] for h in heads]
    n_prev = [n_ref[h:h + 1, :] for h in heads]
    m_prev = [m_ref[h:h + 1, 0:1] for h in heads]
    qk = [lax.dot_general(q[h], k[h], (((1,), (1,)), ((), ())), preferred_element_type=F32) for h in heads]
    qC = [jnp.dot(q[h], C_prev[h].astype(BF16), preferred_element_type=F32) for h in heads]
    s, scale, m_t, kw, decay = [], [], [], [], []
    for h in heads:
        f_l = M_HEADS + h
        b_col = Bc[:, f_l:f_l + 1]
        b_row = BrT[f_l:f_l + 1, :]
        i_col = G[:, h:h + 1]
        i_row = GT[h:h + 1, :]
        logd = jnp.where(causal, b_col + (i_row - b_row), -jnp.inf)
        m_inter = b_col + m_prev[h]
        mt = jnp.maximum(m_inter, jnp.max(logd, axis=-1, keepdims=True))
        s.append(qk[h] * jnp.exp(logd - mt))
        scale.append(jnp.exp(m_inter - mt))
        m_t.append(mt)
        b_last = Bc[L - 1:L, f_l:f_l + 1]
        m_new = jnp.maximum(b_last + m_prev[h], jnp.max(b_last - b_row + i_row, axis=-1, keepdims=True))
        kw.append(kf[h] * jnp.exp(b_last - b_col + i_col - m_new))
        decay.append(jnp.exp(b_last + m_prev[h] - m_new))
        m_ref[h:h + 1, :] = jnp.broadcast_to(m_new, (1, LANES))
    sv = [jnp.dot(s[h].astype(BF16), v[h], preferred_element_type=F32) for h in heads]
    kv = [lax.dot_general(kw[h].astype(BF16), v[h], (((0,), (0,)), ((), ())), preferred_element_type=F32)
          for h in heads]
    outs = []
    for h in heads:
        num = sv[h] + scale[h] * qC[h]
        den = (jnp.sum(s[h], axis=-1, keepdims=True)
               + scale[h] * jnp.sum(qf[h] * n_prev[h], axis=-1, keepdims=True))
        hh = num / jnp.maximum(jnp.abs(den), jnp.exp(-m_t[h]))
        C_ref[h] = decay[h] * C_prev[h] + kv[h]
        n_ref[h:h + 1, :] = decay[h] * n_prev[h] + jnp.sum(kw[h], axis=0, keepdims=True)
        y = hh * lax.rsqrt(jnp.mean(hh * hh, axis=-1, keepdims=True) + EPS)
        outs.append(jax.nn.sigmoid(head(3 * D_M, h)) * y)
    o_ref[...] = (jnp.concatenate(outs, axis=1) * nw).astype(BF16)


def _mlstm(om, omg, gb, nw, S, L):
    return pl.pallas_call(
        _mlstm_kernel,
        grid=(S // L,),
        in_specs=[pl.BlockSpec((L, 4 * D_M), lambda i: (i, 0)), pl.BlockSpec((L, LANES), lambda i: (i, 0)),
                  pl.BlockSpec((1, LANES), lambda i: (0, 0)), pl.BlockSpec((1, D_M), lambda i: (0, 0))],
        out_specs=pl.BlockSpec((L, D_M), lambda i: (i, 0)),
        out_shape=jax.ShapeDtypeStruct((S, D_M), BF16),
        scratch_shapes=[pltpu.VMEM((M_HEADS, HEAD_DIM, HEAD_DIM), F32),
                        pltpu.VMEM((SUBLANES, HEAD_DIM), F32),
                        pltpu.VMEM((SUBLANES, LANES), F32)],
        compiler_params=_params(("arbitrary",)),
        name="mlstm",
    )(om, omg, gb, nw)


def _ssd_kernel(os_ref, odt_ref, cw_ref, cb_ref, dtb_ref, alog_ref, dsk_ref, nw_ref, o_ref,
                carry_ref, st_ref):
    L = os_ref.shape[0]

    @pl.when(pl.program_id(0) == 0)
    def _():
        carry_ref[...] = jnp.zeros_like(carry_ref)
        st_ref[...] = jnp.zeros_like(st_ref)

    causal, tril, triu = _tri(L)
    z = os_ref[:, 0:D_S]
    raw = os_ref[:, D_S:D_S + CONV_DIM]
    ext = jnp.concatenate([carry_ref[...], raw], axis=0)
    cw = cw_ref[...]
    xbc = cb_ref[...] + cw[S_CONV - 1:S_CONV, :] * raw
    for j in range(S_CONV - 1):
        off = SUBLANES - (S_CONV - 1) + j
        xbc = xbc + cw[j:j + 1, :] * ext[off:off + L, :]
    carry_ref[...] = raw[L - SUBLANES:L, :]
    xbc = xbc * jax.nn.sigmoid(xbc)

    dtr = odt_ref[...] + dtb_ref[...]
    DT = jnp.maximum(dtr, 0.0) + jnp.log1p(jnp.exp(-jnp.abs(dtr)))
    dA = DT * (-jnp.exp(alog_ref[...]))
    Ac = _cumsum_cols(tril, dA)
    ArT = _cumsum_rows(dA.T, triu)
    DTT = DT.T
    hpg = S_HEADS // S_GROUPS
    ys = []
    for g in range(S_GROUPS):
        Bg = xbc[:, D_S + S_STATE * g:D_S + S_STATE * (g + 1)]
        Cg = xbc[:, D_S + S_GROUPS * S_STATE + S_STATE * g:D_S + S_GROUPS * S_STATE + S_STATE * (g + 1)]
        Cb = Cg.astype(BF16)
        CB = lax.dot_general(Cb, Bg.astype(BF16), (((1,), (1,)), ((), ())), preferred_element_type=F32)
        BgT = Bg.T
        for hh in range(hpg):
            h = g * hpg + hh
            xpair = xbc[:, LANES * (h // 2):LANES * (h // 2 + 1)]
            xh = xpair[:, HEAD_DIM * (h % 2):HEAD_DIM * (h % 2 + 1)]
            ac_col = Ac[:, h:h + 1]
            ac_row = ArT[h:h + 1, :]
            dec = jnp.exp(jnp.where(causal, ac_col - ac_row, -jnp.inf))
            sc = (CB * dec).astype(BF16)
            xdt = (xh * DT[:, h:h + 1]).astype(BF16)
            st = st_ref[h]
            y = (jnp.dot(sc, xdt, preferred_element_type=F32)
                 + jnp.dot(Cb, st.astype(BF16), preferred_element_type=F32) * jnp.exp(ac_col))
            a_last = Ac[L - 1:L, h:h + 1]
            w_row = jnp.exp(a_last - ac_row) * DTT[h:h + 1, :]
            st_ref[h] = st * jnp.exp(a_last) + jnp.dot((BgT * w_row).astype(BF16), xh.astype(BF16),
                                                       preferred_element_type=F32)
            ys.append(y)
    Y = jnp.concatenate(ys, axis=1) + dsk_ref[...] * xbc[:, 0:D_S]
    gated = Y * (z * jax.nn.sigmoid(z))
    gw = D_S // S_GROUPS
    outs = []
    for g in range(S_GROUPS):
        gg = gated[:, gw * g:gw * (g + 1)]
        outs.append(gg * lax.rsqrt(jnp.mean(gg * gg, axis=-1, keepdims=True) + EPS))
    o_ref[...] = (jnp.concatenate(outs, axis=1) * nw_ref[...]).astype(BF16)


def _ssd(os_, odt, cw, cb, dtb, alog, dsk, nw, S, L):
    full = lambda r, c: pl.BlockSpec((r, c), lambda i: (0, 0))
    return pl.pallas_call(
        _ssd_kernel,
        grid=(S // L,),
        in_specs=[pl.BlockSpec((L, D_S + CONV_DIM), lambda i: (i, 0)), pl.BlockSpec((L, LANES), lambda i: (i, 0)),
                  full(S_CONV, CONV_DIM), full(1, CONV_DIM), full(1, LANES), full(1, LANES),
                  full(1, D_S), full(1, D_S)],
        out_specs=pl.BlockSpec((L, D_S), lambda i: (i, 0)),
        out_shape=jax.ShapeDtypeStruct((S, D_S), BF16),
        scratch_shapes=[pltpu.VMEM((SUBLANES, CONV_DIM), F32),
                        pltpu.VMEM((S_HEADS, S_STATE, HEAD_DIM), F32)],
        compiler_params=_params(("arbitrary",)),
        name="ssd",
    )(os_, odt, cw, cb, dtb, alog, dsk, nw)


def _max16(a, b):
    return jnp.where(a >= b, a, b)


def _count_ge16(ref, nblk, rb, thr_row, tq):
    thr = jnp.broadcast_to(thr_row.astype(jnp.int16), (PACK, tq))
    n_acc = 4
    one = jnp.ones((PACK, tq), jnp.int16)

    def body(b, accs):
        slab = ref[pl.ds(pl.multiple_of(b * rb, rb), rb), :]
        accs = list(accs)
        for j in range(rb // PACK):
            blk = slab[PACK * j:PACK * (j + 1)]
            a = accs[j % n_acc]
            accs[j % n_acc] = jnp.where(blk >= thr, a + one, a)
        return tuple(accs)

    accs = lax.fori_loop(0, nblk, body, tuple(jnp.zeros((PACK, tq), jnp.int16) for _ in range(n_acc)))
    tot = (accs[0].astype(I32) + accs[1].astype(I32)) + (accs[2].astype(I32) + accs[3].astype(I32))
    return jnp.sum(tot.astype(F32), axis=0, keepdims=True)


def _avg_floor(lo, hi):
    return (lo >> 1) + (hi >> 1) + (lo & hi & 1)


def _dsa_kernel(aqT_ref, iqT_ref, iwT_ref, ik_ref, ak_ref, avT_ref, o_ref, key_ref, k16_ref, gm_ref, acc_ref,
                *, top_k):
    TQ = aqT_ref.shape[1]
    q0 = pl.program_id(0) * TQ
    nb = (q0 + TQ + DSA_KB - 1) // DSA_KB
    qpos = q0 + lax.broadcasted_iota(I32, (1, TQ), 1)
    kf = float(top_k)
    w_scale = (IDX_HEADS ** -0.5) * (IDX_DIM ** -0.5)

    iqT = iqT_ref[...]
    zpad = jnp.zeros((LANES - IDX_DIM, TQ), BF16)
    iq_pad = [jnp.concatenate([iqT[IDX_DIM * h:IDX_DIM * (h + 1)], zpad], axis=0) for h in range(IDX_HEADS)]
    w_rows = [iwT_ref[h:h + 1, :] * w_scale for h in range(IDX_HEADS)]

    gm_ref[pl.ds(pl.multiple_of(nb * DSA_GB, DSA_GB), DSA_GCHUNK - DSA_GB), :] = jnp.full(
        (DSA_GCHUNK - DSA_GB, TQ), I16_MIN, jnp.int16)

    @pl.when(nb % 2 == 1)
    def _():
        key_ref[pl.ds(pl.multiple_of(nb * DSA_KB, DSA_KB), DSA_KB), :] = jnp.full((DSA_KB, TQ), INT_MIN, I32)

    half = DSA_KB // 2

    def p1(step, c, masked, per_step):
        n_half = 2 * per_step
        base = step * (per_step * DSA_KB)
        k0s = [pl.multiple_of(base + s2 * half, half) for s2 in range(n_half)]
        dots = [[jnp.dot(ik_ref[pl.ds(k0s[s2], half), :], iq_pad[h], preferred_element_type=F32)
                 for h in range(IDX_HEADS)] for s2 in range(n_half)]
        gms = []
        for s2 in range(n_half):
            k0 = k0s[s2]
            sc = jnp.zeros((half, TQ), F32)
            for h in range(IDX_HEADS):
                sc = sc + w_rows[h] * jnp.maximum(dots[s2][h], 0.0)
            bits = lax.bitcast_convert_type(sc, I32)
            key = jnp.where(bits < 0, INT_MIN - bits, bits)
            if masked:
                kpos = k0 + lax.broadcasted_iota(I32, (half, 1), 0)
                key = jnp.where(kpos <= qpos, key, INT_MIN)
            key_ref[pl.ds(k0, half), :] = key
            k16 = (key >> DIGIT_BITS).astype(jnp.int16)
            k16_ref[pl.ds(k0, half), :] = k16
            span = DSA_GROUP * PACK
            for g in range(half // span):
                m = k16[span * g:span * g + PACK]
                for j in range(1, DSA_GROUP):
                    m = _max16(m, k16[span * g + PACK * j:span * g + PACK * (j + 1)])
                gms.append(m)
        rows = per_step * DSA_GB
        gm_ref[pl.ds(pl.multiple_of(step * rows, rows), rows), :] = jnp.concatenate(gms, axis=0)
        return c

    nfull = q0 // DSA_KB
    lax.fori_loop(0, nfull // 4, functools.partial(p1, masked=False, per_step=4), 0)
    lax.fori_loop(2 * (nfull // 4), nfull // 2, functools.partial(p1, masked=False, per_step=2), 0)
    lax.fori_loop(2 * (nfull // 2), nfull, functools.partial(p1, masked=False, per_step=1), 0)
    lax.fori_loop(nfull, nb, functools.partial(p1, masked=True, per_step=1), 0)

    ngc = (nb * DSA_GB + DSA_GCHUNK - 1) // DSA_GCHUNK

    def gm_bit(it, prefix):
        cand = prefix | jnp.left_shift(jnp.int32(1), DIGIT_BITS - 1 - it)
        cnt = _count_ge16(gm_ref, ngc, DSA_GCHUNK, cand + I16_MIN, TQ)
        return jnp.where(cnt >= kf, cand, prefix)

    lo_h = lax.fori_loop(0, DIGIT_BITS, gm_bit, jnp.zeros((1, TQ), I32)) + I16_MIN

    def gm_max(c, m):
        r0 = pl.multiple_of(c * DSA_GCHUNK, DSA_GCHUNK)
        for j in range(DSA_GCHUNK // PACK):
            m = _max16(m, gm_ref[pl.ds(r0 + PACK * j, PACK), :])
        return m

    gmax = lax.fori_loop(0, ngc, gm_max, jnp.full((PACK, TQ), I16_MIN, jnp.int16))
    hi_h = jnp.max(gmax.astype(I32), axis=0, keepdims=True) + 1

    def halve(lo_0, hi_0, want, first_probe, n_free):
        def step(it, lo, hi, chi):
            probe = jnp.where(it < 1, first_probe, INT_MIN)
            mid = jnp.where((lo < probe) & (probe < hi), probe, _avg_floor(lo, hi))
            c = _count_ge16(k16_ref, nb, DSA_KB, mid, TQ)
            ge = c >= want
            ex = jnp.logical_and(c == want, mid != lo)
            lo2 = jnp.where(ge, mid, lo)
            hi2 = jnp.where(ex, mid + 1, jnp.where(ge, hi, mid))
            chi2 = jnp.where(ex, EXACT, jnp.where(ge, chi, c))
            return lo2, hi2, chi2

        def active(lo, hi):
            return jnp.max(jnp.where(_avg_floor(lo, hi) != lo, 1.0, 0.0))

        def w_cond(st):
            return jnp.logical_and(st[0] < DSA_MAX_STEPS, st[4] > 0.0)

        def w_body(st):
            lo2, hi2, chi2 = step(st[0], st[1], st[2], st[3])
            return st[0] + 1, lo2, hi2, chi2, active(lo2, hi2)

        lo, hi, chi = lax.fori_loop(0, n_free, lambda it, st: step(it, *st),
                                    (lo_0, hi_0, jnp.zeros((1, TQ), F32)))
        _, lo, _, chi, _ = lax.while_loop(w_cond, w_body, (jnp.int32(n_free), lo, hi, chi, active(lo, hi)))
        return lo, chi

    tau_h, chi_h = halve(lo_h, hi_h, kf, INT_MIN, DSA_FREE_STEPS)
    exact_h = chi_h == EXACT
    want_l = kf - chi_h

    tau_h16 = tau_h.astype(jnp.int16)
    floor16 = jnp.full((DSA_KB, TQ), I16_MIN, jnp.int16)

    def build(b, c):
        k0 = pl.multiple_of(b * DSA_KB, DSA_KB)
        low = ((key_ref[pl.ds(k0, DSA_KB), :] & LOW_MASK) + I16_MIN).astype(jnp.int16)
        k16_ref[pl.ds(k0, DSA_KB), :] = jnp.where(k16_ref[pl.ds(k0, DSA_KB), :] == tau_h16, low, floor16)
        return c

    lax.fori_loop(0, nb, build, 0)
    tau_l, chi_l = halve(jnp.where(exact_h, 0, I16_MIN), jnp.where(exact_h, 1, -I16_MIN), want_l, I16_MIN + 1,
                         DSA_FREE_STEPS)
    tau = jnp.where(exact_h, tau_h << DIGIT_BITS, (tau_h << DIGIT_BITS) + (tau_l - I16_MIN))
    r = jnp.where(tau == INT_MIN, 0.0,
                  jnp.where(jnp.logical_or(exact_h, chi_l == EXACT), TIE_ALL, want_l - chi_l))

    AB = DSA_AB
    rowi = lax.broadcasted_iota(I32, (AB, AB), 0)
    coli = lax.broadcasted_iota(I32, (AB, AB), 1)
    tril = jnp.where(coli <= rowi, 1.0, 0.0).astype(BF16)
    aqT = aqT_ref[...]
    hrow = lax.broadcasted_iota(I32, (LANES, TQ), 0) // HEAD_DIM
    q_pad = []
    for h in range(A_HEADS):
        pair = aqT[LANES * (h // 2):LANES * (h // 2 + 1)]
        q_pad.append(jnp.where(hrow == (h % 2), pair, jnp.zeros_like(pair)))
    acc_ref[...] = jnp.zeros_like(acc_ref)

    nsub = 2 * DSA_KB // AB

    def p3(b, carry, speculate):
        cnt, ms = carry
        k0 = pl.multiple_of(b * 2 * DSA_KB, 2 * DSA_KB)
        bias = []
        for s2 in range(nsub):
            key = key_ref[pl.ds(k0 + s2 * AB, AB), :]
            eq = key == tau
            pref = jnp.dot(tril, jnp.where(eq, 1.0, 0.0).astype(BF16), preferred_element_type=F32)
            sel = jnp.where(key > tau, -NOT_TIED, jnp.where(eq, pref, NOT_TIED)) <= r - cnt
            bias.append(jnp.where(sel, 0.0, NEG_BIG))
            cnt = cnt + pref[AB - 1:AB, :]

        def logits(h):
            return [jnp.dot(ak_ref[pl.ds(k0 + s2 * AB, AB), pl.ds(LANES * (h // 2), LANES)], q_pad[h],
                            preferred_element_type=F32) + bias[s2] for s2 in range(nsub)]

        def block_max(lm):
            return functools.reduce(jnp.maximum, [jnp.max(t, axis=0, keepdims=True) for t in lm])

        def weights(lm, m):
            return jnp.concatenate([jnp.exp2(t - m).astype(BF16) for t in lm], axis=0)

        def values(h):
            return avT_ref[pl.ds(V_EXT * h, V_EXT), pl.ds(k0, 2 * DSA_KB)]

        def exact(ms2, lms=None):
            for h in range(A_HEADS):
                hs = pl.ds(V_EXT * h, V_EXT)
                lm = logits(h) if lms is None else lms[h]
                pv = jnp.dot(values(h), weights(lm, ms2[h]), preferred_element_type=F32)
                acc_ref[hs, :] = acc_ref[hs, :] * jnp.exp2(ms[h] - ms2[h]) + pv

        lms = [logits(h) for h in range(A_HEADS)]
        if not speculate:
            ms2 = [jnp.maximum(ms[h], block_max(lms[h])) for h in range(A_HEADS)]
            exact(ms2, lms)
            return cnt, tuple(ms2)

        ps, ms2 = [], []
        for h in range(A_HEADS):
            ps.append(weights(lms[h], ms[h]))
            ms2.append(jnp.maximum(ms[h], block_max(lms[h])))
        pvs = [jnp.dot(values(h), ps[h], preferred_element_type=F32) for h in range(A_HEADS)]
        rise = jnp.max(functools.reduce(jnp.maximum, [ms2[h] - ms[h] for h in range(A_HEADS)]))

        @pl.when(rise <= EXP2_HEADROOM)
        def _():
            for h in range(A_HEADS):
                hs = pl.ds(V_EXT * h, V_EXT)
                acc_ref[hs, :] = (acc_ref[hs, :] + pvs[h]) * jnp.exp2(ms[h] - ms2[h])

        @pl.when(rise > EXP2_HEADROOM)
        def _():
            exact(ms2)

        return cnt, tuple(ms2)

    init = (jnp.zeros((1, TQ), F32), tuple(jnp.full((1, TQ), M_FLOOR, F32) for _ in range(A_HEADS)))
    first = p3(0, init, speculate=False)
    lax.fori_loop(1, (nb + 1) // 2, functools.partial(p3, speculate=True), first)
    outT = jnp.concatenate([acc_ref[pl.ds(V_EXT * h, HEAD_DIM), :] / acc_ref[pl.ds(V_EXT * h + HEAD_DIM, 1), :]
                            for h in range(A_HEADS)], axis=0)
    o_ref[...] = outT.T.astype(BF16)


def _dsa(aqT, iqT, iwT, ik, ak, avT, S, tq):
    top_k = min(TOPK_MAX, S // 4)
    colT = lambda n: pl.BlockSpec((n, tq), lambda i: (0, i))
    return pl.pallas_call(
        functools.partial(_dsa_kernel, top_k=top_k),
        grid=(S // tq,),
        in_specs=[colT(D_A), colT(D_A), colT(SUBLANES),
                  _resident((S, LANES), lambda i: (0, 0)),
                  _resident((S, D_A), lambda i: (0, 0)),
                  _resident((A_HEADS * V_EXT, S), lambda i: (0, 0))],
        out_specs=pl.BlockSpec((tq, D_A), lambda i: (i, 0)),
        out_shape=jax.ShapeDtypeStruct((S, D_A), BF16),
        scratch_shapes=[pltpu.VMEM((S, tq), I32),
                        pltpu.VMEM((S, tq), jnp.int16),
                        pltpu.VMEM((S // DSA_GROUP + DSA_GCHUNK, tq), jnp.int16),
                        pltpu.VMEM((A_HEADS * V_EXT, tq), F32)],
        compiler_params=_params(("arbitrary",)),
        name="dsa",
    )(aqT, iqT, iwT, ik, ak, avT)


FF_CHUNKS = ((0, 768), (768, 768), (1536, 768), (2304, 512))


def _rms(v, w):
    return v * lax.rsqrt(jnp.mean(v * v, axis=-1, keepdims=True) + EPS) * w


def _outffn_kernel(x_ref, hm_ref, ha_ref, hs_ref, nw_ref, wo_ref, wg_ref, wu_ref, wd_ref, o_ref):
    mix = (jnp.dot(hm_ref[...], wo_ref[0:D_M, :], preferred_element_type=F32)
           + jnp.dot(ha_ref[...], wo_ref[D_M:D_M + D_A, :], preferred_element_type=F32)
           + jnp.dot(hs_ref[...], wo_ref[D_M + D_A:, :], preferred_element_type=F32))
    x1 = x_ref[...] + _rms(mix, nw_ref[1:2, :])
    h2 = _rms(x1, nw_ref[2:3, :]).astype(BF16)
    ff = jnp.zeros_like(x1)
    for c0, n in FF_CHUNKS:
        g = jnp.dot(h2, wg_ref[:, c0:c0 + n], preferred_element_type=F32)
        u = jnp.dot(h2, wu_ref[:, c0:c0 + n], preferred_element_type=F32)
        act = (g * jax.nn.sigmoid(g) * u).astype(BF16)
        ff = ff + jnp.dot(act, wd_ref[c0:c0 + n, :], preferred_element_type=F32)
    o_ref[...] = x1 + _rms(ff, nw_ref[3:4, :])


def _outffn(x2, hm, ha, hs, nw4, wo, wg, wu, wd, S, tm):
    row = lambda n: pl.BlockSpec((tm, n), lambda i: (i, 0))
    return pl.pallas_call(
        _outffn_kernel,
        grid=(S // tm,),
        in_specs=[row(D_MODEL), row(D_M), row(D_A), row(D_S),
                  pl.BlockSpec((4, D_MODEL), lambda i: (0, 0)),
                  _resident((D_MODEL, D_MODEL), lambda i: (0, 0)),
                  _resident((D_MODEL, D_FF), lambda i: (0, 0)),
                  _resident((D_MODEL, D_FF), lambda i: (0, 0)),
                  _resident((D_FF, D_MODEL), lambda i: (0, 0))],
        out_specs=row(D_MODEL),
        out_shape=jax.ShapeDtypeStruct((S, D_MODEL), F32),
        compiler_params=_params(("arbitrary",)),
        name="outffn",
    )(x2, hm, ha, hs, nw4, wo, wg, wu, wd)


def _plan(S):
    assert S % (2 * DSA_KB) == 0
    return dict(tm=min(S, 512), lm=min(S, 256), ls=min(S, 256), tq=min(S, 256))


def _pad_lanes(v):
    return jnp.pad(v, [(0, 0)] * (v.ndim - 1) + [(0, LANES - v.shape[-1])])


def kernel(x, positions, norm_w, w_in, mlstm_gate_bias, mlstm_norm_w, conv_w, conv_b, dt_bias, a_log,
           d_skip, ssd_norm_w, w_out, w_gate, w_up, w_down):
    B, S, D = x.shape
    assert B == 1 and D == D_MODEL
    depth = w_in.shape[0]
    plan = _plan(S)

    o = np.cumsum([0, D_M, D_M, D_M, D_M, M_HEADS, M_HEADS, D_A, D_A, D_A, IDX_HEADS * IDX_DIM, IDX_DIM,
                   IDX_HEADS, D_S, CONV_DIM, S_HEADS])
    wp = jnp.concatenate([
        w_in[:, :, o[0]:o[4]], w_in[:, :, o[6]:o[10]], w_in[:, :, o[12]:o[13]], w_in[:, :, o[13]:o[14]],
        _pad_lanes(w_in[:, :, o[4]:o[6]]), _pad_lanes(w_in[:, :, o[10]:o[12]]), _pad_lanes(w_in[:, :, o[14]:o[15]]),
    ], axis=-1).astype(BF16)
    wo = w_out.astype(BF16)
    wg = w_gate.astype(BF16)
    wu = w_up.astype(BF16)
    wd = w_down.astype(BF16)
    gb = _pad_lanes(mlstm_gate_bias[:, None, :])
    dtb = _pad_lanes(dt_bias[:, None, :])
    alog = _pad_lanes(a_log[:, None, :])
    dsk = jnp.repeat(d_skip, HEAD_DIM, axis=-1)[:, None, :]

    cf, sa, sb = _rope_tables(positions.astype(I32), S)
    x2 = x.reshape(S, D)
    for l in range(depth):
        om, omg, os_, odt, aqT, iqT, iwT, ik, ak, avT = _inproj(
            x2, norm_w[l, 0:1], wp[l], cf, sa, sb, S, plan["tm"])
        hm = _mlstm(om, omg, gb[l], mlstm_norm_w[l][None, :], S, plan["lm"])
        hs = _ssd(os_, odt, conv_w[l], conv_b[l][None, :], dtb[l], alog[l], dsk[l], ssd_norm_w[l][None, :],
                  S, plan["ls"])
        ha = _dsa(aqT, iqT, iwT, ik, ak, avT, S, plan["tq"])
        x2 = _outffn(x2, hm, ha, hs, norm_w[l], wo[l], wg[l], wu[l], wd[l], S, plan["tm"])
    return x2.reshape(B, S, D)
```
